```python
import jax, jax.numpy as jnp
from jax import lax
import numpy as np

D_MODEL = 4096
BATCH = 1
SEQ = 16384
DEPTH = 4

CHUNK = 64
N_MIXERS = 3
ATTN_HEADS = 32
ATTN_HEAD_DIM = D_MODEL // ATTN_HEADS
LEFT_CHUNKS = 8
BAND = (LEFT_CHUNKS + 1) * CHUNK
MAX_REL = 128
CONV_WIDTH = 31
POOL_WINDOWS = (2, 4, 8, 16)
N_POOL_GROUPS = len(POOL_WINDOWS)
POOL_GROUP_WIDTH = D_MODEL // N_POOL_GROUPS
N_MEM = 256
MEM_HEADS = 4
MEM_HEAD_DIM = 128
MEM_INNER = MEM_HEADS * MEM_HEAD_DIM
N_GROUPS = 4
EXPERTS_PER_GROUP = 8
N_EXPERTS = N_GROUPS * EXPERTS_PER_GROUP
TOP_K = 2
D_EXPERT = 384
EXPERT_BLOCK = 128
LN_EPS = 1e-5
DEEPNORM_ALPHA = (2 * DEPTH) ** 0.25
DEEPNORM_BETA = (8 * DEPTH) ** -0.25

kernel_name = 'streaming_hybrid_interleaved_moe'


def layer_norm(x, g, b):
    xf = x.astype(jnp.float32)
    mu = jnp.mean(xf, axis=-1, keepdims=True)
    var = jnp.mean(jnp.square(xf - mu), axis=-1, keepdims=True)
    y = (xf - mu) * lax.rsqrt(var + LN_EPS) * g.astype(jnp.float32) + b.astype(jnp.float32)
    return y.astype(x.dtype)


def chunked_rel_attention(h, w_qkv, w_o, rel_table):
    B, S, D = h.shape
    n_chunks = S // CHUNK
    q, k, v = jnp.split(h @ w_qkv, 3, axis=-1)
    q = q.reshape(B, S, ATTN_HEADS, ATTN_HEAD_DIM) * (ATTN_HEAD_DIM ** -0.5)
    k = k.reshape(B, S, ATTN_HEADS, ATTN_HEAD_DIM)
    v = v.reshape(B, S, ATTN_HEADS, ATTN_HEAD_DIM)
    pad = LEFT_CHUNKS * CHUNK
    k_pad = jnp.pad(k, ((0, 0), (pad, 0), (0, 0), (0, 0)))
    v_pad = jnp.pad(v, ((0, 0), (pad, 0), (0, 0), (0, 0)))
    q_c = q.reshape(B, n_chunks, CHUNK, ATTN_HEADS, ATTN_HEAD_DIM).transpose(1, 0, 2, 3, 4)
    qq = jnp.arange(CHUNK)
    kk = jnp.arange(BAND)
    rel = qq[:, None] + pad - kk[None, :]
    rel_idx = jnp.clip(rel, -MAX_REL, MAX_REL) + MAX_REL
    bias = rel_table[:, rel_idx].astype(jnp.float32)

    def one_chunk(args):
        c, q_blk = args
        start = c * CHUNK
        k_blk = lax.dynamic_slice_in_dim(k_pad, start, BAND, axis=1)
        v_blk = lax.dynamic_slice_in_dim(v_pad, start, BAND, axis=1)
        s = jnp.einsum('bqhd,bkhd->bhqk', q_blk, k_blk).astype(jnp.float32) + bias[None]
        key_pos = start - pad + kk
        s = jnp.where((key_pos >= 0)[None, None, None, :], s, -1e30)
        p = jax.nn.softmax(s, axis=-1).astype(v_blk.dtype)
        return jnp.einsum('bhqk,bkhd->bqhd', p, v_blk)

    o = lax.map(one_chunk, (jnp.arange(n_chunks), q_c))
    o = o.transpose(1, 0, 2, 3, 4).reshape(B, S, D)
    return o @ w_o


def conformer_conv(h, w_in, b_in, w_dw, b_dw, ln_g, ln_b, w_out, b_out):
    D = h.shape[-1]
    a, g = jnp.split(h @ w_in + b_in, 2, axis=-1)
    u = a * jax.nn.sigmoid(g)
    u = lax.conv_general_dilated(u, w_dw[:, None, :], window_strides=(1,), padding=[(CONV_WIDTH - 1, 0)],
                                 dimension_numbers=('NWC', 'WIO', 'NWC'), feature_group_count=D) + b_dw
    u = jax.nn.silu(layer_norm(u, ln_g, ln_b))
    return u @ w_out + b_out


def multiscale_pool(h, w_pool, scale):
    B, S, D = h.shape
    xg = h.reshape(B, S, N_POOL_GROUPS, POOL_GROUP_WIDTH).astype(jnp.float32)
    cs = jnp.pad(jnp.cumsum(xg, axis=1), ((0, 0), (1, 0), (0, 0), (0, 0)))
    t = jnp.arange(S)[:, None]
    win = jnp.array(POOL_WINDOWS, dtype=jnp.int32)[None, :]
    lo = jnp.maximum(t + 1 - win, 0)
    cnt = (t + 1 - lo).astype(jnp.float32)
    grp = jnp.arange(N_POOL_GROUPS)[None, :]
    window_sum = cs[:, 1:] - cs[:, lo, grp]
    pooled = (window_sum / cnt[None, :, :, None] - xg).astype(h.dtype)
    y = jnp.einsum('bsgc,gcd->bsgd', pooled, w_pool).reshape(B, S, D)
    return y * scale


def memory_cross_attention(h, mem, w_q, w_kv, w_o):
    B, S, _ = h.shape
    q = (h @ w_q).reshape(B, S, MEM_HEADS, MEM_HEAD_DIM) * (MEM_HEAD_DIM ** -0.5)
    k, v = jnp.split(mem @ w_kv, 2, axis=-1)
    k = k.reshape(B, -1, MEM_HEADS, MEM_HEAD_DIM)
    v = v.reshape(B, -1, MEM_HEADS, MEM_HEAD_DIM)
    s = jnp.einsum('bshd,bmhd->bhsm', q, k).astype(jnp.float32)
    p = jax.nn.softmax(s, axis=-1).astype(v.dtype)
    o = jnp.einsum('bhsm,bmhd->bshd', p, v).reshape(B, S, MEM_INNER)
    return o @ w_o


def grouped_expert_ffn(hf, expert_ids, gates, w_gate, w_up, w_down):
    T, D = hf.shape
    n_assign = T * TOP_K
    flat_e = expert_ids.reshape(-1)
    flat_tok = jnp.repeat(jnp.arange(T, dtype=jnp.int32), TOP_K)
    flat_gate = gates.reshape(-1)
    order = jnp.argsort(flat_e)
    se = flat_e[order]
    counts = jnp.bincount(flat_e, length=N_EXPERTS)
    padded = (counts + EXPERT_BLOCK - 1) // EXPERT_BLOCK * EXPERT_BLOCK
    starts = jnp.cumsum(counts) - counts
    ends_p = jnp.cumsum(padded)
    pstarts = ends_p - padded
    dest = pstarts[se] + jnp.arange(n_assign, dtype=jnp.int32) - starts[se]
    n_blocks = -(-n_assign // EXPERT_BLOCK) + N_EXPERTS
    buf_len = n_blocks * EXPERT_BLOCK
    tok_buf = jnp.full((buf_len,), T, dtype=jnp.int32).at[dest].set(flat_tok[order])
    gate_buf = jnp.zeros((buf_len,), jnp.float32).at[dest].set(flat_gate[order])
    block_expert = jnp.minimum(jnp.searchsorted(ends_p, jnp.arange(n_blocks) * EXPERT_BLOCK, side='right'),
                               N_EXPERTS - 1)
    h_pad = jnp.concatenate([hf, jnp.zeros((1, D), hf.dtype)], axis=0)

    def run_block(args):
        tok, e = args
        xe = h_pad[tok]
        u = jax.nn.silu(xe @ w_gate[e]) * (xe @ w_up[e])
        return u @ w_down[e]

    yb = lax.map(run_block, (tok_buf.reshape(n_blocks, EXPERT_BLOCK), block_expert)).reshape(buf_len, D)
    y = jnp.zeros((T + 1, D), hf.dtype).at[tok_buf].add(yb * gate_buf[:, None].astype(yb.dtype))
    return y[:T]


def hierarchical_moe(h, w_group, b_group, w_router, b_router, w_gate, w_up, w_down):
    B, S, D = h.shape
    hf = h.reshape(B * S, D)
    g_prob = jax.nn.softmax((hf @ w_group + b_group).astype(jnp.float32), axis=-1)
    g_sel = jnp.argmax(g_prob, axis=-1).astype(jnp.int32)
    g_w = jnp.take_along_axis(g_prob, g_sel[:, None], axis=-1)
    e_logits_all = jnp.einsum('td,gde->tge', hf, w_router) + b_router
    e_logits = jnp.take_along_axis(e_logits_all, g_sel[:, None, None], axis=1)[:, 0].astype(jnp.float32)
    e_prob = jax.nn.softmax(e_logits, axis=-1)
    top_p, top_i = lax.top_k(e_prob, TOP_K)
    gates = top_p / jnp.sum(top_p, axis=-1, keepdims=True) * g_w
    expert_ids = g_sel[:, None] * EXPERTS_PER_GROUP + top_i.astype(jnp.int32)
    y = grouped_expert_ffn(hf, expert_ids, gates, w_gate, w_up, w_down)
    return y.reshape(B, S, D)


def setup_inputs(seed: int = 0) -> dict:
    key = jax.random.key(seed)
    keys = list(jax.random.split(key, 40))

    def nrm(shape, scale):
        return jax.random.normal(keys.pop(), shape, jnp.float32) * scale

    D = D_MODEL
    s = D ** -0.5
    beta = DEEPNORM_BETA
    n_a = len(range(0, DEPTH, N_MIXERS))
    n_b = len(range(1, DEPTH, N_MIXERS))
    n_c = len(range(2, DEPTH, N_MIXERS))
    return {
        'x': nrm((BATCH, SEQ, D), 1.0),
        'mem': nrm((BATCH, N_MEM, D), 1.0),
        'attn_w_qkv': jnp.concatenate([nrm((n_a, D, 2 * D), s), nrm((n_a, D, D), beta * s)], axis=-1),
        'attn_w_o': nrm((n_a, D, D), beta * s),
        'attn_rel_bias': nrm((n_a, ATTN_HEADS, 2 * MAX_REL + 1), 0.5),
        'conv_w_in': nrm((n_b, D, 2 * D), s),
        'conv_b_in': nrm((n_b, 2 * D), 0.02),
        'conv_w_dw': nrm((n_b, CONV_WIDTH, D), CONV_WIDTH ** -0.5),
        'conv_b_dw': nrm((n_b, D), 0.02),
        'conv_ln_g': 1.0 + nrm((n_b, D), 0.02),
        'conv_ln_b': nrm((n_b, D), 0.02),
        'conv_w_out': nrm((n_b, D, D), beta * s),
        'conv_b_out': nrm((n_b, D), 0.02),
        'pool_w': nrm((n_c, N_POOL_GROUPS, POOL_GROUP_WIDTH, POOL_GROUP_WIDTH), beta * POOL_GROUP_WIDTH ** -0.5),
        'pool_scale': 1.0 + nrm((n_c, D), 0.1),
        'mem_w_q': nrm((DEPTH, D, MEM_INNER), s),
        'mem_w_kv': jnp.concatenate([nrm((DEPTH, D, MEM_INNER), s), nrm((DEPTH, D, MEM_INNER), beta * s)], axis=-1),
        'mem_w_o': nrm((DEPTH, MEM_INNER, D), beta * MEM_INNER ** -0.5),
        'moe_w_group': nrm((DEPTH, D, N_GROUPS), s),
        'moe_b_group': nrm((DEPTH, N_GROUPS), 0.01),
        'moe_w_router': nrm((DEPTH, N_GROUPS, D, EXPERTS_PER_GROUP), s),
        'moe_b_router': nrm((DEPTH, N_GROUPS, EXPERTS_PER_GROUP), 0.01),
        'moe_w_gate': nrm((DEPTH, N_EXPERTS, D, D_EXPERT), s),
        'moe_w_up': nrm((DEPTH, N_EXPERTS, D, D_EXPERT), s),
        'moe_w_down': nrm((DEPTH, N_EXPERTS, D_EXPERT, D), beta * D_EXPERT ** -0.5),
        'ln_g': 1.0 + nrm((DEPTH, 3, D), 0.02),
        'ln_b': nrm((DEPTH, 3, D), 0.02),
    }


def reference(x, mem, attn_w_qkv, attn_w_o, attn_rel_bias, conv_w_in, conv_b_in, conv_w_dw, conv_b_dw,
              conv_ln_g, conv_ln_b, conv_w_out, conv_b_out, pool_w, pool_scale, mem_w_q, mem_w_kv, mem_w_o,
              moe_w_group, moe_b_group, moe_w_router, moe_b_router, moe_w_gate, moe_w_up, moe_w_down,
              ln_g, ln_b):
    for i in range(DEPTH):
        kind = i % N_MIXERS
        j = i // N_MIXERS
        if kind == 0:
            f = chunked_rel_attention(x, attn_w_qkv[j], attn_w_o[j], attn_rel_bias[j])
        elif kind == 1:
            f = conformer_conv(x, conv_w_in[j], conv_b_in[j], conv_w_dw[j], conv_b_dw[j], conv_ln_g[j],
                               conv_ln_b[j], conv_w_out[j], conv_b_out[j])
        else:
            f = multiscale_pool(x, pool_w[j], pool_scale[j])
        x = layer_norm(DEEPNORM_ALPHA * x + f, ln_g[i, 0], ln_b[i, 0])
        c = memory_cross_attention(x, mem, mem_w_q[i], mem_w_kv[i], mem_w_o[i])
        x = layer_norm(DEEPNORM_ALPHA * x + c, ln_g[i, 1], ln_b[i, 1])
        m = hierarchical_moe(x, moe_w_group[i], moe_b_group[i], moe_w_router[i], moe_b_router[i],
                             moe_w_gate[i], moe_w_up[i], moe_w_down[i])
        x = layer_norm(DEEPNORM_ALPHA * x + m, ln_g[i, 2], ln_b[i, 2])
    return x
```

```python
import functools

import jax
import jax.numpy as jnp
from jax import lax
from jax.experimental import pallas as pl
from jax.experimental.pallas import tpu as pltpu

F32 = jnp.float32
BF16 = jnp.bfloat16
U32 = jnp.uint32
I32 = jnp.int32

DEPTH = 4
N_MIXERS = 3
ATTN_HEADS = 32
CHUNK = 64
LEFT_CHUNKS = 8
POOL_WINDOWS = (2, 4, 8, 16)
MEM_HEADS = 4
TOP_K = 2
LN_EPS = 1e-5
DEEPNORM_ALPHA = (2 * DEPTH) ** 0.25
NEG_BIG = -1e30

V7X_VMEM_BYTES = 64 * 1024 * 1024
VMEM_LIMIT = V7X_VMEM_BYTES - 8 * 1024 * 1024
LANES = 128

ATTN_Q_TILE = 4 * CHUNK
ATTN_HEADS_PER_STEP = 8
MOE_BLOCK = 256
ROW_TILE = 256


def _params(n_grid_dims):
    return pltpu.CompilerParams(dimension_semantics=("arbitrary",) * n_grid_dims,
                                vmem_limit_bytes=VMEM_LIMIT)


def _ln_pieces(pieces, g_pieces, b_pieces):
    d = sum(p.shape[-1] for p in pieces)
    mu = sum(jnp.sum(p, axis=-1, keepdims=True) for p in pieces) * (1.0 / d)
    cen = [p - mu for p in pieces]
    var = sum(jnp.sum(c * c, axis=-1, keepdims=True) for c in cen) * (1.0 / d)
    inv = lax.rsqrt(var + LN_EPS)
    return [c * inv * g + b for c, g, b in zip(cen, g_pieces, b_pieces)]


def _round_bf16_bits(x):
    bits = lax.bitcast_convert_type(x, U32)
    return (bits + jnp.uint32(0x7FFF) + ((bits >> 16) & jnp.uint32(1))) & jnp.uint32(0xFFFF0000)


def _pack_halves(lo, hi):
    return (_round_bf16_bits(lo) >> 16) | _round_bf16_bits(hi)


def _unpack_halves(w):
    lo = lax.bitcast_convert_type(w << 16, F32)
    hi = lax.bitcast_convert_type(w & jnp.uint32(0xFFFF0000), F32)
    return lo, hi


def _mm_kernel(x_ref, w_ref, *rest, has_bias):
    if has_bias:
        b_ref, o_ref = rest
    else:
        (o_ref,) = rest
    acc = jnp.dot(x_ref[...], w_ref[...], preferred_element_type=F32)
    if has_bias:
        acc = acc + b_ref[...]
    o_ref[...] = acc.astype(o_ref.dtype)


def _matmul(x, w, bias=None, *, out_dtype=BF16, tm=512, tn=1024):
    m, k = x.shape
    n = w.shape[1]
    tm = min(tm, m)
    tn = min(tn, n)
    assert m % tm == 0 and n % tn == 0
    in_specs = [pl.BlockSpec((tm, k), lambda j, i: (i, 0)),
                pl.BlockSpec((k, tn), lambda j, i: (0, j))]
    args = [x, w]
    if bias is not None:
        in_specs.append(pl.BlockSpec((1, tn), lambda j, i: (0, j)))
        args.append(bias.reshape(1, n).astype(F32))
    return pl.pallas_call(
        functools.partial(_mm_kernel, has_bias=bias is not None),
        grid=(n // tn, m // tm),
        in_specs=in_specs,
        out_specs=pl.BlockSpec((tm, tn), lambda j, i: (i, j)),
        out_shape=jax.ShapeDtypeStruct((m, n), out_dtype),
        compiler_params=_params(2),
    )(*args)


def _attn_kernel(q_ref, k0_ref, k1_ref, k2_ref, v0_ref, v1_ref, v2_ref, bias_ref, o_ref, *, hb, dh, qt):
    i = pl.program_id(1)
    pens = [jnp.where(i + j >= 2, 0.0, NEG_BIG).astype(F32) for j in range(2)] + [None]
    scale = dh ** -0.5
    k_refs = (k0_ref, k1_ref, k2_ref)
    v_refs = (v0_ref, v1_ref, v2_ref)
    for h in range(hb):
        sl = slice(h * dh, (h + 1) * dh)
        q = (q_ref[:, sl].astype(F32) * scale).astype(BF16)
        s = []
        for j in range(3):
            sj = lax.dot_general(q, k_refs[j][:, sl], (((1,), (1,)), ((), ())), preferred_element_type=F32)
            sj = sj + bias_ref[h, :, j * qt:(j + 1) * qt]
            if pens[j] is not None:
                sj = sj + pens[j]
            s.append(sj)
        m = jnp.maximum(jnp.maximum(jnp.max(s[0], axis=-1, keepdims=True), jnp.max(s[1], axis=-1, keepdims=True)),
                        jnp.max(s[2], axis=-1, keepdims=True))
        p = [jnp.exp(sj - m) for sj in s]
        l = sum(jnp.sum(pj, axis=-1, keepdims=True) for pj in p)
        o = sum(jnp.dot(p[j].astype(BF16), v_refs[j][:, sl], preferred_element_type=F32) for j in range(3))
        o_ref[:, sl] = (o / l).astype(o_ref.dtype)


def _attn_bias(rel_table, qt):
    max_rel = (rel_table.shape[1] - 1) // 2
    qi = jnp.arange(qt)[:, None]
    kj = jnp.arange(3 * qt)[None, :]
    rel = qi + 2 * qt - kj
    idx = jnp.clip(rel, -max_rel, max_rel) + max_rel
    qc = qi // CHUNK
    kc = kj // CHUNK
    band = (kc >= qc) & (kc <= qc + LEFT_CHUNKS)
    return jnp.where(band[None], rel_table[:, idx].astype(F32), NEG_BIG)


def _chunked_attention(qkv, rel_table, d):
    t = qkv.shape[0]
    qt = ATTN_Q_TILE
    hb = ATTN_HEADS_PER_STEP
    dh = d // ATTN_HEADS
    assert LEFT_CHUNKS * CHUNK == 2 * qt and t % qt == 0 and ATTN_HEADS % hb == 0
    w = hb * dh
    ng = d // w
    bias = _attn_bias(rel_table, qt)

    def kv_spec(j, base):
        return pl.BlockSpec((qt, w), lambda g, i: (jnp.maximum(i - 2 + j, 0), base + g))

    return pl.pallas_call(
        functools.partial(_attn_kernel, hb=hb, dh=dh, qt=qt),
        grid=(ng, t // qt),
        in_specs=[pl.BlockSpec((qt, w), lambda g, i: (i, g))]
                 + [kv_spec(j, ng) for j in range(3)]
                 + [kv_spec(j, 2 * ng) for j in range(3)]
                 + [pl.BlockSpec((hb, qt, 3 * qt), lambda g, i: (g, 0, 0))],
        out_specs=pl.BlockSpec((qt, w), lambda g, i: (i, g)),
        out_shape=jax.ShapeDtypeStruct((t, d), BF16),
        compiler_params=_params(2),
    )(qkv, qkv, qkv, qkv, qkv, qkv, qkv, bias)


def _glu_kernel(x_ref, wa_ref, wg_ref, ba_ref, bg_ref, o_ref):
    x = x_ref[...]
    a = jnp.dot(x, wa_ref[...], preferred_element_type=F32) + ba_ref[...]
    g = jnp.dot(x, wg_ref[...], preferred_element_type=F32) + bg_ref[...]
    o_ref[...] = (a * jax.nn.sigmoid(g)).astype(o_ref.dtype)


def _glu_matmul(x, w, b, *, tm=512, tn=512):
    m, k = x.shape
    n = w.shape[1] // 2
    assert m % tm == 0 and n % tn == 0
    nb = n // tn
    b2 = b.reshape(1, 2 * n).astype(F32)
    return pl.pallas_call(
        _glu_kernel,
        grid=(nb, m // tm),
        in_specs=[pl.BlockSpec((tm, k), lambda j, i: (i, 0)),
                  pl.BlockSpec((k, tn), lambda j, i: (0, j)),
                  pl.BlockSpec((k, tn), lambda j, i: (0, nb + j)),
                  pl.BlockSpec((1, tn), lambda j, i: (0, j)),
                  pl.BlockSpec((1, tn), lambda j, i: (0, nb + j))],
        out_specs=pl.BlockSpec((tm, tn), lambda j, i: (i, j)),
        out_shape=jax.ShapeDtypeStruct((m, n), BF16),
        compiler_params=_params(2),
    )(x, w, w, b2, b2)


CONV_HALO = 32
CONV_ROWS = 64
CONV_COLS = 128


def _conv_kernel(u_ref, halo_ref, w_ref, bdw_ref, g_ref, b_ref, o_ref, ext_ref, acc_ref, *, width, tm):
    i = pl.program_id(0)
    d = u_ref.shape[1]
    halo = halo_ref[...].astype(F32)
    ext_ref[0:CONV_HALO, :] = jnp.where(i > 0, halo, 0.0)
    ext_ref[CONV_HALO:, :] = u_ref[...].astype(F32)
    off = CONV_HALO - (width - 1)

    def col_chunk(c, carry):
        c0 = pl.multiple_of(c * CONV_COLS, CONV_COLS)
        w = w_ref[:, pl.ds(c0, CONV_COLS)]
        for r0 in range(0, tm, CONV_ROWS):
            acc = jnp.zeros((CONV_ROWS, CONV_COLS), F32)
            for b in range(8):
                taps = [k for k in range(b, width, 8)]
                span = CONV_ROWS + (taps[-1] - b)
                v = ext_ref[pl.ds(r0 + off + b, span), pl.ds(c0, CONV_COLS)]
                for k in taps:
                    acc = acc + v[k - b:k - b + CONV_ROWS, :] * w[k:k + 1, :]
            acc_ref[r0:r0 + CONV_ROWS, pl.ds(c0, CONV_COLS)] = acc + bdw_ref[:, pl.ds(c0, CONV_COLS)]
        return carry

    lax.fori_loop(0, d // CONV_COLS, col_chunk, 0)
    (y,) = _ln_pieces([acc_ref[...]], [g_ref[...]], [b_ref[...]])
    o_ref[...] = (y * jax.nn.sigmoid(y)).astype(o_ref.dtype)


def _conv_ln_swish(u, w_dw, b_dw, ln_g, ln_b, *, tm=ROW_TILE):
    t, d = u.shape
    width = w_dw.shape[0]
    assert width - 1 <= CONV_HALO and t % tm == 0 and tm % CONV_HALO == 0 and d % CONV_COLS == 0
    hb = tm // CONV_HALO
    row = lambda a: a.reshape(1, d).astype(F32)
    return pl.pallas_call(
        functools.partial(_conv_kernel, width=width, tm=tm),
        grid=(t // tm,),
        in_specs=[pl.BlockSpec((tm, d), lambda i: (i, 0)),
                  pl.BlockSpec((CONV_HALO, d), lambda i: (jnp.maximum(i * hb - 1, 0), 0)),
                  pl.BlockSpec((width, d), lambda i: (0, 0)),
                  pl.BlockSpec((1, d), lambda i: (0, 0)),
                  pl.BlockSpec((1, d), lambda i: (0, 0)),
                  pl.BlockSpec((1, d), lambda i: (0, 0))],
        out_specs=pl.BlockSpec((tm, d), lambda i: (i, 0)),
        out_shape=jax.ShapeDtypeStruct((t, d), BF16),
        scratch_shapes=[pltpu.VMEM((tm + CONV_HALO, d), F32), pltpu.VMEM((tm, d), F32)],
        compiler_params=_params(1),
    )(u, u, w_dw.astype(F32), row(b_dw), row(ln_g), row(ln_b))


POOL_HALO = 16


def _pool_kernel(x_ref, halo_ref, w_ref, scale_ref, o_ref, ext_ref, *, tm, gw):
    i = pl.program_id(0)
    ext_ref[0:POOL_HALO, :] = jnp.where(i > 0, halo_ref[...], 0.0)
    ext_ref[POOL_HALO:, :] = x_ref[...]
    t_abs = i * tm + lax.broadcasted_iota(I32, (tm, 1), 0)
    for g, win in enumerate(POOL_WINDOWS):
        cols = slice(g * gw, (g + 1) * gw)
        acc = ext_ref[POOL_HALO:POOL_HALO + tm, cols]
        for j in range(1, win):
            acc = acc + ext_ref[POOL_HALO - j:POOL_HALO - j + tm, cols]
        cnt = jnp.minimum(t_abs + 1, win).astype(F32)
        pooled = acc / cnt - ext_ref[POOL_HALO:POOL_HALO + tm, cols]
        y = jnp.dot(pooled.astype(BF16), w_ref[g], preferred_element_type=F32)
        o_ref[:, cols] = (y * scale_ref[:, cols]).astype(o_ref.dtype)


def _multiscale_pool(x, w_pool, scale, *, tm=ROW_TILE):
    t, d = x.shape
    ng, gw, _ = w_pool.shape
    assert ng == len(POOL_WINDOWS) and max(POOL_WINDOWS) <= POOL_HALO and t % tm == 0
    hb = tm // POOL_HALO
    return pl.pallas_call(
        functools.partial(_pool_kernel, tm=tm, gw=gw),
        grid=(t // tm,),
        in_specs=[pl.BlockSpec((tm, d), lambda i: (i, 0)),
                  pl.BlockSpec((POOL_HALO, d), lambda i: (jnp.maximum(i * hb - 1, 0), 0)),
                  pl.BlockSpec((ng, gw, gw), lambda i: (0, 0, 0)),
                  pl.BlockSpec((1, d), lambda i: (0, 0))],
        out_specs=pl.BlockSpec((tm, d), lambda i: (i, 0)),
        out_shape=jax.ShapeDtypeStruct((t, d), BF16),
        scratch_shapes=[pltpu.VMEM((tm + POOL_HALO, d), F32)],
        compiler_params=_params(1),
    )(x, x, w_pool, scale.reshape(1, d).astype(F32))


def _post_kernel(x_ref, f_ref, g0_ref, b0_ref, g1_ref, b1_ref, wq_ref, kv_ref, wo_ref, wr_ref, br_ref,
                 x2_ref, x2p_ref, route_ref, *, n_groups, per_group):
    d = x_ref.shape[1]
    half = d // 2
    n_mem, inner2 = kv_ref.shape
    inner = inner2 // 2
    dh = inner // MEM_HEADS
    y = DEEPNORM_ALPHA * x_ref[...] + f_ref[...].astype(F32)
    (x1,) = _ln_pieces([y], [g0_ref[...]], [b0_ref[...]])
    q = jnp.dot(x1.astype(BF16), wq_ref[...], preferred_element_type=F32) * (dh ** -0.5)
    heads = []
    for h in range(MEM_HEADS):
        qh = q[:, h * dh:(h + 1) * dh].astype(BF16)
        kh = kv_ref[:, h * dh:(h + 1) * dh]
        vh = kv_ref[:, inner + h * dh:inner + (h + 1) * dh]
        s = lax.dot_general(qh, kh, (((1,), (1,)), ((), ())), preferred_element_type=F32)
        p = jnp.exp(s - jnp.max(s, axis=-1, keepdims=True))
        l = jnp.sum(p, axis=-1, keepdims=True)
        heads.append((jnp.dot(p.astype(BF16), vh, preferred_element_type=F32) / l).astype(BF16))
    o = jnp.concatenate(heads, axis=1)
    c = jnp.dot(o, wo_ref[...], preferred_element_type=F32)
    (x2,) = _ln_pieces([DEEPNORM_ALPHA * x1 + c], [g1_ref[...]], [b1_ref[...]])
    x2_ref[...] = x2
    x2p_ref[...] = _pack_halves(x2[:, :half], x2[:, half:])

    lg = jnp.dot(x2.astype(BF16), wr_ref[...], preferred_element_type=F32) + br_ref[...]
    lane = lax.broadcasted_iota(I32, lg.shape, 1).astype(F32)
    far = float(LANES)

    def first_lane_of_max(v):
        m = jnp.max(v, axis=-1, keepdims=True)
        return m, jnp.min(jnp.where(v == m, lane, far), axis=-1, keepdims=True)

    gl = jnp.where(lane < n_groups, lg, NEG_BIG)
    gm, gsel = first_lane_of_max(gl)
    gw = 1.0 / jnp.sum(jnp.exp(gl - gm), axis=-1, keepdims=True)
    lo = n_groups + gsel * per_group
    el = jnp.where((lane >= lo) & (lane < lo + per_group), lg, NEG_BIG)
    m1, i1 = first_lane_of_max(el)
    m2, i2 = first_lane_of_max(jnp.where(lane == i1, NEG_BIG, el))
    e2 = jnp.exp(m2 - m1)
    den = 1.0 + e2
    route = jnp.where(lane == 0, i1 - n_groups,
                      jnp.where(lane == 1, i2 - n_groups,
                                jnp.where(lane == 2, gw / den,
                                          jnp.where(lane == 3, gw * e2 / den, 0.0))))
    route_ref[...] = route


def _post_block(x, f, ln_g, ln_b, w_q, kv, w_o, w_r, b_r, n_groups, per_group, *, tm=ROW_TILE):
    t, d = x.shape
    inner = w_q.shape[1]
    n_mem = kv.shape[0]
    row = lambda a: a.reshape(1, d).astype(F32)
    full = lambda shape: pl.BlockSpec(shape, lambda i: (0,) * len(shape))
    tile = lambda w: pl.BlockSpec((tm, w), lambda i: (i, 0))
    return pl.pallas_call(
        functools.partial(_post_kernel, n_groups=n_groups, per_group=per_group),
        grid=(t // tm,),
        in_specs=[tile(d), tile(d), full((1, d)), full((1, d)), full((1, d)), full((1, d)),
                  full((d, inner)), full((n_mem, 2 * inner)), full((inner, d)), full((d, LANES)), full((1, LANES))],
        out_specs=[tile(d), tile(d // 2), tile(LANES)],
        out_shape=[jax.ShapeDtypeStruct((t, d), F32), jax.ShapeDtypeStruct((t, d // 2), U32),
                   jax.ShapeDtypeStruct((t, LANES), F32)],
        compiler_params=_params(1),
    )(x, f, row(ln_g[0]), row(ln_b[0]), row(ln_g[1]), row(ln_b[1]), w_q, kv, w_o, w_r, b_r)


def _moe_kernel(be_ref, tok_ref, nu_ref, xg_hbm, gate_ref, wg_ref, wu_ref, wd_ref, yb_ref, xbuf, sem, *, blk):
    i = pl.program_id(0)
    n_used = nu_ref[0]
    slot = i % 2
    half = xbuf.shape[2]

    def issue(j, s):
        base = j * blk

        def body(r, carry):
            tok = tok_ref[base + r]
            pltpu.make_async_copy(xg_hbm.at[pl.ds(tok, 1), :], xbuf.at[s, pl.ds(r, 1), :], sem.at[s]).start()
            return carry

        lax.fori_loop(0, blk, body, 0)

    @pl.when(i == 0)
    def _():
        issue(0, 0)

    @pl.when(i + 1 < n_used)
    def _():
        issue(i + 1, 1 - slot)

    @pl.when(i < n_used)
    def _():
        pltpu.make_async_copy(xg_hbm.at[pl.ds(0, blk), :], xbuf.at[slot], sem.at[slot]).wait()
        lo, hi = _unpack_halves(xbuf[slot])
        lo = lo.astype(BF16)
        hi = hi.astype(BF16)
        a = (jnp.dot(lo, wg_ref[0:half, :], preferred_element_type=F32)
             + jnp.dot(hi, wg_ref[half:, :], preferred_element_type=F32))
        u = (jnp.dot(lo, wu_ref[0:half, :], preferred_element_type=F32)
             + jnp.dot(hi, wu_ref[half:, :], preferred_element_type=F32))
        hmid = (a * jax.nn.sigmoid(a) * u).astype(BF16)
        y = jnp.dot(hmid, wd_ref[...], preferred_element_type=F32) * gate_ref[...]
        yb_ref[...] = _pack_halves(y[:, :half], y[:, half:])

    @pl.when(i >= n_used)
    def _():
        yb_ref[...] = jnp.zeros(yb_ref.shape, yb_ref.dtype)


def _moe_experts(xg, slot_tok, gate_buf, blk_expert, n_used, w_gate, w_up, w_down, *, blk=MOE_BLOCK):
    n_slots = slot_tok.shape[0]
    nb = n_slots // blk
    half = xg.shape[1]
    d = 2 * half
    de = w_gate.shape[2]
    grid_spec = pltpu.PrefetchScalarGridSpec(
        num_scalar_prefetch=3,
        grid=(nb,),
        in_specs=[pl.BlockSpec(memory_space=pl.ANY),
                  pl.BlockSpec((blk, 1), lambda i, be, tok, nu: (i, 0)),
                  pl.BlockSpec((None, d, de), lambda i, be, tok, nu: (be[i], 0, 0)),
                  pl.BlockSpec((None, d, de), lambda i, be, tok, nu: (be[i], 0, 0)),
                  pl.BlockSpec((None, de, d), lambda i, be, tok, nu: (be[i], 0, 0))],
        out_specs=pl.BlockSpec((blk, half), lambda i, be, tok, nu: (i, 0)),
        scratch_shapes=[pltpu.VMEM((2, blk, half), U32), pltpu.SemaphoreType.DMA((2,))],
    )
    return pl.pallas_call(
        functools.partial(_moe_kernel, blk=blk),
        grid_spec=grid_spec,
        out_shape=jax.ShapeDtypeStruct((n_slots, half), U32),
        compiler_params=_params(1),
    )(blk_expert, slot_tok, n_used, xg, gate_buf.reshape(n_slots, 1), w_gate, w_up, w_down)


def _combine_kernel(dest_ref, x2_ref, g_ref, b_ref, yb_hbm, x3_ref, x3b_ref, ybuf, sem, *, tm):
    i = pl.program_id(0)
    n = pl.num_programs(0)
    slot = i % 2
    half = ybuf.shape[3]

    def issue(j, s):
        base = j * tm * TOP_K

        def body(r, carry):
            for k in range(TOP_K):
                dst = dest_ref[base + TOP_K * r + k]
                pltpu.make_async_copy(yb_hbm.at[pl.ds(dst, 1), :], ybuf.at[s, k, pl.ds(r, 1), :], sem.at[s]).start()
            return carry

        lax.fori_loop(0, tm, body, 0)

    @pl.when(i == 0)
    def _():
        issue(0, 0)

    @pl.when(i + 1 < n)
    def _():
        issue(i + 1, 1 - slot)

    for k in range(TOP_K):
        pltpu.make_async_copy(yb_hbm.at[pl.ds(0, tm), :], ybuf.at[slot, k], sem.at[slot]).wait()
    lo, hi = _unpack_halves(ybuf[slot, 0])
    for k in range(1, TOP_K):
        lo_k, hi_k = _unpack_halves(ybuf[slot, k])
        lo = lo + lo_k
        hi = hi + hi_k
    y_lo = DEEPNORM_ALPHA * x2_ref[:, 0:half] + lo
    y_hi = DEEPNORM_ALPHA * x2_ref[:, half:] + hi
    o_lo, o_hi = _ln_pieces([y_lo, y_hi], [g_ref[:, 0:half], g_ref[:, half:]], [b_ref[:, 0:half], b_ref[:, half:]])
    x3_ref[:, 0:half] = o_lo
    x3_ref[:, half:] = o_hi
    x3b_ref[:, 0:half] = o_lo.astype(BF16)
    x3b_ref[:, half:] = o_hi.astype(BF16)


def _moe_combine(x2, yb, dest, ln_g, ln_b, *, tm=ROW_TILE):
    t, d = x2.shape
    half = d // 2
    row = lambda a: a.reshape(1, d).astype(F32)
    grid_spec = pltpu.PrefetchScalarGridSpec(
        num_scalar_prefetch=1,
        grid=(t // tm,),
        in_specs=[pl.BlockSpec((tm, d), lambda i, dst: (i, 0)),
                  pl.BlockSpec((1, d), lambda i, dst: (0, 0)),
                  pl.BlockSpec((1, d), lambda i, dst: (0, 0)),
                  pl.BlockSpec(memory_space=pl.ANY)],
        out_specs=[pl.BlockSpec((tm, d), lambda i, dst: (i, 0)), pl.BlockSpec((tm, d), lambda i, dst: (i, 0))],
        scratch_shapes=[pltpu.VMEM((2, TOP_K, tm, half), U32), pltpu.SemaphoreType.DMA((2,))],
    )
    return pl.pallas_call(
        functools.partial(_combine_kernel, tm=tm),
        grid_spec=grid_spec,
        out_shape=[jax.ShapeDtypeStruct((t, d), F32), jax.ShapeDtypeStruct((t, d), BF16)],
        compiler_params=_params(1),
    )(dest, x2, row(ln_g), row(ln_b), yb)


def _dispatch_plan(route, n_experts, blk):
    t = route.shape[0]
    ids = route[:, 0:TOP_K].astype(I32)
    gates = route[:, TOP_K:2 * TOP_K]
    flat_e = ids.reshape(-1)
    onehot = (flat_e[:, None] == jnp.arange(n_experts, dtype=I32)[None, :]).astype(I32)
    csum = jnp.cumsum(onehot, axis=0)
    rank = jnp.take_along_axis(csum, flat_e[:, None], axis=1)[:, 0] - 1
    counts = csum[-1]
    padded = (counts + blk - 1) // blk * blk
    ends = jnp.cumsum(padded)
    dest = (ends - padded)[flat_e] + rank
    nb = (t * TOP_K) // blk + n_experts
    n_slots = nb * blk
    flat_tok = jnp.arange(t * TOP_K, dtype=I32) // TOP_K
    slot_tok = jnp.zeros((n_slots,), I32).at[dest].set(flat_tok, unique_indices=True)
    gate_buf = jnp.zeros((n_slots,), F32).at[dest].set(gates.reshape(-1), unique_indices=True)
    blk_expert = jnp.minimum(jnp.searchsorted(ends, jnp.arange(nb, dtype=I32) * blk, side='right'),
                             n_experts - 1).astype(I32)
    n_used = (ends[-1] // blk).astype(I32).reshape(1)
    return dest.astype(I32), slot_tok, gate_buf, blk_expert, n_used


def kernel(x, mem, attn_w_qkv, attn_w_o, attn_rel_bias, conv_w_in, conv_b_in, conv_w_dw, conv_b_dw, conv_ln_g,
           conv_ln_b, conv_w_out, conv_b_out, pool_w, pool_scale, mem_w_q, mem_w_kv, mem_w_o, moe_w_group,
           moe_b_group, moe_w_router, moe_b_router, moe_w_gate, moe_w_up, moe_w_down, ln_g, ln_b):
    batch, seq, d = x.shape
    assert batch == 1 and ln_g.shape[0] == DEPTH
    n_groups = moe_w_group.shape[2]
    per_group = moe_w_router.shape[3]
    n_experts = n_groups * per_group
    assert n_groups + n_experts <= LANES

    xf = x.reshape(seq, d)
    xb = xf.astype(BF16)
    memb = mem.reshape(mem.shape[1], d).astype(BF16)

    for i in range(DEPTH):
        kind = i % N_MIXERS
        j = i // N_MIXERS
        if kind == 0:
            qkv = _matmul(xb, attn_w_qkv[j].astype(BF16))
            o = _chunked_attention(qkv, attn_rel_bias[j], d)
            f = _matmul(o, attn_w_o[j].astype(BF16))
        elif kind == 1:
            u = _glu_matmul(xb, conv_w_in[j].astype(BF16), conv_b_in[j])
            v = _conv_ln_swish(u, conv_w_dw[j], conv_b_dw[j], conv_ln_g[j], conv_ln_b[j])
            f = _matmul(v, conv_w_out[j].astype(BF16), conv_b_out[j])
        else:
            f = _multiscale_pool(xf, pool_w[j].astype(BF16), pool_scale[j])

        kv = _matmul(memb, mem_w_kv[i].astype(BF16), tm=memb.shape[0])
        w_r = jnp.concatenate([moe_w_group[i], moe_w_router[i].transpose(1, 0, 2).reshape(d, n_experts)], axis=1)
        w_r = jnp.pad(w_r, ((0, 0), (0, LANES - w_r.shape[1]))).astype(BF16)
        b_r = jnp.concatenate([moe_b_group[i], moe_b_router[i].reshape(n_experts)])
        b_r = jnp.pad(b_r, (0, LANES - b_r.shape[0])).reshape(1, LANES).astype(F32)
        x2, x2p, route = _post_block(xf, f, ln_g[i], ln_b[i], mem_w_q[i].astype(BF16), kv,
                                     mem_w_o[i].astype(BF16), w_r, b_r, n_groups, per_group)

        dest, slot_tok, gate_buf, blk_expert, n_used = _dispatch_plan(route, n_experts, MOE_BLOCK)
        yb = _moe_experts(x2p, slot_tok, gate_buf, blk_expert, n_used, moe_w_gate[i].astype(BF16),
                          moe_w_up[i].astype(BF16), moe_w_down[i].astype(BF16))
        xf, xb = _moe_combine(x2, yb, dest, ln_g[i, 2], ln_b[i, 2])

    return xf.reshape(batch, seq, d)
```

```python
import functools

import jax
import jax.numpy as jnp
from jax import lax
from jax.experimental import pallas as pl
from jax.experimental.pallas import tpu as pltpu

F32 = jnp.float32
BF16 = jnp.bfloat16
U32 = jnp.uint32
I32 = jnp.int32

DEPTH = 4
N_MIXERS = 3
ATTN_HEADS = 32
CHUNK = 64
LEFT_CHUNKS = 8
POOL_WINDOWS = (2, 4, 8, 16)
MEM_HEADS = 4
TOP_K = 2
LN_EPS = 1e-5
DEEPNORM_ALPHA = (2 * DEPTH) ** 0.25
NEG_BIG = -1e30

V7X_VMEM_BYTES = 64 * 1024 * 1024
VMEM_LIMIT = V7X_VMEM_BYTES - 8 * 1024 * 1024
LANES = 128

ATTN_Q_TILE = 4 * CHUNK
ATTN_HEADS_PER_STEP = 8
MOE_BLOCK = 256
MOE_UP_PIECES = 3
MOE_DOWN_PIECES = 4
ROW_TILE = 256


def _params(n_grid_dims):
    return pltpu.CompilerParams(dimension_semantics=("arbitrary",) * n_grid_dims,
                                vmem_limit_bytes=VMEM_LIMIT)


def _ln_pieces(pieces, g_pieces, b_pieces):
    d = sum(p.shape[-1] for p in pieces)
    mu = sum(jnp.sum(p, axis=-1, keepdims=True) for p in pieces) * (1.0 / d)
    cen = [p - mu for p in pieces]
    var = sum(jnp.sum(c * c, axis=-1, keepdims=True) for c in cen) * (1.0 / d)
    inv = lax.rsqrt(var + LN_EPS)
    return [c * inv * g + b for c, g, b in zip(cen, g_pieces, b_pieces)]


def _pack_halves(lo, hi):
    half_ulp = jnp.uint32(0x8000)
    lo_bits = (lax.bitcast_convert_type(lo, U32) + half_ulp) >> 16
    hi_bits = (lax.bitcast_convert_type(hi, U32) + half_ulp) & jnp.uint32(0xFFFF0000)
    return lo_bits | hi_bits


def _unpack_halves(w):
    lo = lax.bitcast_convert_type(w << 16, F32)
    hi = lax.bitcast_convert_type(w & jnp.uint32(0xFFFF0000), F32)
    return lo, hi


def _mm_kernel(x_ref, w_ref, *rest, has_bias):
    if has_bias:
        b_ref, o_ref = rest
    else:
        (o_ref,) = rest
    acc = jnp.dot(x_ref[...], w_ref[...], preferred_element_type=F32)
    if has_bias:
        acc = acc + b_ref[...]
    o_ref[...] = acc.astype(o_ref.dtype)


def _matmul(x, w, bias=None, *, out_dtype=BF16, tm=512, tn=1024):
    m, k = x.shape
    n = w.shape[1]
    tm = min(tm, m)
    tn = min(tn, n)
    assert m % tm == 0 and n % tn == 0
    in_specs = [pl.BlockSpec((tm, k), lambda j, i: (i, 0)),
                pl.BlockSpec((k, tn), lambda j, i: (0, j))]
    args = [x, w]
    if bias is not None:
        in_specs.append(pl.BlockSpec((1, tn), lambda j, i: (0, j)))
        args.append(bias.reshape(1, n).astype(F32))
    return pl.pallas_call(
        functools.partial(_mm_kernel, has_bias=bias is not None),
        grid=(n // tn, m // tm),
        in_specs=in_specs,
        out_specs=pl.BlockSpec((tm, tn), lambda j, i: (i, j)),
        out_shape=jax.ShapeDtypeStruct((m, n), out_dtype),
        compiler_params=_params(2),
    )(*args)


def _attn_kernel(q_ref, k0_ref, k1_ref, k2_ref, v0_ref, v1_ref, v2_ref, bias_ref, o_ref, *, hb, dh, qt):
    i = pl.program_id(1)
    pens = [jnp.where(i + j >= 2, 0.0, NEG_BIG).astype(F32) for j in range(2)] + [None]
    scale = dh ** -0.5
    k_refs = (k0_ref, k1_ref, k2_ref)
    v_refs = (v0_ref, v1_ref, v2_ref)
    for h in range(hb):
        sl = slice(h * dh, (h + 1) * dh)
        q = (q_ref[:, sl].astype(F32) * scale).astype(BF16)
        s = []
        for j in range(3):
            sj = lax.dot_general(q, k_refs[j][:, sl], (((1,), (1,)), ((), ())), preferred_element_type=F32)
            sj = sj + bias_ref[h, :, j * qt:(j + 1) * qt]
            if pens[j] is not None:
                sj = sj + pens[j]
            s.append(sj)
        m = jnp.maximum(jnp.maximum(jnp.max(s[0], axis=-1, keepdims=True), jnp.max(s[1], axis=-1, keepdims=True)),
                        jnp.max(s[2], axis=-1, keepdims=True))
        p = [jnp.exp(sj - m) for sj in s]
        l = sum(jnp.sum(pj, axis=-1, keepdims=True) for pj in p)
        o = sum(jnp.dot(p[j].astype(BF16), v_refs[j][:, sl], preferred_element_type=F32) for j in range(3))
        o_ref[:, sl] = (o / l).astype(o_ref.dtype)


def _attn_bias(rel_table, qt):
    n_heads = rel_table.shape[0]
    max_rel = (rel_table.shape[1] - 1) // 2
    kw = 3 * qt
    period = 4 * qt
    m = jnp.arange(period)
    off = jnp.where(m < kw, m, m - period)
    idx = jnp.clip(2 * qt - off, -max_rel, max_rel) + max_rel
    onehot = (idx[:, None] == jnp.arange(rel_table.shape[1])[None, :]).astype(F32)
    u = jnp.einsum('mr,hr->hm', onehot, rel_table.astype(F32), precision=lax.Precision.HIGHEST)
    flat = jnp.tile(u, (1, qt))[:, :qt * (period - 1)]
    toep = flat.reshape(n_heads, qt, period - 1)[:, :, :kw]
    qc = jnp.arange(qt)[:, None] // CHUNK
    kc = jnp.arange(kw)[None, :] // CHUNK
    band = (kc >= qc) & (kc <= qc + LEFT_CHUNKS)
    return jnp.where(band[None], toep, NEG_BIG)


def _chunked_attention(qkv, rel_table, d):
    t = qkv.shape[0]
    qt = ATTN_Q_TILE
    hb = ATTN_HEADS_PER_STEP
    dh = d // ATTN_HEADS
    assert LEFT_CHUNKS * CHUNK == 2 * qt and t % qt == 0 and ATTN_HEADS % hb == 0
    w = hb * dh
    ng = d // w
    bias = _attn_bias(rel_table, qt)

    def kv_spec(j, base):
        return pl.BlockSpec((qt, w), lambda g, i: (jnp.maximum(i - 2 + j, 0), base + g))

    return pl.pallas_call(
        functools.partial(_attn_kernel, hb=hb, dh=dh, qt=qt),
        grid=(ng, t // qt),
        in_specs=[pl.BlockSpec((qt, w), lambda g, i: (i, g))]
                 + [kv_spec(j, ng) for j in range(3)]
                 + [kv_spec(j, 2 * ng) for j in range(3)]
                 + [pl.BlockSpec((hb, qt, 3 * qt), lambda g, i: (g, 0, 0))],
        out_specs=pl.BlockSpec((qt, w), lambda g, i: (i, g)),
        out_shape=jax.ShapeDtypeStruct((t, d), BF16),
        compiler_params=_params(2),
    )(qkv, qkv, qkv, qkv, qkv, qkv, qkv, bias)


def _glu_kernel(x_ref, wa_ref, wg_ref, ba_ref, bg_ref, o_ref):
    x = x_ref[...]
    a = jnp.dot(x, wa_ref[...], preferred_element_type=F32) + ba_ref[...]
    g = jnp.dot(x, wg_ref[...], preferred_element_type=F32) + bg_ref[...]
    o_ref[...] = (a * jax.nn.sigmoid(g)).astype(o_ref.dtype)


def _glu_matmul(x, w, b, *, tm=512, tn=512):
    m, k = x.shape
    n = w.shape[1] // 2
    assert m % tm == 0 and n % tn == 0
    nb = n // tn
    b2 = b.reshape(1, 2 * n).astype(F32)
    return pl.pallas_call(
        _glu_kernel,
        grid=(nb, m // tm),
        in_specs=[pl.BlockSpec((tm, k), lambda j, i: (i, 0)),
                  pl.BlockSpec((k, tn), lambda j, i: (0, j)),
                  pl.BlockSpec((k, tn), lambda j, i: (0, nb + j)),
                  pl.BlockSpec((1, tn), lambda j, i: (0, j)),
                  pl.BlockSpec((1, tn), lambda j, i: (0, nb + j))],
        out_specs=pl.BlockSpec((tm, tn), lambda j, i: (i, j)),
        out_shape=jax.ShapeDtypeStruct((m, n), BF16),
        compiler_params=_params(2),
    )(x, w, w, b2, b2)


CONV_HALO = 32
CONV_ROWS = 64
CONV_COLS = 128


def _conv_kernel(u_ref, halo_ref, w_ref, bdw_ref, g_ref, b_ref, o_ref, ext_ref, acc_ref, *, width, tm):
    i = pl.program_id(0)
    d = u_ref.shape[1]
    halo = halo_ref[...].astype(F32)
    ext_ref[0:CONV_HALO, :] = jnp.where(i > 0, halo, 0.0)
    ext_ref[CONV_HALO:, :] = u_ref[...].astype(F32)
    off = CONV_HALO - (width - 1)

    def col_chunk(c, carry):
        c0 = pl.multiple_of(c * CONV_COLS, CONV_COLS)
        w = w_ref[:, pl.ds(c0, CONV_COLS)]
        for r0 in range(0, tm, CONV_ROWS):
            acc = jnp.zeros((CONV_ROWS, CONV_COLS), F32)
            for b in range(8):
                taps = [k for k in range(b, width, 8)]
                span = CONV_ROWS + (taps[-1] - b)
                v = ext_ref[pl.ds(r0 + off + b, span), pl.ds(c0, CONV_COLS)]
                for k in taps:
                    acc = acc + v[k - b:k - b + CONV_ROWS, :] * w[k:k + 1, :]
            acc_ref[r0:r0 + CONV_ROWS, pl.ds(c0, CONV_COLS)] = acc + bdw_ref[:, pl.ds(c0, CONV_COLS)]
        return carry

    lax.fori_loop(0, d // CONV_COLS, col_chunk, 0)
    (y,) = _ln_pieces([acc_ref[...]], [g_ref[...]], [b_ref[...]])
    o_ref[...] = (y * jax.nn.sigmoid(y)).astype(o_ref.dtype)


def _conv_ln_swish(u, w_dw, b_dw, ln_g, ln_b, *, tm=ROW_TILE):
    t, d = u.shape
    width = w_dw.shape[0]
    assert width - 1 <= CONV_HALO and t % tm == 0 and tm % CONV_HALO == 0 and d % CONV_COLS == 0
    hb = tm // CONV_HALO
    row = lambda a: a.reshape(1, d).astype(F32)
    return pl.pallas_call(
        functools.partial(_conv_kernel, width=width, tm=tm),
        grid=(t // tm,),
        in_specs=[pl.BlockSpec((tm, d), lambda i: (i, 0)),
                  pl.BlockSpec((CONV_HALO, d), lambda i: (jnp.maximum(i * hb - 1, 0), 0)),
                  pl.BlockSpec((width, d), lambda i: (0, 0)),
                  pl.BlockSpec((1, d), lambda i: (0, 0)),
                  pl.BlockSpec((1, d), lambda i: (0, 0)),
                  pl.BlockSpec((1, d), lambda i: (0, 0))],
        out_specs=pl.BlockSpec((tm, d), lambda i: (i, 0)),
        out_shape=jax.ShapeDtypeStruct((t, d), BF16),
        scratch_shapes=[pltpu.VMEM((tm + CONV_HALO, d), F32), pltpu.VMEM((tm, d), F32)],
        compiler_params=_params(1),
    )(u, u, w_dw.astype(F32), row(b_dw), row(ln_g), row(ln_b))


POOL_HALO = 16


def _pool_kernel(x_ref, halo_ref, w_ref, scale_ref, o_ref, ext_ref, *, tm, gw):
    i = pl.program_id(0)
    ext_ref[0:POOL_HALO, :] = jnp.where(i > 0, halo_ref[...], 0.0)
    ext_ref[POOL_HALO:, :] = x_ref[...]
    t_abs = i * tm + lax.broadcasted_iota(I32, (tm, 1), 0)
    for g, win in enumerate(POOL_WINDOWS):
        cols = slice(g * gw, (g + 1) * gw)
        acc = ext_ref[POOL_HALO:POOL_HALO + tm, cols]
        for j in range(1, win):
            acc = acc + ext_ref[POOL_HALO - j:POOL_HALO - j + tm, cols]
        cnt = jnp.minimum(t_abs + 1, win).astype(F32)
        pooled = acc / cnt - ext_ref[POOL_HALO:POOL_HALO + tm, cols]
        y = jnp.dot(pooled.astype(BF16), w_ref[g], preferred_element_type=F32)
        o_ref[:, cols] = (y * scale_ref[:, cols]).astype(o_ref.dtype)


def _multiscale_pool(x, w_pool, scale, *, tm=ROW_TILE):
    t, d = x.shape
    ng, gw, _ = w_pool.shape
    assert ng == len(POOL_WINDOWS) and max(POOL_WINDOWS) <= POOL_HALO and t % tm == 0
    hb = tm // POOL_HALO
    return pl.pallas_call(
        functools.partial(_pool_kernel, tm=tm, gw=gw),
        grid=(t // tm,),
        in_specs=[pl.BlockSpec((tm, d), lambda i: (i, 0)),
                  pl.BlockSpec((POOL_HALO, d), lambda i: (jnp.maximum(i * hb - 1, 0), 0)),
                  pl.BlockSpec((ng, gw, gw), lambda i: (0, 0, 0)),
                  pl.BlockSpec((1, d), lambda i: (0, 0))],
        out_specs=pl.BlockSpec((tm, d), lambda i: (i, 0)),
        out_shape=jax.ShapeDtypeStruct((t, d), BF16),
        scratch_shapes=[pltpu.VMEM((tm + POOL_HALO, d), F32)],
        compiler_params=_params(1),
    )(x, x, w_pool, scale.reshape(1, d).astype(F32))


def _post_kernel(x_ref, f_ref, g0_ref, b0_ref, g1_ref, b1_ref, wq_ref, kv_ref, wo_ref, wr_ref, br_ref,
                 x2_ref, x2p_ref, route_ref, counts_ref, *, n_groups, per_group):
    d = x_ref.shape[1]
    half = d // 2
    n_mem, inner2 = kv_ref.shape
    inner = inner2 // 2
    dh = inner // MEM_HEADS
    y = DEEPNORM_ALPHA * x_ref[...] + f_ref[...].astype(F32)
    (x1,) = _ln_pieces([y], [g0_ref[...]], [b0_ref[...]])
    q = jnp.dot(x1.astype(BF16), wq_ref[...], preferred_element_type=F32) * (dh ** -0.5)
    heads = []
    for h in range(MEM_HEADS):
        qh = q[:, h * dh:(h + 1) * dh].astype(BF16)
        kh = kv_ref[:, h * dh:(h + 1) * dh]
        vh = kv_ref[:, inner + h * dh:inner + (h + 1) * dh]
        s = lax.dot_general(qh, kh, (((1,), (1,)), ((), ())), preferred_element_type=F32)
        p = jnp.exp(s - jnp.max(s, axis=-1, keepdims=True))
        l = jnp.sum(p, axis=-1, keepdims=True)
        heads.append((jnp.dot(p.astype(BF16), vh, preferred_element_type=F32) / l).astype(BF16))
    o = jnp.concatenate(heads, axis=1)
    c = jnp.dot(o, wo_ref[...], preferred_element_type=F32)
    (x2,) = _ln_pieces([DEEPNORM_ALPHA * x1 + c], [g1_ref[...]], [b1_ref[...]])
    x2_ref[...] = x2
    x2p_ref[...] = _pack_halves(x2[:, :half], x2[:, half:])

    lg = jnp.dot(x2.astype(BF16), wr_ref[...], preferred_element_type=F32) + br_ref[...]
    lane = lax.broadcasted_iota(I32, lg.shape, 1).astype(F32)
    far = float(LANES)

    def first_lane_of_max(v):
        m = jnp.max(v, axis=-1, keepdims=True)
        return m, jnp.min(jnp.where(v == m, lane, far), axis=-1, keepdims=True)

    gl = jnp.where(lane < n_groups, lg, NEG_BIG)
    gm, gsel = first_lane_of_max(gl)
    gw = 1.0 / jnp.sum(jnp.exp(gl - gm), axis=-1, keepdims=True)
    lo = n_groups + gsel * per_group
    el = jnp.where((lane >= lo) & (lane < lo + per_group), lg, NEG_BIG)
    m1, i1 = first_lane_of_max(el)
    m2, i2 = first_lane_of_max(jnp.where(lane == i1, NEG_BIG, el))
    e2 = jnp.exp(m2 - m1)
    den = 1.0 + e2
    id1 = i1 - n_groups
    id2 = i2 - n_groups

    @pl.when(pl.program_id(0) == 0)
    def _():
        counts_ref[...] = jnp.zeros(counts_ref.shape, counts_ref.dtype)

    hot1 = lane == id1
    hot2 = lane == id2
    hot = jnp.where(hot1 | hot2, 1.0, 0.0)
    tm = lg.shape[0]
    strict_lower = (lax.broadcasted_iota(I32, (tm, tm), 0) > lax.broadcasted_iota(I32, (tm, tm), 1))
    before = jnp.dot(jnp.where(strict_lower, 1.0, 0.0).astype(BF16), hot.astype(BF16),
                     preferred_element_type=F32) + counts_ref[...]
    rank1 = jnp.sum(jnp.where(hot1, before, 0.0), axis=-1, keepdims=True)
    rank2 = jnp.sum(jnp.where(hot2, before, 0.0), axis=-1, keepdims=True)
    counts_ref[...] = counts_ref[...] + jnp.sum(hot, axis=0, keepdims=True)

    route = jnp.where(lane == 0, id1,
                      jnp.where(lane == 1, id2,
                                jnp.where(lane == 2, gw / den,
                                          jnp.where(lane == 3, gw * e2 / den,
                                                    jnp.where(lane == 4, rank1,
                                                              jnp.where(lane == 5, rank2, 0.0))))))
    route_ref[...] = route


def _post_block(x, f, ln_g, ln_b, w_q, kv, w_o, w_r, b_r, n_groups, per_group, *, tm=ROW_TILE):
    t, d = x.shape
    inner = w_q.shape[1]
    n_mem = kv.shape[0]
    row = lambda a: a.reshape(1, d).astype(F32)
    full = lambda shape: pl.BlockSpec(shape, lambda i: (0,) * len(shape))
    tile = lambda w: pl.BlockSpec((tm, w), lambda i: (i, 0))
    return pl.pallas_call(
        functools.partial(_post_kernel, n_groups=n_groups, per_group=per_group),
        grid=(t // tm,),
        in_specs=[tile(d), tile(d), full((1, d)), full((1, d)), full((1, d)), full((1, d)),
                  full((d, inner)), full((n_mem, 2 * inner)), full((inner, d)), full((d, LANES)), full((1, LANES))],
        out_specs=[tile(d), tile(d // 2), tile(LANES), full((1, LANES))],
        out_shape=[jax.ShapeDtypeStruct((t, d), F32), jax.ShapeDtypeStruct((t, d // 2), U32),
                   jax.ShapeDtypeStruct((t, LANES), F32), jax.ShapeDtypeStruct((1, LANES), F32)],
        compiler_params=_params(1),
    )(x, f, row(ln_g[0]), row(ln_b[0]), row(ln_g[1]), row(ln_b[1]), w_q, kv, w_o, w_r, b_r)


def _moe_kernel(be_ref, tok_ref, nu_ref, xg_hbm, wgu_ref, wd_ref, yb_ref, xbuf0, xbuf1, sem, *, blk):
    i = pl.program_id(0)
    n_used = nu_ref[0]
    bufs = (xbuf0, xbuf1)
    half = xbuf0.shape[1]
    de = wd_ref.shape[0]

    def row_copy(j, r, s):
        tok = tok_ref[j * blk + r]
        return pltpu.make_async_copy(xg_hbm.at[pl.ds(tok, 1), :], bufs[s].at[pl.ds(r, 1), :], sem.at[s])

    def wait_block(s):
        pltpu.make_async_copy(xg_hbm.at[pl.ds(0, blk), :], bufs[s], sem.at[s]).wait()

    @pl.when(i == 0)
    def _():
        def body(r, carry):
            row_copy(0, r, 0).start()
            return carry
        lax.fori_loop(0, blk, body, 0)

    def expert_block(s):
        wait_block(s)
        lo, hi = _unpack_halves(bufs[s][...])
        lo = lo.astype(BF16)
        hi = hi.astype(BF16)
        n_pieces = MOE_UP_PIECES + MOE_DOWN_PIECES
        rows_per_piece = -(-blk // n_pieces)
        issued = [0]

        def issue_some():
            stop = min(issued[0] + rows_per_piece, blk)
            for r in range(issued[0], stop):
                row_copy(i + 1, r, 1 - s).start()
            issued[0] = stop

        wu = 2 * de // MOE_UP_PIECES
        au = []
        for c in range(MOE_UP_PIECES):
            cols = slice(c * wu, (c + 1) * wu)
            au.append(jnp.dot(lo, wgu_ref[0:half, cols], preferred_element_type=F32)
                      + jnp.dot(hi, wgu_ref[half:, cols], preferred_element_type=F32))
            issue_some()
        au = jnp.concatenate(au, axis=1)
        a = au[:, 0:de]
        u = au[:, de:]
        hmid = (a * jax.nn.sigmoid(a) * u).astype(BF16)
        wdn = half // MOE_DOWN_PIECES
        for c in range(MOE_DOWN_PIECES):
            y_lo = jnp.dot(hmid, wd_ref[:, c * wdn:(c + 1) * wdn], preferred_element_type=F32)
            y_hi = jnp.dot(hmid, wd_ref[:, half + c * wdn:half + (c + 1) * wdn], preferred_element_type=F32)
            yb_ref[:, c * wdn:(c + 1) * wdn] = _pack_halves(y_lo, y_hi)
            issue_some()
        assert issued[0] == blk

    for s in range(2):
        pl.when((i < n_used) & (i % 2 == s))(functools.partial(expert_block, s))
        pl.when((i == n_used) & (i % 2 == s))(functools.partial(wait_block, s))

    @pl.when(i >= n_used)
    def _():
        yb_ref[...] = jnp.zeros(yb_ref.shape, yb_ref.dtype)


def _moe_experts(xg, slot_tok, blk_expert, n_used, w_gate_up, w_down, *, blk=MOE_BLOCK):
    n_slots = slot_tok.shape[0]
    nb = n_slots // blk
    half = xg.shape[1]
    d = 2 * half
    de = w_down.shape[1]
    grid_spec = pltpu.PrefetchScalarGridSpec(
        num_scalar_prefetch=3,
        grid=(nb,),
        in_specs=[pl.BlockSpec(memory_space=pl.ANY),
                  pl.BlockSpec((None, d, 2 * de), lambda i, be, tok, nu: (be[i], 0, 0)),
                  pl.BlockSpec((None, de, d), lambda i, be, tok, nu: (be[i], 0, 0))],
        out_specs=pl.BlockSpec((blk, half), lambda i, be, tok, nu: (i, 0)),
        scratch_shapes=[pltpu.VMEM((blk, half), U32), pltpu.VMEM((blk, half), U32), pltpu.SemaphoreType.DMA((2,))],
    )
    return pl.pallas_call(
        functools.partial(_moe_kernel, blk=blk),
        grid_spec=grid_spec,
        out_shape=jax.ShapeDtypeStruct((n_slots, half), U32),
        compiler_params=_params(1),
    )(blk_expert, slot_tok, n_used, xg, w_gate_up, w_down)


def _combine_kernel(dest_ref, x2_ref, route_ref, g_ref, b_ref, yb_hbm, x3_ref, x3b_ref, ybuf, sem, *, tm):
    i = pl.program_id(0)
    n = pl.num_programs(0)
    slot = i % 2
    half = ybuf.shape[3]

    def issue(j, s):
        base = j * tm * TOP_K

        def body(r, carry):
            for k in range(TOP_K):
                dst = dest_ref[base + TOP_K * r + k]
                pltpu.make_async_copy(yb_hbm.at[pl.ds(dst, 1), :], ybuf.at[s, k, pl.ds(r, 1), :], sem.at[s]).start()
            return carry

        lax.fori_loop(0, tm, body, 0)

    @pl.when(i == 0)
    def _():
        issue(0, 0)

    @pl.when(i + 1 < n)
    def _():
        issue(i + 1, 1 - slot)

    for k in range(TOP_K):
        pltpu.make_async_copy(yb_hbm.at[pl.ds(0, tm), :], ybuf.at[slot, k], sem.at[slot]).wait()
    lo = hi = None
    for k in range(TOP_K):
        gate = route_ref[:, TOP_K + k:TOP_K + k + 1]
        lo_k, hi_k = _unpack_halves(ybuf[slot, k])
        lo = lo_k * gate if lo is None else lo + lo_k * gate
        hi = hi_k * gate if hi is None else hi + hi_k * gate
    y_lo = DEEPNORM_ALPHA * x2_ref[:, 0:half] + lo
    y_hi = DEEPNORM_ALPHA * x2_ref[:, half:] + hi
    o_lo, o_hi = _ln_pieces([y_lo, y_hi], [g_ref[:, 0:half], g_ref[:, half:]], [b_ref[:, 0:half], b_ref[:, half:]])
    x3_ref[:, 0:half] = o_lo
    x3_ref[:, half:] = o_hi
    x3b_ref[:, 0:half] = o_lo.astype(BF16)
    x3b_ref[:, half:] = o_hi.astype(BF16)


def _moe_combine(x2, route, yb, dest, ln_g, ln_b, *, tm=ROW_TILE):
    t, d = x2.shape
    half = d // 2
    row = lambda a: a.reshape(1, d).astype(F32)
    grid_spec = pltpu.PrefetchScalarGridSpec(
        num_scalar_prefetch=1,
        grid=(t // tm,),
        in_specs=[pl.BlockSpec((tm, d), lambda i, dst: (i, 0)),
                  pl.BlockSpec((tm, LANES), lambda i, dst: (i, 0)),
                  pl.BlockSpec((1, d), lambda i, dst: (0, 0)),
                  pl.BlockSpec((1, d), lambda i, dst: (0, 0)),
                  pl.BlockSpec(memory_space=pl.ANY)],
        out_specs=[pl.BlockSpec((tm, d), lambda i, dst: (i, 0)), pl.BlockSpec((tm, d), lambda i, dst: (i, 0))],
        scratch_shapes=[pltpu.VMEM((2, TOP_K, tm, half), U32), pltpu.SemaphoreType.DMA((2,))],
    )
    return pl.pallas_call(
        functools.partial(_combine_kernel, tm=tm),
        grid_spec=grid_spec,
        out_shape=[jax.ShapeDtypeStruct((t, d), F32), jax.ShapeDtypeStruct((t, d), BF16)],
        compiler_params=_params(1),
    )(dest, x2, route, row(ln_g), row(ln_b), yb)


def _dispatch_plan(route, counts, n_experts, blk):
    t = route.shape[0]
    ids = route[:, 0:TOP_K].astype(I32)
    rank = route[:, 2 * TOP_K:3 * TOP_K].astype(I32)
    counts = counts[0, :n_experts].astype(I32)
    padded = (counts + blk - 1) // blk * blk
    ends = jnp.cumsum(padded)
    starts = ends - padded
    expert_iota = jnp.arange(n_experts, dtype=I32)
    seg_start = jnp.sum(jnp.where(ids[:, :, None] == expert_iota, starts, 0), axis=-1)
    dest = (seg_start + rank).reshape(-1)
    nb = (t * TOP_K) // blk + n_experts
    flat_tok = jnp.arange(t * TOP_K, dtype=I32) // TOP_K
    slot_tok = jnp.zeros((nb * blk,), I32).at[dest].set(flat_tok, unique_indices=True)
    blk_start = jnp.arange(nb, dtype=I32) * blk
    blk_expert = jnp.minimum(jnp.sum((ends[None, :] <= blk_start[:, None]).astype(I32), axis=1), n_experts - 1)
    n_used = (ends[-1] // blk).astype(I32).reshape(1)
    return dest, slot_tok, blk_expert, n_used


def kernel(x, mem, attn_w_qkv, attn_w_o, attn_rel_bias, conv_w_in, conv_b_in, conv_w_dw, conv_b_dw, conv_ln_g,
           conv_ln_b, conv_w_out, conv_b_out, pool_w, pool_scale, mem_w_q, mem_w_kv, mem_w_o, moe_w_group,
           moe_b_group, moe_w_router, moe_b_router, moe_w_gate, moe_w_up, moe_w_down, ln_g, ln_b):
    batch, seq, d = x.shape
    assert batch == 1 and ln_g.shape[0] == DEPTH
    n_groups = moe_w_group.shape[2]
    per_group = moe_w_router.shape[3]
    n_experts = n_groups * per_group
    assert n_groups + n_experts <= LANES

    xf = x.reshape(seq, d)
    xb = xf.astype(BF16)
    memb = mem.reshape(mem.shape[1], d).astype(BF16)

    for i in range(DEPTH):
        kind = i % N_MIXERS
        j = i // N_MIXERS
        if kind == 0:
            qkv = _matmul(xb, attn_w_qkv[j].astype(BF16))
            o = _chunked_attention(qkv, attn_rel_bias[j], d)
            f = _matmul(o, attn_w_o[j].astype(BF16))
        elif kind == 1:
            u = _glu_matmul(xb, conv_w_in[j].astype(BF16), conv_b_in[j])
            v = _conv_ln_swish(u, conv_w_dw[j], conv_b_dw[j], conv_ln_g[j], conv_ln_b[j])
            f = _matmul(v, conv_w_out[j].astype(BF16), conv_b_out[j])
        else:
            f = _multiscale_pool(xf, pool_w[j].astype(BF16), pool_scale[j])

        kv = _matmul(memb, mem_w_kv[i].astype(BF16), tm=memb.shape[0])
        w_r = jnp.concatenate([moe_w_group[i], moe_w_router[i].transpose(1, 0, 2).reshape(d, n_experts)], axis=1)
        w_r = jnp.pad(w_r, ((0, 0), (0, LANES - w_r.shape[1]))).astype(BF16)
        b_r = jnp.concatenate([moe_b_group[i], moe_b_router[i].reshape(n_experts)])
        b_r = jnp.pad(b_r, (0, LANES - b_r.shape[0])).reshape(1, LANES).astype(F32)
        x2, x2p, route, counts = _post_block(xf, f, ln_g[i], ln_b[i], mem_w_q[i].astype(BF16), kv,
                                             mem_w_o[i].astype(BF16), w_r, b_r, n_groups, per_group)

        dest, slot_tok, blk_expert, n_used = _dispatch_plan(route, counts, n_experts, MOE_BLOCK)
        w_gate_up = jnp.concatenate([moe_w_gate[i], moe_w_up[i]], axis=-1).astype(BF16)
        yb = _moe_experts(x2p, slot_tok, blk_expert, n_used, w_gate_up, moe_w_down[i].astype(BF16))
        xf, xb = _moe_combine(x2, route, yb, dest, ln_g[i, 2], ln_b[i, 2])

    return xf.reshape(batch, seq, d)
```

```python
import functools

import jax
import jax.numpy as jnp
from jax import lax
from jax.experimental import pallas as pl
from jax.experimental.pallas import tpu as pltpu

F32 = jnp.float32
BF16 = jnp.bfloat16
U32 = jnp.uint32
I32 = jnp.int32

DEPTH = 4
N_MIXERS = 3
ATTN_HEADS = 32
CHUNK = 64
LEFT_CHUNKS = 8
POOL_WINDOWS = (2, 4, 8, 16)
MEM_HEADS = 4
TOP_K = 2
LN_EPS = 1e-5
DEEPNORM_ALPHA = (2 * DEPTH) ** 0.25
NEG_BIG = -1e30

V7X_VMEM_BYTES = 64 * 1024 * 1024
VMEM_LIMIT = V7X_VMEM_BYTES - 8 * 1024 * 1024
LANES = 128

ATTN_Q_TILE = 4 * CHUNK
ATTN_HEADS_PER_STEP = 8
MOE_BLOCK = 256
MOE_UP_PIECES = 3
MOE_DOWN_PIECES = 4
ROW_TILE = 256


def _params(n_grid_dims):
    return pltpu.CompilerParams(dimension_semantics=("arbitrary",) * n_grid_dims,
                                vmem_limit_bytes=VMEM_LIMIT)


def _ln_pieces(pieces, g_pieces, b_pieces):
    d = sum(p.shape[-1] for p in pieces)
    mu = sum(jnp.sum(p, axis=-1, keepdims=True) for p in pieces) * (1.0 / d)
    cen = [p - mu for p in pieces]
    var = sum(jnp.sum(c * c, axis=-1, keepdims=True) for c in cen) * (1.0 / d)
    inv = lax.rsqrt(var + LN_EPS)
    return [c * inv * g + b for c, g, b in zip(cen, g_pieces, b_pieces)]


def _pack_halves(lo, hi):
    half_ulp = jnp.uint32(0x8000)
    lo_bits = (lax.bitcast_convert_type(lo, U32) + half_ulp) >> 16
    hi_bits = (lax.bitcast_convert_type(hi, U32) + half_ulp) & jnp.uint32(0xFFFF0000)
    return lo_bits | hi_bits


def _unpack_halves(w):
    lo = lax.bitcast_convert_type(w << 16, F32)
    hi = lax.bitcast_convert_type(w & jnp.uint32(0xFFFF0000), F32)
    return lo, hi


GATHER_PITCH = 24


def _store_slabs(ref, first_slab, packed):
    rows = packed.shape[0]
    n_slabs = ref.shape[0] // rows
    for q in range(packed.shape[1] // LANES):
        ref[pl.ds(first_slab + q, rows, stride=n_slabs), :] = packed[:, q * LANES:(q + 1) * LANES]


def _load_slab_column(ref_2d, s, rows):
    return ref_2d[pl.ds(s, rows, stride=GATHER_PITCH), :]


def _mm_kernel(x_ref, w_ref, *rest, has_bias):
    if has_bias:
        b_ref, o_ref = rest
    else:
        (o_ref,) = rest
    acc = jnp.dot(x_ref[...], w_ref[...], preferred_element_type=F32)
    if has_bias:
        acc = acc + b_ref[...]
    o_ref[...] = acc.astype(o_ref.dtype)


def _matmul(x, w, bias=None, *, out_dtype=BF16, tm=512, tn=1024):
    m, k = x.shape
    n = w.shape[1]
    tm = min(tm, m)
    tn = min(tn, n)
    assert m % tm == 0 and n % tn == 0
    in_specs = [pl.BlockSpec((tm, k), lambda j, i: (i, 0)),
                pl.BlockSpec((k, tn), lambda j, i: (0, j))]
    args = [x, w]
    if bias is not None:
        in_specs.append(pl.BlockSpec((1, tn), lambda j, i: (0, j)))
        args.append(bias.reshape(1, n).astype(F32))
    return pl.pallas_call(
        functools.partial(_mm_kernel, has_bias=bias is not None),
        grid=(n // tn, m // tm),
        in_specs=in_specs,
        out_specs=pl.BlockSpec((tm, tn), lambda j, i: (i, j)),
        out_shape=jax.ShapeDtypeStruct((m, n), out_dtype),
        compiler_params=_params(2),
    )(*args)


def _attn_kernel(q_ref, k0_ref, k1_ref, k2_ref, v0_ref, v1_ref, v2_ref, u_ref, o_ref, bias_ref, *, hb, dh, qt):
    i = pl.program_id(1)

    @pl.when(i == 0)
    def _():
        shift = CHUNK.bit_length() - 1
        qc = lax.shift_right_logical(lax.broadcasted_iota(I32, (qt, 3 * qt), 0), shift)
        kc = lax.shift_right_logical(lax.broadcasted_iota(I32, (qt, 3 * qt), 1), shift)
        band = (kc >= qc) & (kc <= qc + LEFT_CHUNKS)
        for h in range(hb):
            rows = jnp.broadcast_to(u_ref[h:h + 1, :], (qt, u_ref.shape[1]))
            toep = pltpu.roll(rows, 0, 1, stride=1, stride_axis=0)
            bias_ref[h] = jnp.where(band, toep[:, 0:3 * qt], NEG_BIG)

    pens = [jnp.where(i + j >= 2, 0.0, NEG_BIG).astype(F32) for j in range(2)] + [None]
    scale = dh ** -0.5
    k_refs = (k0_ref, k1_ref, k2_ref)
    v_refs = (v0_ref, v1_ref, v2_ref)
    for h in range(hb):
        sl = slice(h * dh, (h + 1) * dh)
        q = (q_ref[:, sl].astype(F32) * scale).astype(BF16)
        s = []
        for j in range(3):
            sj = lax.dot_general(q, k_refs[j][:, sl], (((1,), (1,)), ((), ())), preferred_element_type=F32)
            sj = sj + bias_ref[h, :, j * qt:(j + 1) * qt]
            if pens[j] is not None:
                sj = sj + pens[j]
            s.append(sj)
        m = jnp.maximum(jnp.maximum(jnp.max(s[0], axis=-1, keepdims=True), jnp.max(s[1], axis=-1, keepdims=True)),
                        jnp.max(s[2], axis=-1, keepdims=True))
        p = [jnp.exp(sj - m) for sj in s]
        l = sum(jnp.sum(pj, axis=-1, keepdims=True) for pj in p)
        o = sum(jnp.dot(p[j].astype(BF16), v_refs[j][:, sl], preferred_element_type=F32) for j in range(3))
        o_ref[:, sl] = (o / l).astype(o_ref.dtype)


def _attn_offset_bias(rel_table, qt):
    max_rel = (rel_table.shape[1] - 1) // 2
    kw = 3 * qt
    period = 4 * qt
    m = jnp.arange(period)
    off = jnp.where(m < kw, m, m - period)
    idx = jnp.clip(2 * qt - off, -max_rel, max_rel) + max_rel
    return jnp.take(rel_table.astype(F32), idx, axis=1)


def _chunked_attention(qkv, rel_table, d):
    t = qkv.shape[0]
    qt = ATTN_Q_TILE
    hb = ATTN_HEADS_PER_STEP
    dh = d // ATTN_HEADS
    assert LEFT_CHUNKS * CHUNK == 2 * qt and t % qt == 0 and ATTN_HEADS % hb == 0
    w = hb * dh
    ng = d // w
    u = _attn_offset_bias(rel_table, qt)

    def kv_spec(j, base):
        return pl.BlockSpec((qt, w), lambda g, i: (jnp.maximum(i - 2 + j, 0), base + g))

    return pl.pallas_call(
        functools.partial(_attn_kernel, hb=hb, dh=dh, qt=qt),
        grid=(ng, t // qt),
        in_specs=[pl.BlockSpec((qt, w), lambda g, i: (i, g))]
                 + [kv_spec(j, ng) for j in range(3)]
                 + [kv_spec(j, 2 * ng) for j in range(3)]
                 + [pl.BlockSpec((hb, u.shape[1]), lambda g, i: (g, 0))],
        out_specs=pl.BlockSpec((qt, w), lambda g, i: (i, g)),
        out_shape=jax.ShapeDtypeStruct((t, d), BF16),
        scratch_shapes=[pltpu.VMEM((hb, qt, 3 * qt), F32)],
        compiler_params=_params(2),
    )(qkv, qkv, qkv, qkv, qkv, qkv, qkv, u)


def _glu_kernel(x_ref, wa_ref, wg_ref, ba_ref, bg_ref, o_ref):
    x = x_ref[...]
    a = jnp.dot(x, wa_ref[...], preferred_element_type=F32) + ba_ref[...]
    g = jnp.dot(x, wg_ref[...], preferred_element_type=F32) + bg_ref[...]
    o_ref[...] = (a * jax.nn.sigmoid(g)).astype(o_ref.dtype)


def _glu_matmul(x, w, b, *, tm=512, tn=512):
    m, k = x.shape
    n = w.shape[1] // 2
    assert m % tm == 0 and n % tn == 0
    nb = n // tn
    b2 = b.reshape(1, 2 * n).astype(F32)
    return pl.pallas_call(
        _glu_kernel,
        grid=(nb, m // tm),
        in_specs=[pl.BlockSpec((tm, k), lambda j, i: (i, 0)),
                  pl.BlockSpec((k, tn), lambda j, i: (0, j)),
                  pl.BlockSpec((k, tn), lambda j, i: (0, nb + j)),
                  pl.BlockSpec((1, tn), lambda j, i: (0, j)),
                  pl.BlockSpec((1, tn), lambda j, i: (0, nb + j))],
        out_specs=pl.BlockSpec((tm, tn), lambda j, i: (i, j)),
        out_shape=jax.ShapeDtypeStruct((m, n), BF16),
        compiler_params=_params(2),
    )(x, w, w, b2, b2)


CONV_HALO = 32
CONV_ROWS = 64
CONV_COLS = 128


def _conv_kernel(u_ref, halo_ref, w_ref, bdw_ref, g_ref, b_ref, o_ref, ext_ref, acc_ref, *, width, tm):
    i = pl.program_id(0)
    d = u_ref.shape[1]
    halo = halo_ref[...].astype(F32)
    ext_ref[0:CONV_HALO, :] = jnp.where(i > 0, halo, 0.0)
    ext_ref[CONV_HALO:, :] = u_ref[...].astype(F32)
    off = CONV_HALO - (width - 1)

    def col_chunk(c, carry):
        c0 = pl.multiple_of(c * CONV_COLS, CONV_COLS)
        w = w_ref[:, pl.ds(c0, CONV_COLS)]
        for r0 in range(0, tm, CONV_ROWS):
            acc = jnp.zeros((CONV_ROWS, CONV_COLS), F32)
            for b in range(8):
                taps = [k for k in range(b, width, 8)]
                span = CONV_ROWS + (taps[-1] - b)
                v = ext_ref[pl.ds(r0 + off + b, span), pl.ds(c0, CONV_COLS)]
                for k in taps:
                    acc = acc + v[k - b:k - b + CONV_ROWS, :] * w[k:k + 1, :]
            acc_ref[r0:r0 + CONV_ROWS, pl.ds(c0, CONV_COLS)] = acc + bdw_ref[:, pl.ds(c0, CONV_COLS)]
        return carry

    lax.fori_loop(0, d // CONV_COLS, col_chunk, 0)
    (y,) = _ln_pieces([acc_ref[...]], [g_ref[...]], [b_ref[...]])
    o_ref[...] = (y * jax.nn.sigmoid(y)).astype(o_ref.dtype)


def _conv_ln_swish(u, w_dw, b_dw, ln_g, ln_b, *, tm=ROW_TILE):
    t, d = u.shape
    width = w_dw.shape[0]
    assert width - 1 <= CONV_HALO and t % tm == 0 and tm % CONV_HALO == 0 and d % CONV_COLS == 0
    hb = tm // CONV_HALO
    row = lambda a: a.reshape(1, d).astype(F32)
    return pl.pallas_call(
        functools.partial(_conv_kernel, width=width, tm=tm),
        grid=(t // tm,),
        in_specs=[pl.BlockSpec((tm, d), lambda i: (i, 0)),
                  pl.BlockSpec((CONV_HALO, d), lambda i: (jnp.maximum(i * hb - 1, 0), 0)),
                  pl.BlockSpec((width, d), lambda i: (0, 0)),
                  pl.BlockSpec((1, d), lambda i: (0, 0)),
                  pl.BlockSpec((1, d), lambda i: (0, 0)),
                  pl.BlockSpec((1, d), lambda i: (0, 0))],
        out_specs=pl.BlockSpec((tm, d), lambda i: (i, 0)),
        out_shape=jax.ShapeDtypeStruct((t, d), BF16),
        scratch_shapes=[pltpu.VMEM((tm + CONV_HALO, d), F32), pltpu.VMEM((tm, d), F32)],
        compiler_params=_params(1),
    )(u, u, w_dw.astype(F32), row(b_dw), row(ln_g), row(ln_b))


POOL_HALO = 16


def _pool_kernel(x_ref, halo_ref, w_ref, scale_ref, o_ref, ext_ref, *, tm, gw):
    i = pl.program_id(0)
    ext_ref[0:POOL_HALO, :] = jnp.where(i > 0, halo_ref[...], 0.0)
    ext_ref[POOL_HALO:, :] = x_ref[...]
    t_abs = i * tm + lax.broadcasted_iota(I32, (tm, 1), 0)
    for g, win in enumerate(POOL_WINDOWS):
        cols = slice(g * gw, (g + 1) * gw)
        acc = ext_ref[POOL_HALO:POOL_HALO + tm, cols]
        for j in range(1, win):
            acc = acc + ext_ref[POOL_HALO - j:POOL_HALO - j + tm, cols]
        cnt = jnp.minimum(t_abs + 1, win).astype(F32)
        pooled = acc / cnt - ext_ref[POOL_HALO:POOL_HALO + tm, cols]
        y = jnp.dot(pooled.astype(BF16), w_ref[g], preferred_element_type=F32)
        o_ref[:, cols] = (y * scale_ref[:, cols]).astype(o_ref.dtype)


def _multiscale_pool(x, w_pool, scale, *, tm=ROW_TILE):
    t, d = x.shape
    ng, gw, _ = w_pool.shape
    assert ng == len(POOL_WINDOWS) and max(POOL_WINDOWS) <= POOL_HALO and t % tm == 0
    hb = tm // POOL_HALO
    return pl.pallas_call(
        functools.partial(_pool_kernel, tm=tm, gw=gw),
        grid=(t // tm,),
        in_specs=[pl.BlockSpec((tm, d), lambda i: (i, 0)),
                  pl.BlockSpec((POOL_HALO, d), lambda i: (jnp.maximum(i * hb - 1, 0), 0)),
                  pl.BlockSpec((ng, gw, gw), lambda i: (0, 0, 0)),
                  pl.BlockSpec((1, d), lambda i: (0, 0))],
        out_specs=pl.BlockSpec((tm, d), lambda i: (i, 0)),
        out_shape=jax.ShapeDtypeStruct((t, d), BF16),
        scratch_shapes=[pltpu.VMEM((tm + POOL_HALO, d), F32)],
        compiler_params=_params(1),
    )(x, x, w_pool, scale.reshape(1, d).astype(F32))


def _post_kernel(x_ref, f_ref, g0_ref, b0_ref, g1_ref, b1_ref, wq_ref, kv_ref, wo_ref, wr_ref, br_ref,
                 x2_ref, x2p_ref, route_ref, counts_ref, *, n_groups, per_group):
    d = x_ref.shape[1]
    half = d // 2
    n_mem, inner2 = kv_ref.shape
    inner = inner2 // 2
    dh = inner // MEM_HEADS
    y = DEEPNORM_ALPHA * x_ref[...] + f_ref[...].astype(F32)
    (x1,) = _ln_pieces([y], [g0_ref[...]], [b0_ref[...]])
    q = jnp.dot(x1.astype(BF16), wq_ref[...], preferred_element_type=F32) * (dh ** -0.5)
    heads = []
    for h in range(MEM_HEADS):
        qh = q[:, h * dh:(h + 1) * dh].astype(BF16)
        kh = kv_ref[:, h * dh:(h + 1) * dh]
        vh = kv_ref[:, inner + h * dh:inner + (h + 1) * dh]
        s = lax.dot_general(qh, kh, (((1,), (1,)), ((), ())), preferred_element_type=F32)
        p = jnp.exp(s - jnp.max(s, axis=-1, keepdims=True))
        l = jnp.sum(p, axis=-1, keepdims=True)
        heads.append((jnp.dot(p.astype(BF16), vh, preferred_element_type=F32) / l).astype(BF16))
    o = jnp.concatenate(heads, axis=1)
    c = jnp.dot(o, wo_ref[...], preferred_element_type=F32)
    (x2,) = _ln_pieces([DEEPNORM_ALPHA * x1 + c], [g1_ref[...]], [b1_ref[...]])
    x2_ref[...] = x2
    _store_slabs(x2p_ref, 0, _pack_halves(x2[:, :half], x2[:, half:]))

    lg = jnp.dot(x2.astype(BF16), wr_ref[...], preferred_element_type=F32) + br_ref[...]
    lane = lax.broadcasted_iota(I32, lg.shape, 1).astype(F32)
    far = float(LANES)

    def first_lane_of_max(v):
        m = jnp.max(v, axis=-1, keepdims=True)
        return m, jnp.min(jnp.where(v == m, lane, far), axis=-1, keepdims=True)

    gl = jnp.where(lane < n_groups, lg, NEG_BIG)
    gm, gsel = first_lane_of_max(gl)
    gw = 1.0 / jnp.sum(jnp.exp(gl - gm), axis=-1, keepdims=True)
    lo = n_groups + gsel * per_group
    el = jnp.where((lane >= lo) & (lane < lo + per_group), lg, NEG_BIG)
    m1, i1 = first_lane_of_max(el)
    m2, i2 = first_lane_of_max(jnp.where(lane == i1, NEG_BIG, el))
    e2 = jnp.exp(m2 - m1)
    den = 1.0 + e2
    id1 = i1 - n_groups
    id2 = i2 - n_groups

    @pl.when(pl.program_id(0) == 0)
    def _():
        counts_ref[...] = jnp.zeros(counts_ref.shape, counts_ref.dtype)

    hot1 = lane == id1
    hot2 = lane == id2
    hot = jnp.where(hot1 | hot2, 1.0, 0.0)
    tm = lg.shape[0]
    strict_lower = (lax.broadcasted_iota(I32, (tm, tm), 0) > lax.broadcasted_iota(I32, (tm, tm), 1))
    before = jnp.dot(jnp.where(strict_lower, 1.0, 0.0).astype(BF16), hot.astype(BF16),
                     preferred_element_type=F32) + counts_ref[...]
    rank1 = jnp.sum(jnp.where(hot1, before, 0.0), axis=-1, keepdims=True)
    rank2 = jnp.sum(jnp.where(hot2, before, 0.0), axis=-1, keepdims=True)
    counts_ref[...] = counts_ref[...] + jnp.sum(hot, axis=0, keepdims=True)

    route = jnp.where(lane == 0, id1,
                      jnp.where(lane == 1, id2,
                                jnp.where(lane == 2, gw / den,
                                          jnp.where(lane == 3, gw * e2 / den,
                                                    jnp.where(lane == 4, rank1,
                                                              jnp.where(lane == 5, rank2, 0.0))))))
    route_ref[...] = route


def _post_block(x, f, ln_g, ln_b, w_q, kv, w_o, w_r, b_r, n_groups, per_group, *, tm=ROW_TILE):
    t, d = x.shape
    inner = w_q.shape[1]
    n_mem = kv.shape[0]
    n_slabs = d // 2 // LANES
    row = lambda a: a.reshape(1, d).astype(F32)
    full = lambda shape: pl.BlockSpec(shape, lambda i: (0,) * len(shape))
    tile = lambda w: pl.BlockSpec((tm, w), lambda i: (i, 0))
    return pl.pallas_call(
        functools.partial(_post_kernel, n_groups=n_groups, per_group=per_group),
        grid=(t // tm,),
        in_specs=[tile(d), tile(d), full((1, d)), full((1, d)), full((1, d)), full((1, d)),
                  full((d, inner)), full((n_mem, 2 * inner)), full((inner, d)), full((d, LANES)), full((1, LANES))],
        out_specs=[tile(d), pl.BlockSpec((tm * n_slabs, LANES), lambda i: (i, 0)), tile(LANES), full((1, LANES))],
        out_shape=[jax.ShapeDtypeStruct((t, d), F32), jax.ShapeDtypeStruct((t * n_slabs, LANES), U32),
                   jax.ShapeDtypeStruct((t, LANES), F32), jax.ShapeDtypeStruct((1, LANES), F32)],
        compiler_params=_params(1),
    )(x, f, row(ln_g[0]), row(ln_b[0]), row(ln_g[1]), row(ln_b[1]), w_q, kv, w_o, w_r, b_r)


def _moe_kernel(be_ref, src_ref, nu_ref, xg_hbm, wgu_ref, wd_ref, yb_ref, xbuf0, xbuf1, sem, *, blk):
    i = pl.program_id(0)
    n_used = nu_ref[0]
    bufs = (xbuf0, xbuf1)
    n_slabs = yb_ref.shape[0] // blk
    half = n_slabs * LANES
    de = wd_ref.shape[0]

    def row_copy(j, r, s):
        src = pl.multiple_of(src_ref[j * blk + r], n_slabs)
        return pltpu.make_async_copy(xg_hbm.at[pl.ds(src, n_slabs), :],
                                     bufs[s].at[pl.ds(r * GATHER_PITCH, n_slabs), :], sem.at[s])

    def wait_block(s):
        pltpu.make_async_copy(xg_hbm.at[pl.ds(0, blk * n_slabs), :], bufs[s].at[pl.ds(0, blk * n_slabs), :],
                              sem.at[s]).wait()

    @pl.when(i == 0)
    def _():
        def body(r, carry):
            src = pl.multiple_of(src_ref[r], n_slabs)
            dst = pl.multiple_of(r * GATHER_PITCH, 8)
            pltpu.make_async_copy(xg_hbm.at[pl.ds(src, n_slabs), :], xbuf0.at[pl.ds(dst, n_slabs), :],
                                  sem.at[0]).start()
            return carry
        lax.fori_loop(0, blk, body, 0)

    def expert_block(s):
        wait_block(s)
        los, his = [], []
        for q in range(n_slabs):
            lo_q, hi_q = _unpack_halves(_load_slab_column(bufs[s], q, blk))
            los.append(lo_q.astype(BF16))
            his.append(hi_q.astype(BF16))
        lo = jnp.concatenate(los, axis=1)
        hi = jnp.concatenate(his, axis=1)
        for r in range(blk):
            row_copy(i + 1, r, 1 - s).start()
        wu = 2 * de // MOE_UP_PIECES
        au = []
        for c in range(MOE_UP_PIECES):
            cols = slice(c * wu, (c + 1) * wu)
            au.append(jnp.dot(lo, wgu_ref[0:half, cols], preferred_element_type=F32)
                      + jnp.dot(hi, wgu_ref[half:, cols], preferred_element_type=F32))
        au = jnp.concatenate(au, axis=1)
        a = au[:, 0:de]
        u = au[:, de:]
        hmid = (a * jax.nn.sigmoid(a) * u).astype(BF16)
        wdn = half // MOE_DOWN_PIECES
        for c in range(MOE_DOWN_PIECES):
            y_lo = jnp.dot(hmid, wd_ref[:, c * wdn:(c + 1) * wdn], preferred_element_type=F32)
            y_hi = jnp.dot(hmid, wd_ref[:, half + c * wdn:half + (c + 1) * wdn], preferred_element_type=F32)
            _store_slabs(yb_ref, c * (wdn // LANES), _pack_halves(y_lo, y_hi))

    for s in range(2):
        pl.when((i < n_used) & (i % 2 == s))(functools.partial(expert_block, s))
        pl.when((i == n_used) & (i % 2 == s))(functools.partial(wait_block, s))

    @pl.when(i >= n_used)
    def _():
        yb_ref[...] = jnp.zeros(yb_ref.shape, yb_ref.dtype)


def _moe_experts(xg, slot_src, blk_expert, n_used, w_gate_up, w_down, *, blk=MOE_BLOCK):
    n_slots = slot_src.shape[0]
    nb = n_slots // blk
    de, d = w_down.shape[1:]
    n_slabs = d // 2 // LANES
    assert n_slabs <= GATHER_PITCH and GATHER_PITCH % 8 == 0
    buf = pltpu.VMEM((blk * GATHER_PITCH, LANES), U32)
    grid_spec = pltpu.PrefetchScalarGridSpec(
        num_scalar_prefetch=3,
        grid=(nb,),
        in_specs=[pl.BlockSpec(memory_space=pl.ANY),
                  pl.BlockSpec((None, d, 2 * de), lambda i, be, src, nu: (be[i], 0, 0)),
                  pl.BlockSpec((None, de, d), lambda i, be, src, nu: (be[i], 0, 0))],
        out_specs=pl.BlockSpec((blk * n_slabs, LANES), lambda i, be, src, nu: (i, 0)),
        scratch_shapes=[buf, buf, pltpu.SemaphoreType.DMA((2,))],
    )
    return pl.pallas_call(
        functools.partial(_moe_kernel, blk=blk),
        grid_spec=grid_spec,
        out_shape=jax.ShapeDtypeStruct((n_slots * n_slabs, LANES), U32),
        compiler_params=_params(1),
    )(blk_expert, slot_src, n_used, xg, w_gate_up, w_down)


def _combine_kernel(src_ref, x2_ref, route_ref, g_ref, b_ref, yb_hbm, x3_ref, x3b_ref, ybuf, sem, *, tm):
    i = pl.program_id(0)
    n = pl.num_programs(0)
    slot = i % 2
    d = x2_ref.shape[1]
    half = d // 2
    n_slabs = half // LANES

    def issue(j, s):
        base = j * tm * TOP_K

        def body(r, carry):
            dst = pl.multiple_of(r * GATHER_PITCH, 8)
            for k in range(TOP_K):
                src = pl.multiple_of(src_ref[base + TOP_K * r + k], n_slabs)
                pltpu.make_async_copy(yb_hbm.at[pl.ds(src, n_slabs), :], ybuf.at[s, k, pl.ds(dst, n_slabs), :],
                                      sem.at[s]).start()
            return carry

        lax.fori_loop(0, tm, body, 0)

    @pl.when(i == 0)
    def _():
        issue(0, 0)

    @pl.when(i + 1 < n)
    def _():
        issue(i + 1, 1 - slot)

    for k in range(TOP_K):
        pltpu.make_async_copy(yb_hbm.at[pl.ds(0, tm * n_slabs), :], ybuf.at[slot, k, pl.ds(0, tm * n_slabs), :],
                              sem.at[slot]).wait()
    gates = [route_ref[:, TOP_K + k:TOP_K + k + 1] for k in range(TOP_K)]
    y_lo, y_hi = [], []
    for q in range(n_slabs):
        lo = hi = None
        for k in range(TOP_K):
            lo_k, hi_k = _unpack_halves(ybuf[slot, k, pl.ds(q, tm, stride=GATHER_PITCH), :])
            lo = lo_k * gates[k] if lo is None else lo + lo_k * gates[k]
            hi = hi_k * gates[k] if hi is None else hi + hi_k * gates[k]
        y_lo.append(DEEPNORM_ALPHA * x2_ref[:, q * LANES:(q + 1) * LANES] + lo)
        y_hi.append(DEEPNORM_ALPHA * x2_ref[:, half + q * LANES:half + (q + 1) * LANES] + hi)
    cols = [slice(c * LANES, (c + 1) * LANES) for c in range(2 * n_slabs)]
    out = _ln_pieces(y_lo + y_hi, [g_ref[:, c] for c in cols], [b_ref[:, c] for c in cols])
    for c, o in zip(cols, out):
        x3_ref[:, c] = o
        x3b_ref[:, c] = o.astype(BF16)


def _moe_combine(x2, route, yb, slot_of, ln_g, ln_b, *, tm=ROW_TILE):
    t, d = x2.shape
    half = d // 2
    row = lambda a: a.reshape(1, d).astype(F32)
    grid_spec = pltpu.PrefetchScalarGridSpec(
        num_scalar_prefetch=1,
        grid=(t // tm,),
        in_specs=[pl.BlockSpec((tm, d), lambda i, dst: (i, 0)),
                  pl.BlockSpec((tm, LANES), lambda i, dst: (i, 0)),
                  pl.BlockSpec((1, d), lambda i, dst: (0, 0)),
                  pl.BlockSpec((1, d), lambda i, dst: (0, 0)),
                  pl.BlockSpec(memory_space=pl.ANY)],
        out_specs=[pl.BlockSpec((tm, d), lambda i, dst: (i, 0)), pl.BlockSpec((tm, d), lambda i, dst: (i, 0))],
        scratch_shapes=[pltpu.VMEM((2, TOP_K, tm * GATHER_PITCH, LANES), U32), pltpu.SemaphoreType.DMA((2,))],
    )
    return pl.pallas_call(
        functools.partial(_combine_kernel, tm=tm),
        grid_spec=grid_spec,
        out_shape=[jax.ShapeDtypeStruct((t, d), F32), jax.ShapeDtypeStruct((t, d), BF16)],
        compiler_params=_params(1),
    )(slot_of, x2, route, row(ln_g), row(ln_b), yb)


def _dispatch_plan(route, counts, n_experts, blk, n_slabs):
    t = route.shape[0]
    ids = route[:, 0:TOP_K].astype(I32)
    rank = route[:, 2 * TOP_K:3 * TOP_K].astype(I32)
    counts = counts[0, :n_experts].astype(I32)
    padded = (counts + blk - 1) // blk * blk
    ends = jnp.cumsum(padded)
    starts = ends - padded
    expert_iota = jnp.arange(n_experts, dtype=I32)
    seg_start = jnp.sum(jnp.where(ids[:, :, None] == expert_iota, starts, 0), axis=-1)
    dest = (seg_start + rank).reshape(-1)
    nb = (t * TOP_K) // blk + n_experts
    flat_src = jnp.arange(t * TOP_K, dtype=I32) // TOP_K * n_slabs
    slot_src = jnp.zeros((nb * blk,), I32).at[dest].set(flat_src, unique_indices=True)
    blk_start = jnp.arange(nb, dtype=I32) * blk
    blk_expert = jnp.minimum(jnp.sum((ends[None, :] <= blk_start[:, None]).astype(I32), axis=1), n_experts - 1)
    n_used = (ends[-1] // blk).astype(I32).reshape(1)
    return dest * n_slabs, slot_src, blk_expert, n_used


def kernel(x, mem, attn_w_qkv, attn_w_o, attn_rel_bias, conv_w_in, conv_b_in, conv_w_dw, conv_b_dw, conv_ln_g,
           conv_ln_b, conv_w_out, conv_b_out, pool_w, pool_scale, mem_w_q, mem_w_kv, mem_w_o, moe_w_group,
           moe_b_group, moe_w_router, moe_b_router, moe_w_gate, moe_w_up, moe_w_down, ln_g, ln_b):
    batch, seq, d = x.shape
    assert batch == 1 and ln_g.shape[0] == DEPTH
    n_groups = moe_w_group.shape[2]
    per_group = moe_w_router.shape[3]
    n_experts = n_groups * per_group
    assert n_groups + n_experts <= LANES

    xf = x.reshape(seq, d)
    xb = xf.astype(BF16)
    memb = mem.reshape(mem.shape[1], d).astype(BF16)

    for i in range(DEPTH):
        kind = i % N_MIXERS
        j = i // N_MIXERS
        if kind == 0:
            qkv = _matmul(xb, attn_w_qkv[j].astype(BF16))
            o = _chunked_attention(qkv, attn_rel_bias[j], d)
            f = _matmul(o, attn_w_o[j].astype(BF16))
        elif kind == 1:
            u = _glu_matmul(xb, conv_w_in[j].astype(BF16), conv_b_in[j])
            v = _conv_ln_swish(u, conv_w_dw[j], conv_b_dw[j], conv_ln_g[j], conv_ln_b[j])
            f = _matmul(v, conv_w_out[j].astype(BF16), conv_b_out[j])
        else:
            f = _multiscale_pool(xf, pool_w[j].astype(BF16), pool_scale[j])

        kv = _matmul(memb, mem_w_kv[i].astype(BF16), tm=memb.shape[0])
        w_r = jnp.concatenate([moe_w_group[i], moe_w_router[i].transpose(1, 0, 2).reshape(d, n_experts)], axis=1)
        w_r = jnp.pad(w_r, ((0, 0), (0, LANES - w_r.shape[1]))).astype(BF16)
        b_r = jnp.concatenate([moe_b_group[i], moe_b_router[i].reshape(n_experts)])
        b_r = jnp.pad(b_r, (0, LANES - b_r.shape[0])).reshape(1, LANES).astype(F32)
        x2, x2p, route, counts = _post_block(xf, f, ln_g[i], ln_b[i], mem_w_q[i].astype(BF16), kv,
                                             mem_w_o[i].astype(BF16), w_r, b_r, n_groups, per_group)

        slot_of, slot_src, blk_expert, n_used = _dispatch_plan(route, counts, n_experts, MOE_BLOCK, d // 2 // LANES)
        w_gate_up = jnp.concatenate([moe_w_gate[i], moe_w_up[i]], axis=-1).astype(BF16)
        yb = _moe_experts(x2p, slot_src, blk_expert, n_used, w_gate_up, moe_w_down[i].astype(BF16))
        xf, xb = _moe_combine(x2, route, yb, slot_of, ln_g[i, 2], ln_b[i, 2])

    return xf.reshape(batch, seq, d)
```

```python
import functools
from typing import NamedTuple

import jax
import jax.numpy as jnp
from jax import lax
from jax.experimental import pallas as pl
from jax.experimental.pallas import tpu as pltpu

F32 = jnp.float32
BF16 = jnp.bfloat16
U32 = jnp.uint32
I32 = jnp.int32

DEPTH = 4
N_MIXERS = 3
ATTN_HEADS = 32
CHUNK = 64
LEFT_CHUNKS = 8
POOL_WINDOWS = (2, 4, 8, 16)
MEM_HEADS = 4
TOP_K = 2
LN_EPS = 1e-5
DEEPNORM_ALPHA = (2 * DEPTH) ** 0.25
NEG_BIG = -1e30

V7X_VMEM_BYTES = 64 * 1024 * 1024
VMEM_LIMIT = V7X_VMEM_BYTES - 8 * 1024 * 1024
LANES = 128

ATTN_Q_TILE = 4 * CHUNK
ATTN_HEADS_PER_STEP = 8
MOE_BLOCK = 256
MOE_UP_PIECES = 3
MOE_DOWN_PIECES = 4
ROW_TILE = 256


def _params(n_grid_dims):
    return pltpu.CompilerParams(dimension_semantics=("arbitrary",) * n_grid_dims,
                                vmem_limit_bytes=VMEM_LIMIT)


def _ln_pieces(pieces, g_pieces, b_pieces):
    d = sum(p.shape[-1] for p in pieces)
    mu = sum(jnp.sum(p, axis=-1, keepdims=True) for p in pieces) * (1.0 / d)
    cen = [p - mu for p in pieces]
    var = sum(jnp.sum(c * c, axis=-1, keepdims=True) for c in cen) * (1.0 / d)
    inv = lax.rsqrt(var + LN_EPS)
    return [c * inv * g + b for c, g, b in zip(cen, g_pieces, b_pieces)]


def _pack_halves(lo, hi):
    half_ulp = jnp.uint32(0x8000)
    lo_bits = (lax.bitcast_convert_type(lo, U32) + half_ulp) >> 16
    hi_bits = (lax.bitcast_convert_type(hi, U32) + half_ulp) & jnp.uint32(0xFFFF0000)
    return lo_bits | hi_bits


def _unpack_halves(w):
    lo = lax.bitcast_convert_type(w << 16, F32)
    hi = lax.bitcast_convert_type(w & jnp.uint32(0xFFFF0000), F32)
    return lo, hi


GATHER_PITCH = 24


def _store_slabs(ref, first_slab, packed):
    rows = packed.shape[0]
    n_slabs = ref.shape[0] // rows
    for q in range(packed.shape[1] // LANES):
        ref[pl.ds(first_slab + q, rows, stride=n_slabs), :] = packed[:, q * LANES:(q + 1) * LANES]


def _load_slab_column(ref_2d, s, rows):
    return ref_2d[pl.ds(s, rows, stride=GATHER_PITCH), :]


MM_CAST_ROWS = 512


def _cast_weight_tile(w_ref, w_bf16_ref):
    k = w_ref.shape[0]
    step = min(MM_CAST_ROWS, k)
    for r in range(0, k, step):
        w_bf16_ref[r:r + step, :] = w_ref[r:r + step, :].astype(BF16)


def _mm_kernel(x_ref, w_ref, *rest, has_bias):
    if has_bias:
        b_ref, o_ref, w_bf16_ref = rest
    else:
        o_ref, w_bf16_ref = rest

    @pl.when(pl.program_id(1) == 0)
    def _():
        _cast_weight_tile(w_ref, w_bf16_ref)

    acc = jnp.dot(x_ref[...], w_bf16_ref[...], preferred_element_type=F32)
    if has_bias:
        acc = acc + b_ref[...]
    o_ref[...] = acc.astype(o_ref.dtype)


def _matmul(x, w_stack, layer, bias=None, *, out_dtype=BF16, tm=512, tn=1024):
    m, k = x.shape
    n = w_stack.shape[2]
    tm = min(tm, m)
    tn = min(tn, n)
    assert m % tm == 0 and n % tn == 0 and w_stack.shape[1] == k
    in_specs = [pl.BlockSpec((tm, k), lambda j, i: (i, 0)),
                pl.BlockSpec((None, k, tn), lambda j, i: (layer, 0, j))]
    args = [x, w_stack]
    if bias is not None:
        in_specs.append(pl.BlockSpec((1, tn), lambda j, i: (0, j)))
        args.append(bias.reshape(1, n).astype(F32))
    return pl.pallas_call(
        functools.partial(_mm_kernel, has_bias=bias is not None),
        grid=(n // tn, m // tm),
        in_specs=in_specs,
        out_specs=pl.BlockSpec((tm, tn), lambda j, i: (i, j)),
        out_shape=jax.ShapeDtypeStruct((m, n), out_dtype),
        scratch_shapes=[pltpu.VMEM((k, tn), BF16)],
        compiler_params=_params(2),
    )(*args)


def _attn_kernel(q_ref, k0_ref, k1_ref, k2_ref, v0_ref, v1_ref, v2_ref, u_ref, o_ref, bias_ref, *, hb, dh, qt):
    i = pl.program_id(1)

    @pl.when(i == 0)
    def _():
        shift = CHUNK.bit_length() - 1
        qc = lax.shift_right_logical(lax.broadcasted_iota(I32, (qt, 3 * qt), 0), shift)
        kc = lax.shift_right_logical(lax.broadcasted_iota(I32, (qt, 3 * qt), 1), shift)
        band = (kc >= qc) & (kc <= qc + LEFT_CHUNKS)
        for h in range(hb):
            rows = jnp.broadcast_to(u_ref[h:h + 1, :], (qt, u_ref.shape[1]))
            toep = pltpu.roll(rows, 0, 1, stride=1, stride_axis=0)
            bias_ref[h] = jnp.where(band, toep[:, 0:3 * qt], NEG_BIG)

    pens = [jnp.where(i + j >= 2, 0.0, NEG_BIG).astype(F32) for j in range(2)] + [None]
    scale = dh ** -0.5
    k_refs = (k0_ref, k1_ref, k2_ref)
    v_refs = (v0_ref, v1_ref, v2_ref)
    for h in range(hb):
        sl = slice(h * dh, (h + 1) * dh)
        q = (q_ref[:, sl].astype(F32) * scale).astype(BF16)
        s = []
        for j in range(3):
            sj = lax.dot_general(q, k_refs[j][:, sl], (((1,), (1,)), ((), ())), preferred_element_type=F32)
            sj = sj + bias_ref[h, :, j * qt:(j + 1) * qt]
            if pens[j] is not None:
                sj = sj + pens[j]
            s.append(sj)
        m = jnp.maximum(jnp.maximum(jnp.max(s[0], axis=-1, keepdims=True), jnp.max(s[1], axis=-1, keepdims=True)),
                        jnp.max(s[2], axis=-1, keepdims=True))
        p = [jnp.exp(sj - m) for sj in s]
        l = sum(jnp.sum(pj, axis=-1, keepdims=True) for pj in p)
        o = sum(jnp.dot(p[j].astype(BF16), v_refs[j][:, sl], preferred_element_type=F32) for j in range(3))
        o_ref[:, sl] = (o / l).astype(o_ref.dtype)


def _attn_offset_bias(rel_table, qt):
    max_rel = (rel_table.shape[1] - 1) // 2
    kw = 3 * qt
    period = 4 * qt
    m = jnp.arange(period)
    off = jnp.where(m < kw, m, m - period)
    idx = jnp.clip(2 * qt - off, -max_rel, max_rel) + max_rel
    return jnp.take(rel_table.astype(F32), idx, axis=1)


def _chunked_attention(qkv, rel_table, d):
    t = qkv.shape[0]
    qt = ATTN_Q_TILE
    hb = ATTN_HEADS_PER_STEP
    dh = d // ATTN_HEADS
    assert LEFT_CHUNKS * CHUNK == 2 * qt and t % qt == 0 and ATTN_HEADS % hb == 0
    w = hb * dh
    ng = d // w
    u = _attn_offset_bias(rel_table, qt)

    def kv_spec(j, base):
        return pl.BlockSpec((qt, w), lambda g, i: (jnp.maximum(i - 2 + j, 0), base + g))

    return pl.pallas_call(
        functools.partial(_attn_kernel, hb=hb, dh=dh, qt=qt),
        grid=(ng, t // qt),
        in_specs=[pl.BlockSpec((qt, w), lambda g, i: (i, g))]
                 + [kv_spec(j, ng) for j in range(3)]
                 + [kv_spec(j, 2 * ng) for j in range(3)]
                 + [pl.BlockSpec((hb, u.shape[1]), lambda g, i: (g, 0))],
        out_specs=pl.BlockSpec((qt, w), lambda g, i: (i, g)),
        out_shape=jax.ShapeDtypeStruct((t, d), BF16),
        scratch_shapes=[pltpu.VMEM((hb, qt, 3 * qt), F32)],
        compiler_params=_params(2),
    )(qkv, qkv, qkv, qkv, qkv, qkv, qkv, u)


def _glu_kernel(x_ref, wa_ref, wg_ref, ba_ref, bg_ref, o_ref, wa_bf16_ref, wg_bf16_ref):
    @pl.when(pl.program_id(1) == 0)
    def _():
        _cast_weight_tile(wa_ref, wa_bf16_ref)
        _cast_weight_tile(wg_ref, wg_bf16_ref)

    x = x_ref[...]
    a = jnp.dot(x, wa_bf16_ref[...], preferred_element_type=F32) + ba_ref[...]
    g = jnp.dot(x, wg_bf16_ref[...], preferred_element_type=F32) + bg_ref[...]
    o_ref[...] = (a * jax.nn.sigmoid(g)).astype(o_ref.dtype)


def _glu_matmul(x, w_stack, layer, b, *, tm=512, tn=512):
    m, k = x.shape
    n = w_stack.shape[2] // 2
    assert m % tm == 0 and n % tn == 0
    nb = n // tn
    b2 = b.reshape(1, 2 * n).astype(F32)
    return pl.pallas_call(
        _glu_kernel,
        grid=(nb, m // tm),
        in_specs=[pl.BlockSpec((tm, k), lambda j, i: (i, 0)),
                  pl.BlockSpec((None, k, tn), lambda j, i: (layer, 0, j)),
                  pl.BlockSpec((None, k, tn), lambda j, i: (layer, 0, nb + j)),
                  pl.BlockSpec((1, tn), lambda j, i: (0, j)),
                  pl.BlockSpec((1, tn), lambda j, i: (0, nb + j))],
        out_specs=pl.BlockSpec((tm, tn), lambda j, i: (i, j)),
        out_shape=jax.ShapeDtypeStruct((m, n), BF16),
        scratch_shapes=[pltpu.VMEM((k, tn), BF16), pltpu.VMEM((k, tn), BF16)],
        compiler_params=_params(2),
    )(x, w_stack, w_stack, b2, b2)


CONV_HALO = 32
CONV_ROWS = 64
CONV_COLS = 128


def _conv_kernel(u_ref, halo_ref, w_ref, bdw_ref, g_ref, b_ref, o_ref, ext_ref, acc_ref, *, width, tm):
    i = pl.program_id(0)
    d = u_ref.shape[1]
    halo = halo_ref[...].astype(F32)
    ext_ref[0:CONV_HALO, :] = jnp.where(i > 0, halo, 0.0)
    ext_ref[CONV_HALO:, :] = u_ref[...].astype(F32)
    off = CONV_HALO - (width - 1)

    def col_chunk(c, carry):
        c0 = pl.multiple_of(c * CONV_COLS, CONV_COLS)
        w = w_ref[:, pl.ds(c0, CONV_COLS)]
        for r0 in range(0, tm, CONV_ROWS):
            acc = jnp.zeros((CONV_ROWS, CONV_COLS), F32)
            for b in range(8):
                taps = [k for k in range(b, width, 8)]
                span = CONV_ROWS + (taps[-1] - b)
                v = ext_ref[pl.ds(r0 + off + b, span), pl.ds(c0, CONV_COLS)]
                for k in taps:
                    acc = acc + v[k - b:k - b + CONV_ROWS, :] * w[k:k + 1, :]
            acc_ref[r0:r0 + CONV_ROWS, pl.ds(c0, CONV_COLS)] = acc + bdw_ref[:, pl.ds(c0, CONV_COLS)]
        return carry

    lax.fori_loop(0, d // CONV_COLS, col_chunk, 0)
    (y,) = _ln_pieces([acc_ref[...]], [g_ref[...]], [b_ref[...]])
    o_ref[...] = (y * jax.nn.sigmoid(y)).astype(o_ref.dtype)


def _conv_ln_swish(u, w_dw, b_dw, ln_g, ln_b, *, tm=ROW_TILE):
    t, d = u.shape
    width = w_dw.shape[0]
    assert width - 1 <= CONV_HALO and t % tm == 0 and tm % CONV_HALO == 0 and d % CONV_COLS == 0
    hb = tm // CONV_HALO
    row = lambda a: a.reshape(1, d).astype(F32)
    return pl.pallas_call(
        functools.partial(_conv_kernel, width=width, tm=tm),
        grid=(t // tm,),
        in_specs=[pl.BlockSpec((tm, d), lambda i: (i, 0)),
                  pl.BlockSpec((CONV_HALO, d), lambda i: (jnp.maximum(i * hb - 1, 0), 0)),
                  pl.BlockSpec((width, d), lambda i: (0, 0)),
                  pl.BlockSpec((1, d), lambda i: (0, 0)),
                  pl.BlockSpec((1, d), lambda i: (0, 0)),
                  pl.BlockSpec((1, d), lambda i: (0, 0))],
        out_specs=pl.BlockSpec((tm, d), lambda i: (i, 0)),
        out_shape=jax.ShapeDtypeStruct((t, d), BF16),
        scratch_shapes=[pltpu.VMEM((tm + CONV_HALO, d), F32), pltpu.VMEM((tm, d), F32)],
        compiler_params=_params(1),
    )(u, u, w_dw.astype(F32), row(b_dw), row(ln_g), row(ln_b))


POOL_HALO = 16


def _pool_kernel(x_ref, halo_ref, w_ref, scale_ref, o_ref, ext_ref, *, tm, gw):
    i = pl.program_id(0)
    ext_ref[0:POOL_HALO, :] = jnp.where(i > 0, halo_ref[...], 0.0)
    ext_ref[POOL_HALO:, :] = x_ref[...]
    t_abs = i * tm + lax.broadcasted_iota(I32, (tm, 1), 0)
    for g, win in enumerate(POOL_WINDOWS):
        cols = slice(g * gw, (g + 1) * gw)
        acc = ext_ref[POOL_HALO:POOL_HALO + tm, cols]
        for j in range(1, win):
            acc = acc + ext_ref[POOL_HALO - j:POOL_HALO - j + tm, cols]
        cnt = jnp.minimum(t_abs + 1, win).astype(F32)
        pooled = acc / cnt - ext_ref[POOL_HALO:POOL_HALO + tm, cols]
        y = jnp.dot(pooled.astype(BF16), w_ref[g], preferred_element_type=F32)
        o_ref[:, cols] = (y * scale_ref[:, cols]).astype(o_ref.dtype)


def _multiscale_pool(x, w_pool, scale, *, tm=ROW_TILE):
    t, d = x.shape
    ng, gw, _ = w_pool.shape
    assert ng == len(POOL_WINDOWS) and max(POOL_WINDOWS) <= POOL_HALO and t % tm == 0
    hb = tm // POOL_HALO
    return pl.pallas_call(
        functools.partial(_pool_kernel, tm=tm, gw=gw),
        grid=(t // tm,),
        in_specs=[pl.BlockSpec((tm, d), lambda i: (i, 0)),
                  pl.BlockSpec((POOL_HALO, d), lambda i: (jnp.maximum(i * hb - 1, 0), 0)),
                  pl.BlockSpec((ng, gw, gw), lambda i: (0, 0, 0)),
                  pl.BlockSpec((1, d), lambda i: (0, 0))],
        out_specs=pl.BlockSpec((tm, d), lambda i: (i, 0)),
        out_shape=jax.ShapeDtypeStruct((t, d), BF16),
        scratch_shapes=[pltpu.VMEM((tm + POOL_HALO, d), F32)],
        compiler_params=_params(1),
    )(x, x, w_pool, scale.reshape(1, d).astype(F32))


def _post_kernel(x_ref, f_ref, g0_ref, b0_ref, g1_ref, b1_ref, wq_ref, kv_ref, wo_ref, wr_ref, br_ref,
                 x2_ref, x2p_ref, route_ref, counts_ref, *, n_groups, per_group):
    d = x_ref.shape[1]
    half = d // 2
    n_mem, inner2 = kv_ref.shape
    inner = inner2 // 2
    dh = inner // MEM_HEADS
    y = DEEPNORM_ALPHA * x_ref[...] + f_ref[...].astype(F32)
    (x1,) = _ln_pieces([y], [g0_ref[...]], [b0_ref[...]])
    q = jnp.dot(x1.astype(BF16), wq_ref[...], preferred_element_type=F32) * (dh ** -0.5)
    heads = []
    for h in range(MEM_HEADS):
        qh = q[:, h * dh:(h + 1) * dh].astype(BF16)
        kh = kv_ref[:, h * dh:(h + 1) * dh]
        vh = kv_ref[:, inner + h * dh:inner + (h + 1) * dh]
        s = lax.dot_general(qh, kh, (((1,), (1,)), ((), ())), preferred_element_type=F32)
        p = jnp.exp(s - jnp.max(s, axis=-1, keepdims=True))
        l = jnp.sum(p, axis=-1, keepdims=True)
        heads.append((jnp.dot(p.astype(BF16), vh, preferred_element_type=F32) / l).astype(BF16))
    o = jnp.concatenate(heads, axis=1)
    c = jnp.dot(o, wo_ref[...], preferred_element_type=F32)
    (x2,) = _ln_pieces([DEEPNORM_ALPHA * x1 + c], [g1_ref[...]], [b1_ref[...]])
    x2_ref[...] = x2
    _store_slabs(x2p_ref, 0, _pack_halves(x2[:, :half], x2[:, half:]))

    lg = jnp.dot(x2.astype(BF16), wr_ref[...], preferred_element_type=F32) + br_ref[...]
    lane = lax.broadcasted_iota(I32, lg.shape, 1).astype(F32)
    far = float(LANES)

    def first_lane_of_max(v):
        m = jnp.max(v, axis=-1, keepdims=True)
        return m, jnp.min(jnp.where(v == m, lane, far), axis=-1, keepdims=True)

    gl = jnp.where(lane < n_groups, lg, NEG_BIG)
    gm, gsel = first_lane_of_max(gl)
    gw = 1.0 / jnp.sum(jnp.exp(gl - gm), axis=-1, keepdims=True)
    lo = n_groups + gsel * per_group
    el = jnp.where((lane >= lo) & (lane < lo + per_group), lg, NEG_BIG)
    m1, i1 = first_lane_of_max(el)
    m2, i2 = first_lane_of_max(jnp.where(lane == i1, NEG_BIG, el))
    e2 = jnp.exp(m2 - m1)
    den = 1.0 + e2
    id1 = i1 - n_groups
    id2 = i2 - n_groups

    @pl.when(pl.program_id(0) == 0)
    def _():
        counts_ref[...] = jnp.zeros(counts_ref.shape, counts_ref.dtype)

    hot1 = lane == id1
    hot2 = lane == id2
    hot = jnp.where(hot1 | hot2, 1.0, 0.0)
    tm = lg.shape[0]
    strict_lower = (lax.broadcasted_iota(I32, (tm, tm), 0) > lax.broadcasted_iota(I32, (tm, tm), 1))
    before = jnp.dot(jnp.where(strict_lower, 1.0, 0.0).astype(BF16), hot.astype(BF16),
                     preferred_element_type=F32) + counts_ref[...]
    rank1 = jnp.sum(jnp.where(hot1, before, 0.0), axis=-1, keepdims=True)
    rank2 = jnp.sum(jnp.where(hot2, before, 0.0), axis=-1, keepdims=True)
    counts_ref[...] = counts_ref[...] + jnp.sum(hot, axis=0, keepdims=True)

    route = jnp.where(lane == 0, id1,
                      jnp.where(lane == 1, id2,
                                jnp.where(lane == 2, gw / den,
                                          jnp.where(lane == 3, gw * e2 / den,
                                                    jnp.where(lane == 4, rank1,
                                                              jnp.where(lane == 5, rank2, 0.0))))))
    route_ref[...] = route


def _post_block(x, f, ln_g, ln_b, w_q, kv, w_o, w_r, b_r, n_groups, per_group, *, tm=ROW_TILE):
    t, d = x.shape
    inner = w_q.shape[1]
    n_mem = kv.shape[0]
    n_slabs = d // 2 // LANES
    row = lambda a: a.reshape(1, d).astype(F32)
    full = lambda shape: pl.BlockSpec(shape, lambda i: (0,) * len(shape))
    tile = lambda w: pl.BlockSpec((tm, w), lambda i: (i, 0))
    return pl.pallas_call(
        functools.partial(_post_kernel, n_groups=n_groups, per_group=per_group),
        grid=(t // tm,),
        in_specs=[tile(d), tile(d), full((1, d)), full((1, d)), full((1, d)), full((1, d)),
                  full((d, inner)), full((n_mem, 2 * inner)), full((inner, d)), full((d, LANES)), full((1, LANES))],
        out_specs=[tile(d), pl.BlockSpec((tm * n_slabs, LANES), lambda i: (i, 0)), tile(LANES), full((1, LANES))],
        out_shape=[jax.ShapeDtypeStruct((t, d), F32), jax.ShapeDtypeStruct((t * n_slabs, LANES), U32),
                   jax.ShapeDtypeStruct((t, LANES), F32), jax.ShapeDtypeStruct((1, LANES), F32)],
        compiler_params=_params(1),
    )(x, f, row(ln_g[0]), row(ln_b[0]), row(ln_g[1]), row(ln_b[1]), w_q, kv, w_o, w_r, b_r)


MOE_CAST_ROWS = 128


def _moe_kernel(be_ref, first_ref, next_ref, src_ref, nu_ref, xg_hbm, wg_hbm, wu_hbm, wd_hbm, yb_ref,
                xbuf0, xbuf1, stage_g, stage_u, stage_d, wgu_ref, wd_ref, sem, wsem, *, blk):
    i = pl.program_id(0)
    n_used = nu_ref[0]
    bufs = (xbuf0, xbuf1)
    n_slabs = yb_ref.shape[0] // blk
    half = n_slabs * LANES
    de = wd_ref.shape[0]
    d = wd_ref.shape[1]

    def weight_copies(e):
        return (pltpu.make_async_copy(wg_hbm.at[e], stage_g, wsem.at[0]),
                pltpu.make_async_copy(wu_hbm.at[e], stage_u, wsem.at[1]),
                pltpu.make_async_copy(wd_hbm.at[e], stage_d, wsem.at[2]))

    @pl.when(i == 0)
    def _():
        for c in weight_copies(be_ref[0]):
            c.start()

    @pl.when((i < n_used) & (first_ref[i] == 1))
    def _():
        for c in weight_copies(be_ref[i]):
            c.wait()

        def cast_up(r, carry):
            rows = pl.ds(pl.multiple_of(r * MOE_CAST_ROWS, MOE_CAST_ROWS), MOE_CAST_ROWS)
            wgu_ref[rows, 0:de] = stage_g[rows, :].astype(BF16)
            wgu_ref[rows, de:2 * de] = stage_u[rows, :].astype(BF16)
            return carry

        def cast_down(r, carry):
            rows = pl.ds(pl.multiple_of(r * MOE_CAST_ROWS, MOE_CAST_ROWS), MOE_CAST_ROWS)
            for c0 in range(0, d, 1024):
                wd_ref[rows, c0:c0 + 1024] = stage_d[rows, c0:c0 + 1024].astype(BF16)
            return carry

        lax.fori_loop(0, d // MOE_CAST_ROWS, cast_up, 0)
        lax.fori_loop(0, de // MOE_CAST_ROWS, cast_down, 0)

        @pl.when(next_ref[i] >= 0)
        def _():
            for c in weight_copies(next_ref[i]):
                c.start()

    def row_copy(j, r, s):
        src = pl.multiple_of(src_ref[j * blk + r], n_slabs)
        return pltpu.make_async_copy(xg_hbm.at[pl.ds(src, n_slabs), :],
                                     bufs[s].at[pl.ds(r * GATHER_PITCH, n_slabs), :], sem.at[s])

    def wait_block(s):
        pltpu.make_async_copy(xg_hbm.at[pl.ds(0, blk * n_slabs), :], bufs[s].at[pl.ds(0, blk * n_slabs), :],
                              sem.at[s]).wait()

    @pl.when(i == 0)
    def _():
        def body(r, carry):
            src = pl.multiple_of(src_ref[r], n_slabs)
            dst = pl.multiple_of(r * GATHER_PITCH, 8)
            pltpu.make_async_copy(xg_hbm.at[pl.ds(src, n_slabs), :], xbuf0.at[pl.ds(dst, n_slabs), :],
                                  sem.at[0]).start()
            return carry
        lax.fori_loop(0, blk, body, 0)

    def expert_block(s):
        wait_block(s)
        los, his = [], []
        for q in range(n_slabs):
            lo_q, hi_q = _unpack_halves(_load_slab_column(bufs[s], q, blk))
            los.append(lo_q.astype(BF16))
            his.append(hi_q.astype(BF16))
        lo = jnp.concatenate(los, axis=1)
        hi = jnp.concatenate(his, axis=1)
        for r in range(blk):
            row_copy(i + 1, r, 1 - s).start()
        wu = 2 * de // MOE_UP_PIECES
        au = []
        for c in range(MOE_UP_PIECES):
            cols = slice(c * wu, (c + 1) * wu)
            au.append(jnp.dot(lo, wgu_ref[0:half, cols], preferred_element_type=F32)
                      + jnp.dot(hi, wgu_ref[half:, cols], preferred_element_type=F32))
        au = jnp.concatenate(au, axis=1)
        a = au[:, 0:de]
        u = au[:, de:]
        hmid = (a * jax.nn.sigmoid(a) * u).astype(BF16)
        wdn = half // MOE_DOWN_PIECES
        for c in range(MOE_DOWN_PIECES):
            y_lo = jnp.dot(hmid, wd_ref[:, c * wdn:(c + 1) * wdn], preferred_element_type=F32)
            y_hi = jnp.dot(hmid, wd_ref[:, half + c * wdn:half + (c + 1) * wdn], preferred_element_type=F32)
            _store_slabs(yb_ref, c * (wdn // LANES), _pack_halves(y_lo, y_hi))

    for s in range(2):
        pl.when((i < n_used) & (i % 2 == s))(functools.partial(expert_block, s))
        pl.when((i == n_used) & (i % 2 == s))(functools.partial(wait_block, s))

    @pl.when(i >= n_used)
    def _():
        yb_ref[...] = jnp.zeros(yb_ref.shape, yb_ref.dtype)


def _moe_experts(xg, plan, w_gate, w_up, w_down, *, blk=MOE_BLOCK):
    n_slots = plan.slot_src.shape[0]
    nb = n_slots // blk
    de, d = w_down.shape[1:]
    n_slabs = d // 2 // LANES
    assert n_slabs <= GATHER_PITCH and GATHER_PITCH % 8 == 0
    assert d % MOE_CAST_ROWS == 0 and de % MOE_CAST_ROWS == 0 and d % 1024 == 0
    buf = pltpu.VMEM((blk * GATHER_PITCH, LANES), U32)
    any_spec = pl.BlockSpec(memory_space=pl.ANY)
    grid_spec = pltpu.PrefetchScalarGridSpec(
        num_scalar_prefetch=5,
        grid=(nb,),
        in_specs=[any_spec, any_spec, any_spec, any_spec],
        out_specs=pl.BlockSpec((blk * n_slabs, LANES), lambda i, *_: (i, 0)),
        scratch_shapes=[buf, buf,
                        pltpu.VMEM((d, de), F32), pltpu.VMEM((d, de), F32), pltpu.VMEM((de, d), F32),
                        pltpu.VMEM((d, 2 * de), BF16), pltpu.VMEM((de, d), BF16),
                        pltpu.SemaphoreType.DMA((2,)), pltpu.SemaphoreType.DMA((3,))],
    )
    return pl.pallas_call(
        functools.partial(_moe_kernel, blk=blk),
        grid_spec=grid_spec,
        out_shape=jax.ShapeDtypeStruct((n_slots * n_slabs, LANES), U32),
        compiler_params=_params(1),
    )(plan.blk_expert, plan.seg_first, plan.seg_next, plan.slot_src, plan.n_used, xg, w_gate, w_up, w_down)


def _combine_kernel(src_ref, x2_ref, route_ref, g_ref, b_ref, yb_hbm, x3_ref, x3b_ref, ybuf, sem, *, tm):
    i = pl.program_id(0)
    n = pl.num_programs(0)
    slot = i % 2
    d = x2_ref.shape[1]
    half = d // 2
    n_slabs = half // LANES

    def issue(j, s):
        base = j * tm * TOP_K

        def body(r, carry):
            dst = pl.multiple_of(r * GATHER_PITCH, 8)
            for k in range(TOP_K):
                src = pl.multiple_of(src_ref[base + TOP_K * r + k], n_slabs)
                pltpu.make_async_copy(yb_hbm.at[pl.ds(src, n_slabs), :], ybuf.at[s, k, pl.ds(dst, n_slabs), :],
                                      sem.at[s]).start()
            return carry

        lax.fori_loop(0, tm, body, 0)

    @pl.when(i == 0)
    def _():
        issue(0, 0)

    @pl.when(i + 1 < n)
    def _():
        issue(i + 1, 1 - slot)

    for k in range(TOP_K):
        pltpu.make_async_copy(yb_hbm.at[pl.ds(0, tm * n_slabs), :], ybuf.at[slot, k, pl.ds(0, tm * n_slabs), :],
                              sem.at[slot]).wait()
    gates = [route_ref[:, TOP_K + k:TOP_K + k + 1] for k in range(TOP_K)]
    y_lo, y_hi = [], []
    for q in range(n_slabs):
        lo = hi = None
        for k in range(TOP_K):
            lo_k, hi_k = _unpack_halves(ybuf[slot, k, pl.ds(q, tm, stride=GATHER_PITCH), :])
            lo = lo_k * gates[k] if lo is None else lo + lo_k * gates[k]
            hi = hi_k * gates[k] if hi is None else hi + hi_k * gates[k]
        y_lo.append(DEEPNORM_ALPHA * x2_ref[:, q * LANES:(q + 1) * LANES] + lo)
        y_hi.append(DEEPNORM_ALPHA * x2_ref[:, half + q * LANES:half + (q + 1) * LANES] + hi)
    cols = [slice(c * LANES, (c + 1) * LANES) for c in range(2 * n_slabs)]
    out = _ln_pieces(y_lo + y_hi, [g_ref[:, c] for c in cols], [b_ref[:, c] for c in cols])
    for c, o in zip(cols, out):
        x3_ref[:, c] = o
        x3b_ref[:, c] = o.astype(BF16)


def _moe_combine(x2, route, yb, slot_of, ln_g, ln_b, *, tm=ROW_TILE):
    t, d = x2.shape
    half = d // 2
    row = lambda a: a.reshape(1, d).astype(F32)
    grid_spec = pltpu.PrefetchScalarGridSpec(
        num_scalar_prefetch=1,
        grid=(t // tm,),
        in_specs=[pl.BlockSpec((tm, d), lambda i, dst: (i, 0)),
                  pl.BlockSpec((tm, LANES), lambda i, dst: (i, 0)),
                  pl.BlockSpec((1, d), lambda i, dst: (0, 0)),
                  pl.BlockSpec((1, d), lambda i, dst: (0, 0)),
                  pl.BlockSpec(memory_space=pl.ANY)],
        out_specs=[pl.BlockSpec((tm, d), lambda i, dst: (i, 0)), pl.BlockSpec((tm, d), lambda i, dst: (i, 0))],
        scratch_shapes=[pltpu.VMEM((2, TOP_K, tm * GATHER_PITCH, LANES), U32), pltpu.SemaphoreType.DMA((2,))],
    )
    return pl.pallas_call(
        functools.partial(_combine_kernel, tm=tm),
        grid_spec=grid_spec,
        out_shape=[jax.ShapeDtypeStruct((t, d), F32), jax.ShapeDtypeStruct((t, d), BF16)],
        compiler_params=_params(1),
    )(slot_of, x2, route, row(ln_g), row(ln_b), yb)


class DispatchPlan(NamedTuple):
    slot_of: jax.Array
    slot_src: jax.Array
    blk_expert: jax.Array
    seg_first: jax.Array
    seg_next: jax.Array
    n_used: jax.Array


def _dispatch_plan(route, counts, n_experts, blk, n_slabs):
    t = route.shape[0]
    ids = route[:, 0:TOP_K].astype(I32)
    rank = route[:, 2 * TOP_K:3 * TOP_K].astype(I32)
    counts = counts[0, :n_experts].astype(I32)
    padded = (counts + blk - 1) // blk * blk
    ends = jnp.cumsum(padded)
    starts = ends - padded
    expert_iota = jnp.arange(n_experts, dtype=I32)
    seg_start = jnp.sum(jnp.where(ids[:, :, None] == expert_iota, starts, 0), axis=-1)
    dest = (seg_start + rank).reshape(-1)
    nb = (t * TOP_K) // blk + n_experts
    flat_src = jnp.arange(t * TOP_K, dtype=I32) // TOP_K * n_slabs
    slot_src = jnp.zeros((nb * blk,), I32).at[dest].set(flat_src, unique_indices=True)
    blk_start = jnp.arange(nb, dtype=I32) * blk
    blk_expert = jnp.minimum(jnp.sum((ends[None, :] <= blk_start[:, None]).astype(I32), axis=1), n_experts - 1)
    n_used = ends[-1] // blk
    blk_iota = jnp.arange(nb, dtype=I32)
    used = blk_iota < n_used
    prev_expert = jnp.concatenate([jnp.full((1,), -1, I32), blk_expert[:-1]])
    seg_first = (used & (blk_expert != prev_expert)).astype(I32)
    later_segment = (blk_iota[None, :] > blk_iota[:, None]) & used[None, :] & (blk_expert[None, :] != blk_expert[:, None])
    seg_next = jnp.where(jnp.any(later_segment, axis=1), blk_expert[jnp.argmax(later_segment, axis=1)], -1).astype(I32)
    return DispatchPlan(dest * n_slabs, slot_src, blk_expert.astype(I32), seg_first, seg_next,
                        n_used.astype(I32).reshape(1))


def kernel(x, mem, attn_w_qkv, attn_w_o, attn_rel_bias, conv_w_in, conv_b_in, conv_w_dw, conv_b_dw, conv_ln_g,
           conv_ln_b, conv_w_out, conv_b_out, pool_w, pool_scale, mem_w_q, mem_w_kv, mem_w_o, moe_w_group,
           moe_b_group, moe_w_router, moe_b_router, moe_w_gate, moe_w_up, moe_w_down, ln_g, ln_b):
    batch, seq, d = x.shape
    assert batch == 1 and ln_g.shape[0] == DEPTH
    n_groups = moe_w_group.shape[2]
    per_group = moe_w_router.shape[3]
    n_experts = n_groups * per_group
    assert n_groups + n_experts <= LANES

    xf = x.reshape(seq, d)
    xb = xf.astype(BF16)
    memb = mem.reshape(mem.shape[1], d).astype(BF16)

    for i in range(DEPTH):
        kind = i % N_MIXERS
        j = i // N_MIXERS
        if kind == 0:
            qkv = _matmul(xb, attn_w_qkv, j)
            o = _chunked_attention(qkv, attn_rel_bias[j], d)
            f = _matmul(o, attn_w_o, j)
        elif kind == 1:
            u = _glu_matmul(xb, conv_w_in, j, conv_b_in[j])
            v = _conv_ln_swish(u, conv_w_dw[j], conv_b_dw[j], conv_ln_g[j], conv_ln_b[j])
            f = _matmul(v, conv_w_out, j, conv_b_out[j])
        else:
            f = _multiscale_pool(xf, pool_w[j].astype(BF16), pool_scale[j])

        kv = _matmul(memb, mem_w_kv, i, tm=memb.shape[0])
        w_r = jnp.concatenate([moe_w_group[i], moe_w_router[i].transpose(1, 0, 2).reshape(d, n_experts)], axis=1)
        w_r = jnp.pad(w_r, ((0, 0), (0, LANES - w_r.shape[1]))).astype(BF16)
        b_r = jnp.concatenate([moe_b_group[i], moe_b_router[i].reshape(n_experts)])
        b_r = jnp.pad(b_r, (0, LANES - b_r.shape[0])).reshape(1, LANES).astype(F32)
        x2, x2p, route, counts = _post_block(xf, f, ln_g[i], ln_b[i], mem_w_q[i].astype(BF16), kv,
                                             mem_w_o[i].astype(BF16), w_r, b_r, n_groups, per_group)

        plan = _dispatch_plan(route, counts, n_experts, MOE_BLOCK, d // 2 // LANES)
        yb = _moe_experts(x2p, plan, moe_w_gate[i], moe_w_up[i], moe_w_down[i])
        xf, xb = _moe_combine(x2, route, yb, plan.slot_of, ln_g[i, 2], ln_b[i, 2])

    return xf.reshape(batch, seq, d)
```

```python
import functools
from typing import NamedTuple

import jax
import jax.numpy as jnp
from jax import lax
from jax.experimental import pallas as pl
from jax.experimental.pallas import tpu as pltpu

F32 = jnp.float32
BF16 = jnp.bfloat16
U32 = jnp.uint32
I32 = jnp.int32

DEPTH = 4
N_MIXERS = 3
ATTN_HEADS = 32
CHUNK = 64
LEFT_CHUNKS = 8
POOL_WINDOWS = (2, 4, 8, 16)
MEM_HEADS = 4
TOP_K = 2
LN_EPS = 1e-5
DEEPNORM_ALPHA = (2 * DEPTH) ** 0.25
NEG_BIG = -1e30

V7X_VMEM_BYTES = 64 * 1024 * 1024
VMEM_LIMIT = V7X_VMEM_BYTES - 8 * 1024 * 1024
LANES = 128

ATTN_Q_TILE = 4 * CHUNK
ATTN_HEADS_PER_STEP = 8
MOE_BLOCK = 256
MOE_UP_PIECES = 3
MOE_DOWN_PIECES = 4
ROW_TILE = 256


def _params(n_grid_dims):
    return pltpu.CompilerParams(dimension_semantics=("arbitrary",) * n_grid_dims,
                                vmem_limit_bytes=VMEM_LIMIT)


def _ln_pieces(pieces, g_pieces, b_pieces):
    d = sum(p.shape[-1] for p in pieces)
    mu = sum(jnp.sum(p, axis=-1, keepdims=True) for p in pieces) * (1.0 / d)
    cen = [p - mu for p in pieces]
    var = sum(jnp.sum(c * c, axis=-1, keepdims=True) for c in cen) * (1.0 / d)
    inv = lax.rsqrt(var + LN_EPS)
    return [c * inv * g + b for c, g, b in zip(cen, g_pieces, b_pieces)]


def _pack_halves(lo, hi):
    half_ulp = jnp.uint32(0x8000)
    lo_bits = (lax.bitcast_convert_type(lo, U32) + half_ulp) >> 16
    hi_bits = (lax.bitcast_convert_type(hi, U32) + half_ulp) & jnp.uint32(0xFFFF0000)
    return lo_bits | hi_bits


def _unpack_halves(w):
    lo = lax.bitcast_convert_type(w << 16, F32)
    hi = lax.bitcast_convert_type(w & jnp.uint32(0xFFFF0000), F32)
    return lo, hi


GATHER_PITCH = 24


def _store_slabs(ref, first_slab, packed):
    rows = packed.shape[0]
    n_slabs = ref.shape[0] // rows
    for q in range(packed.shape[1] // LANES):
        ref[pl.ds(first_slab + q, rows, stride=n_slabs), :] = packed[:, q * LANES:(q + 1) * LANES]


def _load_slab_column(ref_2d, s, rows):
    return ref_2d[pl.ds(s, rows, stride=GATHER_PITCH), :]


MM_CAST_ROWS = 512


def _cast_weight_tile(w_ref, w_bf16_ref):
    k = w_ref.shape[0]
    step = min(MM_CAST_ROWS, k)
    for r in range(0, k, step):
        w_bf16_ref[r:r + step, :] = w_ref[r:r + step, :].astype(BF16)


def _mm_kernel(x_ref, w_ref, *rest, has_bias):
    if has_bias:
        b_ref, o_ref, w_bf16_ref = rest
    else:
        o_ref, w_bf16_ref = rest

    @pl.when(pl.program_id(1) == 0)
    def _():
        _cast_weight_tile(w_ref, w_bf16_ref)

    acc = jnp.dot(x_ref[...], w_bf16_ref[...], preferred_element_type=F32)
    if has_bias:
        acc = acc + b_ref[...]
    o_ref[...] = acc.astype(o_ref.dtype)


def _matmul(x, w_stack, layer, bias=None, *, out_dtype=BF16, tm=512, tn=1024):
    m, k = x.shape
    n = w_stack.shape[2]
    tm = min(tm, m)
    tn = min(tn, n)
    assert m % tm == 0 and n % tn == 0 and w_stack.shape[1] == k
    in_specs = [pl.BlockSpec((tm, k), lambda j, i: (i, 0)),
                pl.BlockSpec((None, k, tn), lambda j, i: (layer, 0, j))]
    args = [x, w_stack]
    if bias is not None:
        in_specs.append(pl.BlockSpec((1, tn), lambda j, i: (0, j)))
        args.append(bias.reshape(1, n).astype(F32))
    return pl.pallas_call(
        functools.partial(_mm_kernel, has_bias=bias is not None),
        grid=(n // tn, m // tm),
        in_specs=in_specs,
        out_specs=pl.BlockSpec((tm, tn), lambda j, i: (i, j)),
        out_shape=jax.ShapeDtypeStruct((m, n), out_dtype),
        scratch_shapes=[pltpu.VMEM((k, tn), BF16)],
        compiler_params=_params(2),
    )(*args)


def _attn_kernel(q_ref, k0_ref, k1_ref, k2_ref, v0_ref, v1_ref, v2_ref, u_ref, o_ref, bias_ref, *, hb, dh, qt):
    i = pl.program_id(1)

    @pl.when(i == 0)
    def _():
        shift = CHUNK.bit_length() - 1
        qc = lax.shift_right_logical(lax.broadcasted_iota(I32, (qt, 3 * qt), 0), shift)
        kc = lax.shift_right_logical(lax.broadcasted_iota(I32, (qt, 3 * qt), 1), shift)
        band = (kc >= qc) & (kc <= qc + LEFT_CHUNKS)
        for h in range(hb):
            rows = jnp.broadcast_to(u_ref[h:h + 1, :], (qt, u_ref.shape[1]))
            toep = pltpu.roll(rows, 0, 1, stride=1, stride_axis=0)
            bias_ref[h] = jnp.where(band, toep[:, 0:3 * qt], NEG_BIG)

    pens = [jnp.where(i + j >= 2, 0.0, NEG_BIG).astype(F32) for j in range(2)] + [None]
    scale = dh ** -0.5
    k_refs = (k0_ref, k1_ref, k2_ref)
    v_refs = (v0_ref, v1_ref, v2_ref)
    for h in range(hb):
        sl = slice(h * dh, (h + 1) * dh)
        q = (q_ref[:, sl].astype(F32) * scale).astype(BF16)
        s = []
        for j in range(3):
            sj = lax.dot_general(q, k_refs[j][:, sl], (((1,), (1,)), ((), ())), preferred_element_type=F32)
            sj = sj + bias_ref[h, :, j * qt:(j + 1) * qt]
            if pens[j] is not None:
                sj = sj + pens[j]
            s.append(sj)
        m = jnp.maximum(jnp.maximum(jnp.max(s[0], axis=-1, keepdims=True), jnp.max(s[1], axis=-1, keepdims=True)),
                        jnp.max(s[2], axis=-1, keepdims=True))
        p = [jnp.exp(sj - m) for sj in s]
        l = sum(jnp.sum(pj, axis=-1, keepdims=True) for pj in p)
        o = sum(jnp.dot(p[j].astype(BF16), v_refs[j][:, sl], preferred_element_type=F32) for j in range(3))
        o_ref[:, sl] = (o / l).astype(o_ref.dtype)


def _attn_offset_bias(rel_table, qt):
    max_rel = (rel_table.shape[1] - 1) // 2
    kw = 3 * qt
    period = 4 * qt
    m = jnp.arange(period)
    off = jnp.where(m < kw, m, m - period)
    idx = jnp.clip(2 * qt - off, -max_rel, max_rel) + max_rel
    return jnp.take(rel_table.astype(F32), idx, axis=1)


def _chunked_attention(qkv, rel_table, d):
    t = qkv.shape[0]
    qt = ATTN_Q_TILE
    hb = ATTN_HEADS_PER_STEP
    dh = d // ATTN_HEADS
    assert LEFT_CHUNKS * CHUNK == 2 * qt and t % qt == 0 and ATTN_HEADS % hb == 0
    w = hb * dh
    ng = d // w
    u = _attn_offset_bias(rel_table, qt)

    def kv_spec(j, base):
        return pl.BlockSpec((qt, w), lambda g, i: (jnp.maximum(i - 2 + j, 0), base + g))

    return pl.pallas_call(
        functools.partial(_attn_kernel, hb=hb, dh=dh, qt=qt),
        grid=(ng, t // qt),
        in_specs=[pl.BlockSpec((qt, w), lambda g, i: (i, g))]
                 + [kv_spec(j, ng) for j in range(3)]
                 + [kv_spec(j, 2 * ng) for j in range(3)]
                 + [pl.BlockSpec((hb, u.shape[1]), lambda g, i: (g, 0))],
        out_specs=pl.BlockSpec((qt, w), lambda g, i: (i, g)),
        out_shape=jax.ShapeDtypeStruct((t, d), BF16),
        scratch_shapes=[pltpu.VMEM((hb, qt, 3 * qt), F32)],
        compiler_params=_params(2),
    )(qkv, qkv, qkv, qkv, qkv, qkv, qkv, u)


def _glu_kernel(x_ref, wa_ref, wg_ref, ba_ref, bg_ref, o_ref, wa_bf16_ref, wg_bf16_ref):
    @pl.when(pl.program_id(1) == 0)
    def _():
        _cast_weight_tile(wa_ref, wa_bf16_ref)
        _cast_weight_tile(wg_ref, wg_bf16_ref)

    x = x_ref[...]
    a = jnp.dot(x, wa_bf16_ref[...], preferred_element_type=F32) + ba_ref[...]
    g = jnp.dot(x, wg_bf16_ref[...], preferred_element_type=F32) + bg_ref[...]
    o_ref[...] = (a * jax.nn.sigmoid(g)).astype(o_ref.dtype)


def _glu_matmul(x, w_stack, layer, b, *, tm=512, tn=512):
    m, k = x.shape
    n = w_stack.shape[2] // 2
    assert m % tm == 0 and n % tn == 0
    nb = n // tn
    b2 = b.reshape(1, 2 * n).astype(F32)
    return pl.pallas_call(
        _glu_kernel,
        grid=(nb, m // tm),
        in_specs=[pl.BlockSpec((tm, k), lambda j, i: (i, 0)),
                  pl.BlockSpec((None, k, tn), lambda j, i: (layer, 0, j)),
                  pl.BlockSpec((None, k, tn), lambda j, i: (layer, 0, nb + j)),
                  pl.BlockSpec((1, tn), lambda j, i: (0, j)),
                  pl.BlockSpec((1, tn), lambda j, i: (0, nb + j))],
        out_specs=pl.BlockSpec((tm, tn), lambda j, i: (i, j)),
        out_shape=jax.ShapeDtypeStruct((m, n), BF16),
        scratch_shapes=[pltpu.VMEM((k, tn), BF16), pltpu.VMEM((k, tn), BF16)],
        compiler_params=_params(2),
    )(x, w_stack, w_stack, b2, b2)


CONV_HALO = 32
CONV_ROWS = 64
CONV_COLS = 128


def _conv_kernel(u_ref, halo_ref, w_ref, bdw_ref, g_ref, b_ref, o_ref, ext_ref, acc_ref, *, width, tm):
    i = pl.program_id(0)
    d = u_ref.shape[1]
    halo = halo_ref[...].astype(F32)
    ext_ref[0:CONV_HALO, :] = jnp.where(i > 0, halo, 0.0)
    ext_ref[CONV_HALO:, :] = u_ref[...].astype(F32)
    off = CONV_HALO - (width - 1)

    def col_chunk(c, carry):
        c0 = pl.multiple_of(c * CONV_COLS, CONV_COLS)
        w = w_ref[:, pl.ds(c0, CONV_COLS)]
        for r0 in range(0, tm, CONV_ROWS):
            acc = jnp.zeros((CONV_ROWS, CONV_COLS), F32)
            for b in range(8):
                taps = [k for k in range(b, width, 8)]
                span = CONV_ROWS + (taps[-1] - b)
                v = ext_ref[pl.ds(r0 + off + b, span), pl.ds(c0, CONV_COLS)]
                for k in taps:
                    acc = acc + v[k - b:k - b + CONV_ROWS, :] * w[k:k + 1, :]
            acc_ref[r0:r0 + CONV_ROWS, pl.ds(c0, CONV_COLS)] = acc + bdw_ref[:, pl.ds(c0, CONV_COLS)]
        return carry

    lax.fori_loop(0, d // CONV_COLS, col_chunk, 0)
    (y,) = _ln_pieces([acc_ref[...]], [g_ref[...]], [b_ref[...]])
    o_ref[...] = (y * jax.nn.sigmoid(y)).astype(o_ref.dtype)


def _conv_ln_swish(u, w_dw, b_dw, ln_g, ln_b, *, tm=ROW_TILE):
    t, d = u.shape
    width = w_dw.shape[0]
    assert width - 1 <= CONV_HALO and t % tm == 0 and tm % CONV_HALO == 0 and d % CONV_COLS == 0
    hb = tm // CONV_HALO
    row = lambda a: a.reshape(1, d).astype(F32)
    return pl.pallas_call(
        functools.partial(_conv_kernel, width=width, tm=tm),
        grid=(t // tm,),
        in_specs=[pl.BlockSpec((tm, d), lambda i: (i, 0)),
                  pl.BlockSpec((CONV_HALO, d), lambda i: (jnp.maximum(i * hb - 1, 0), 0)),
                  pl.BlockSpec((width, d), lambda i: (0, 0)),
                  pl.BlockSpec((1, d), lambda i: (0, 0)),
                  pl.BlockSpec((1, d), lambda i: (0, 0)),
                  pl.BlockSpec((1, d), lambda i: (0, 0))],
        out_specs=pl.BlockSpec((tm, d), lambda i: (i, 0)),
        out_shape=jax.ShapeDtypeStruct((t, d), BF16),
        scratch_shapes=[pltpu.VMEM((tm + CONV_HALO, d), F32), pltpu.VMEM((tm, d), F32)],
        compiler_params=_params(1),
    )(u, u, w_dw.astype(F32), row(b_dw), row(ln_g), row(ln_b))


POOL_HALO = 16


def _pool_kernel(x_ref, halo_ref, w_ref, scale_ref, o_ref, ext_ref, *, tm, gw):
    i = pl.program_id(0)
    ext_ref[0:POOL_HALO, :] = jnp.where(i > 0, halo_ref[...], 0.0)
    ext_ref[POOL_HALO:, :] = x_ref[...]
    t_abs = i * tm + lax.broadcasted_iota(I32, (tm, 1), 0)
    for g, win in enumerate(POOL_WINDOWS):
        cols = slice(g * gw, (g + 1) * gw)
        acc = ext_ref[POOL_HALO:POOL_HALO + tm, cols]
        for j in range(1, win):
            acc = acc + ext_ref[POOL_HALO - j:POOL_HALO - j + tm, cols]
        cnt = jnp.minimum(t_abs + 1, win).astype(F32)
        pooled = acc / cnt - ext_ref[POOL_HALO:POOL_HALO + tm, cols]
        y = jnp.dot(pooled.astype(BF16), w_ref[g], preferred_element_type=F32)
        o_ref[:, cols] = (y * scale_ref[:, cols]).astype(o_ref.dtype)


def _multiscale_pool(x, w_pool, scale, *, tm=ROW_TILE):
    t, d = x.shape
    ng, gw, _ = w_pool.shape
    assert ng == len(POOL_WINDOWS) and max(POOL_WINDOWS) <= POOL_HALO and t % tm == 0
    hb = tm // POOL_HALO
    return pl.pallas_call(
        functools.partial(_pool_kernel, tm=tm, gw=gw),
        grid=(t // tm,),
        in_specs=[pl.BlockSpec((tm, d), lambda i: (i, 0)),
                  pl.BlockSpec((POOL_HALO, d), lambda i: (jnp.maximum(i * hb - 1, 0), 0)),
                  pl.BlockSpec((ng, gw, gw), lambda i: (0, 0, 0)),
                  pl.BlockSpec((1, d), lambda i: (0, 0))],
        out_specs=pl.BlockSpec((tm, d), lambda i: (i, 0)),
        out_shape=jax.ShapeDtypeStruct((t, d), BF16),
        scratch_shapes=[pltpu.VMEM((tm + POOL_HALO, d), F32)],
        compiler_params=_params(1),
    )(x, x, w_pool, scale.reshape(1, d).astype(F32))


def _post_kernel(x_ref, f_ref, g0_ref, b0_ref, g1_ref, b1_ref, wq_ref, kv_ref, wo_ref, wr_ref, br_ref,
                 x2_ref, x2p_ref, route_ref, counts_ref, *, n_groups, per_group):
    d = x_ref.shape[1]
    half = d // 2
    n_mem, inner2 = kv_ref.shape
    inner = inner2 // 2
    dh = inner // MEM_HEADS
    y = DEEPNORM_ALPHA * x_ref[...] + f_ref[...].astype(F32)
    (x1,) = _ln_pieces([y], [g0_ref[...]], [b0_ref[...]])
    q = jnp.dot(x1.astype(BF16), wq_ref[...], preferred_element_type=F32) * (dh ** -0.5)
    heads = []
    for h in range(MEM_HEADS):
        qh = q[:, h * dh:(h + 1) * dh].astype(BF16)
        kh = kv_ref[:, h * dh:(h + 1) * dh]
        vh = kv_ref[:, inner + h * dh:inner + (h + 1) * dh]
        s = lax.dot_general(qh, kh, (((1,), (1,)), ((), ())), preferred_element_type=F32)
        p = jnp.exp(s - jnp.max(s, axis=-1, keepdims=True))
        l = jnp.sum(p, axis=-1, keepdims=True)
        heads.append((jnp.dot(p.astype(BF16), vh, preferred_element_type=F32) / l).astype(BF16))
    o = jnp.concatenate(heads, axis=1)
    c = jnp.dot(o, wo_ref[...], preferred_element_type=F32)
    (x2,) = _ln_pieces([DEEPNORM_ALPHA * x1 + c], [g1_ref[...]], [b1_ref[...]])
    x2_ref[...] = x2
    _store_slabs(x2p_ref, 0, _pack_halves(x2[:, :half], x2[:, half:]))

    lg = jnp.dot(x2.astype(BF16), wr_ref[...], preferred_element_type=F32) + br_ref[...]
    lane = lax.broadcasted_iota(I32, lg.shape, 1).astype(F32)
    far = float(LANES)

    def first_lane_of_max(v):
        m = jnp.max(v, axis=-1, keepdims=True)
        return m, jnp.min(jnp.where(v == m, lane, far), axis=-1, keepdims=True)

    gl = jnp.where(lane < n_groups, lg, NEG_BIG)
    gm, gsel = first_lane_of_max(gl)
    gw = 1.0 / jnp.sum(jnp.exp(gl - gm), axis=-1, keepdims=True)
    lo = n_groups + gsel * per_group
    el = jnp.where((lane >= lo) & (lane < lo + per_group), lg, NEG_BIG)
    m1, i1 = first_lane_of_max(el)
    m2, i2 = first_lane_of_max(jnp.where(lane == i1, NEG_BIG, el))
    e2 = jnp.exp(m2 - m1)
    den = 1.0 + e2
    id1 = i1 - n_groups
    id2 = i2 - n_groups

    @pl.when(pl.program_id(0) == 0)
    def _():
        counts_ref[...] = jnp.zeros(counts_ref.shape, counts_ref.dtype)

    hot1 = lane == id1
    hot2 = lane == id2
    hot = jnp.where(hot1 | hot2, 1.0, 0.0)
    tm = lg.shape[0]
    strict_lower = (lax.broadcasted_iota(I32, (tm, tm), 0) > lax.broadcasted_iota(I32, (tm, tm), 1))
    before = jnp.dot(jnp.where(strict_lower, 1.0, 0.0).astype(BF16), hot.astype(BF16),
                     preferred_element_type=F32) + counts_ref[...]
    rank1 = jnp.sum(jnp.where(hot1, before, 0.0), axis=-1, keepdims=True)
    rank2 = jnp.sum(jnp.where(hot2, before, 0.0), axis=-1, keepdims=True)
    counts_ref[...] = counts_ref[...] + jnp.sum(hot, axis=0, keepdims=True)

    route = jnp.where(lane == 0, id1,
                      jnp.where(lane == 1, id2,
                                jnp.where(lane == 2, gw / den,
                                          jnp.where(lane == 3, gw * e2 / den,
                                                    jnp.where(lane == 4, rank1,
                                                              jnp.where(lane == 5, rank2, 0.0))))))
    route_ref[...] = route


def _post_block(x, f, ln_g, ln_b, w_q, kv, w_o, w_r, b_r, n_groups, per_group, *, tm=ROW_TILE):
    t, d = x.shape
    inner = w_q.shape[1]
    n_mem = kv.shape[0]
    n_slabs = d // 2 // LANES
    row = lambda a: a.reshape(1, d).astype(F32)
    full = lambda shape: pl.BlockSpec(shape, lambda i: (0,) * len(shape))
    tile = lambda w: pl.BlockSpec((tm, w), lambda i: (i, 0))
    return pl.pallas_call(
        functools.partial(_post_kernel, n_groups=n_groups, per_group=per_group),
        grid=(t // tm,),
        in_specs=[tile(d), tile(d), full((1, d)), full((1, d)), full((1, d)), full((1, d)),
                  full((d, inner)), full((n_mem, 2 * inner)), full((inner, d)), full((d, LANES)), full((1, LANES))],
        out_specs=[tile(d), pl.BlockSpec((tm * n_slabs, LANES), lambda i: (i, 0)), tile(LANES), full((1, LANES))],
        out_shape=[jax.ShapeDtypeStruct((t, d), F32), jax.ShapeDtypeStruct((t * n_slabs, LANES), U32),
                   jax.ShapeDtypeStruct((t, LANES), F32), jax.ShapeDtypeStruct((1, LANES), F32)],
        compiler_params=_params(1),
    )(x, f, row(ln_g[0]), row(ln_b[0]), row(ln_g[1]), row(ln_b[1]), w_q, kv, w_o, w_r, b_r)


MOE_CAST_ROWS = 128
MOE_WEIGHT_COPIES = 8


def _moe_kernel(be_ref, first_ref, next_ref, src_ref, nu_ref, xg_hbm, wg_hbm, wu_hbm, wd_hbm, yb_ref,
                xbuf0, xbuf1, stage_g, stage_u, stage_d, wgu_ref, wd_ref, sem, wsem, *, blk, layer):
    i = pl.program_id(0)
    n_used = nu_ref[0]
    bufs = (xbuf0, xbuf1)
    n_slabs = yb_ref.shape[0] // blk
    half = n_slabs * LANES
    de = wd_ref.shape[0]
    d = wd_ref.shape[1]

    def weight_copies(e):
        copies = []
        for k, (w_hbm, stage) in enumerate(((wg_hbm, stage_g), (wu_hbm, stage_u), (wd_hbm, stage_d))):
            rows = stage.shape[0] // MOE_WEIGHT_COPIES
            for c in range(MOE_WEIGHT_COPIES):
                piece = pl.ds(c * rows, rows)
                copies.append(pltpu.make_async_copy(w_hbm.at[layer, e, piece, :], stage.at[piece, :], wsem.at[k]))
        return copies

    @pl.when(i == 0)
    def _():
        for c in weight_copies(be_ref[0]):
            c.start()

    @pl.when((i < n_used) & (first_ref[i] == 1))
    def _():
        for c in weight_copies(be_ref[i]):
            c.wait()

        def cast_up(r, carry):
            rows = pl.ds(pl.multiple_of(r * MOE_CAST_ROWS, MOE_CAST_ROWS), MOE_CAST_ROWS)
            wgu_ref[rows, 0:de] = stage_g[rows, :].astype(BF16)
            wgu_ref[rows, de:2 * de] = stage_u[rows, :].astype(BF16)
            return carry

        def cast_down(r, carry):
            rows = pl.ds(pl.multiple_of(r * MOE_CAST_ROWS, MOE_CAST_ROWS), MOE_CAST_ROWS)
            for c0 in range(0, d, 1024):
                wd_ref[rows, c0:c0 + 1024] = stage_d[rows, c0:c0 + 1024].astype(BF16)
            return carry

        lax.fori_loop(0, d // MOE_CAST_ROWS, cast_up, 0)
        lax.fori_loop(0, de // MOE_CAST_ROWS, cast_down, 0)

        @pl.when(next_ref[i] >= 0)
        def _():
            for c in weight_copies(next_ref[i]):
                c.start()

    def row_copy(j, r, s):
        src = pl.multiple_of(src_ref[j * blk + r], n_slabs)
        return pltpu.make_async_copy(xg_hbm.at[pl.ds(src, n_slabs), :],
                                     bufs[s].at[pl.ds(r * GATHER_PITCH, n_slabs), :], sem.at[s])

    def wait_block(s):
        pltpu.make_async_copy(xg_hbm.at[pl.ds(0, blk * n_slabs), :], bufs[s].at[pl.ds(0, blk * n_slabs), :],
                              sem.at[s]).wait()

    @pl.when(i == 0)
    def _():
        def body(r, carry):
            src = pl.multiple_of(src_ref[r], n_slabs)
            dst = pl.multiple_of(r * GATHER_PITCH, 8)
            pltpu.make_async_copy(xg_hbm.at[pl.ds(src, n_slabs), :], xbuf0.at[pl.ds(dst, n_slabs), :],
                                  sem.at[0]).start()
            return carry
        lax.fori_loop(0, blk, body, 0)

    def expert_block(s):
        wait_block(s)
        los, his = [], []
        for q in range(n_slabs):
            lo_q, hi_q = _unpack_halves(_load_slab_column(bufs[s], q, blk))
            los.append(lo_q.astype(BF16))
            his.append(hi_q.astype(BF16))
        lo = jnp.concatenate(los, axis=1)
        hi = jnp.concatenate(his, axis=1)
        for r in range(blk):
            row_copy(i + 1, r, 1 - s).start()
        wu = 2 * de // MOE_UP_PIECES
        au = []
        for c in range(MOE_UP_PIECES):
            cols = slice(c * wu, (c + 1) * wu)
            au.append(jnp.dot(lo, wgu_ref[0:half, cols], preferred_element_type=F32)
                      + jnp.dot(hi, wgu_ref[half:, cols], preferred_element_type=F32))
        au = jnp.concatenate(au, axis=1)
        a = au[:, 0:de]
        u = au[:, de:]
        hmid = (a * jax.nn.sigmoid(a) * u).astype(BF16)
        wdn = half // MOE_DOWN_PIECES
        for c in range(MOE_DOWN_PIECES):
            y_lo = jnp.dot(hmid, wd_ref[:, c * wdn:(c + 1) * wdn], preferred_element_type=F32)
            y_hi = jnp.dot(hmid, wd_ref[:, half + c * wdn:half + (c + 1) * wdn], preferred_element_type=F32)
            _store_slabs(yb_ref, c * (wdn // LANES), _pack_halves(y_lo, y_hi))

    for s in range(2):
        pl.when((i < n_used) & (i % 2 == s))(functools.partial(expert_block, s))
        pl.when((i == n_used) & (i % 2 == s))(functools.partial(wait_block, s))

    @pl.when(i >= n_used)
    def _():
        yb_ref[...] = jnp.zeros(yb_ref.shape, yb_ref.dtype)


def _moe_experts(xg, plan, w_gate, w_up, w_down, layer, *, blk=MOE_BLOCK):
    n_slots = plan.slot_src.shape[0]
    nb = n_slots // blk
    de, d = w_down.shape[2:]
    n_slabs = d // 2 // LANES
    assert n_slabs <= GATHER_PITCH and GATHER_PITCH % 8 == 0
    assert d % MOE_CAST_ROWS == 0 and de % MOE_CAST_ROWS == 0 and d % 1024 == 0
    buf = pltpu.VMEM((blk * GATHER_PITCH, LANES), U32)
    any_spec = pl.BlockSpec(memory_space=pl.ANY)
    grid_spec = pltpu.PrefetchScalarGridSpec(
        num_scalar_prefetch=5,
        grid=(nb,),
        in_specs=[any_spec, any_spec, any_spec, any_spec],
        out_specs=pl.BlockSpec((blk * n_slabs, LANES), lambda i, *_: (i, 0)),
        scratch_shapes=[buf, buf,
                        pltpu.VMEM((d, de), F32), pltpu.VMEM((d, de), F32), pltpu.VMEM((de, d), F32),
                        pltpu.VMEM((d, 2 * de), BF16), pltpu.VMEM((de, d), BF16),
                        pltpu.SemaphoreType.DMA((2,)), pltpu.SemaphoreType.DMA((3,))],
    )
    return pl.pallas_call(
        functools.partial(_moe_kernel, blk=blk, layer=layer),
        grid_spec=grid_spec,
        out_shape=jax.ShapeDtypeStruct((n_slots * n_slabs, LANES), U32),
        compiler_params=_params(1),
    )(plan.blk_expert, plan.seg_first, plan.seg_next, plan.slot_src, plan.n_used, xg, w_gate, w_up, w_down)


def _combine_kernel(src_ref, x2_ref, route_ref, g_ref, b_ref, yb_hbm, x3_ref, x3b_ref, ybuf, sem, *, tm):
    i = pl.program_id(0)
    n = pl.num_programs(0)
    slot = i % 2
    d = x2_ref.shape[1]
    half = d // 2
    n_slabs = half // LANES

    def issue(j, s):
        base = j * tm * TOP_K

        def body(r, carry):
            dst = pl.multiple_of(r * GATHER_PITCH, 8)
            for k in range(TOP_K):
                src = pl.multiple_of(src_ref[base + TOP_K * r + k], n_slabs)
                pltpu.make_async_copy(yb_hbm.at[pl.ds(src, n_slabs), :], ybuf.at[s, k, pl.ds(dst, n_slabs), :],
                                      sem.at[s]).start()
            return carry

        lax.fori_loop(0, tm, body, 0)

    @pl.when(i == 0)
    def _():
        issue(0, 0)

    @pl.when(i + 1 < n)
    def _():
        issue(i + 1, 1 - slot)

    for k in range(TOP_K):
        pltpu.make_async_copy(yb_hbm.at[pl.ds(0, tm * n_slabs), :], ybuf.at[slot, k, pl.ds(0, tm * n_slabs), :],
                              sem.at[slot]).wait()
    gates = [route_ref[:, TOP_K + k:TOP_K + k + 1] for k in range(TOP_K)]
    y_lo, y_hi = [], []
    for q in range(n_slabs):
        lo = hi = None
        for k in range(TOP_K):
            lo_k, hi_k = _unpack_halves(ybuf[slot, k, pl.ds(q, tm, stride=GATHER_PITCH), :])
            lo = lo_k * gates[k] if lo is None else lo + lo_k * gates[k]
            hi = hi_k * gates[k] if hi is None else hi + hi_k * gates[k]
        y_lo.append(DEEPNORM_ALPHA * x2_ref[:, q * LANES:(q + 1) * LANES] + lo)
        y_hi.append(DEEPNORM_ALPHA * x2_ref[:, half + q * LANES:half + (q + 1) * LANES] + hi)
    cols = [slice(c * LANES, (c + 1) * LANES) for c in range(2 * n_slabs)]
    out = _ln_pieces(y_lo + y_hi, [g_ref[:, c] for c in cols], [b_ref[:, c] for c in cols])
    for c, o in zip(cols, out):
        x3_ref[:, c] = o
        x3b_ref[:, c] = o.astype(BF16)


def _moe_combine(x2, route, yb, slot_of, ln_g, ln_b, *, tm=ROW_TILE):
    t, d = x2.shape
    half = d // 2
    row = lambda a: a.reshape(1, d).astype(F32)
    grid_spec = pltpu.PrefetchScalarGridSpec(
        num_scalar_prefetch=1,
        grid=(t // tm,),
        in_specs=[pl.BlockSpec((tm, d), lambda i, dst: (i, 0)),
                  pl.BlockSpec((tm, LANES), lambda i, dst: (i, 0)),
                  pl.BlockSpec((1, d), lambda i, dst: (0, 0)),
                  pl.BlockSpec((1, d), lambda i, dst: (0, 0)),
                  pl.BlockSpec(memory_space=pl.ANY)],
        out_specs=[pl.BlockSpec((tm, d), lambda i, dst: (i, 0)), pl.BlockSpec((tm, d), lambda i, dst: (i, 0))],
        scratch_shapes=[pltpu.VMEM((2, TOP_K, tm * GATHER_PITCH, LANES), U32), pltpu.SemaphoreType.DMA((2,))],
    )
    return pl.pallas_call(
        functools.partial(_combine_kernel, tm=tm),
        grid_spec=grid_spec,
        out_shape=[jax.ShapeDtypeStruct((t, d), F32), jax.ShapeDtypeStruct((t, d), BF16)],
        compiler_params=_params(1),
    )(slot_of, x2, route, row(ln_g), row(ln_b), yb)


class DispatchPlan(NamedTuple):
    slot_of: jax.Array
    slot_src: jax.Array
    blk_expert: jax.Array
    seg_first: jax.Array
    seg_next: jax.Array
    n_used: jax.Array


def _dispatch_plan(route, counts, n_experts, blk, n_slabs):
    t = route.shape[0]
    ids = route[:, 0:TOP_K].astype(I32)
    rank = route[:, 2 * TOP_K:3 * TOP_K].astype(I32)
    counts = counts[0, :n_experts].astype(I32)
    padded = (counts + blk - 1) // blk * blk
    ends = jnp.cumsum(padded)
    starts = ends - padded
    expert_iota = jnp.arange(n_experts, dtype=I32)
    seg_start = jnp.sum(jnp.where(ids[:, :, None] == expert_iota, starts, 0), axis=-1)
    nb = (t * TOP_K) // blk + n_experts
    dest = jnp.clip(seg_start + rank, 0, nb * blk - 1).reshape(-1)
    flat_src = jnp.arange(t * TOP_K, dtype=I32) // TOP_K * n_slabs
    slot_src = jnp.zeros((nb * blk,), I32).at[dest].set(flat_src, unique_indices=True)
    blk_start = jnp.arange(nb, dtype=I32) * blk
    blk_expert = jnp.minimum(jnp.sum((ends[None, :] <= blk_start[:, None]).astype(I32), axis=1), n_experts - 1)
    n_used = ends[-1] // blk
    blk_iota = jnp.arange(nb, dtype=I32)
    used = blk_iota < n_used
    prev_expert = jnp.concatenate([jnp.full((1,), -1, I32), blk_expert[:-1]])
    seg_first = (used & (blk_expert != prev_expert)).astype(I32)
    later_segment = (blk_iota[None, :] > blk_iota[:, None]) & used[None, :] & (blk_expert[None, :] != blk_expert[:, None])
    seg_next = jnp.where(jnp.any(later_segment, axis=1), blk_expert[jnp.argmax(later_segment, axis=1)], -1).astype(I32)
    return DispatchPlan(dest * n_slabs, slot_src, blk_expert.astype(I32), seg_first, seg_next,
                        n_used.astype(I32).reshape(1))


def kernel(x, mem, attn_w_qkv, attn_w_o, attn_rel_bias, conv_w_in, conv_b_in, conv_w_dw, conv_b_dw, conv_ln_g,
           conv_ln_b, conv_w_out, conv_b_out, pool_w, pool_scale, mem_w_q, mem_w_kv, mem_w_o, moe_w_group,
           moe_b_group, moe_w_router, moe_b_router, moe_w_gate, moe_w_up, moe_w_down, ln_g, ln_b):
    batch, seq, d = x.shape
    assert batch == 1 and ln_g.shape[0] == DEPTH
    n_groups = moe_w_group.shape[2]
    per_group = moe_w_router.shape[3]
    n_experts = n_groups * per_group
    assert n_groups + n_experts <= LANES

    xf = x.reshape(seq, d)
    xb = xf.astype(BF16)
    memb = mem.reshape(mem.shape[1], d).astype(BF16)

    for i in range(DEPTH):
        kind = i % N_MIXERS
        j = i // N_MIXERS
        if kind == 0:
            qkv = _matmul(xb, attn_w_qkv, j)
            o = _chunked_attention(qkv, attn_rel_bias[j], d)
            f = _matmul(o, attn_w_o, j)
        elif kind == 1:
            u = _glu_matmul(xb, conv_w_in, j, conv_b_in[j])
            v = _conv_ln_swish(u, conv_w_dw[j], conv_b_dw[j], conv_ln_g[j], conv_ln_b[j])
            f = _matmul(v, conv_w_out, j, conv_b_out[j])
        else:
            f = _multiscale_pool(xf, pool_w[j].astype(BF16), pool_scale[j])

        kv = _matmul(memb, mem_w_kv, i, tm=memb.shape[0])
        w_r = jnp.concatenate([moe_w_group[i], moe_w_router[i].transpose(1, 0, 2).reshape(d, n_experts)], axis=1)
        w_r = jnp.pad(w_r, ((0, 0), (0, LANES - w_r.shape[1]))).astype(BF16)
        b_r = jnp.concatenate([moe_b_group[i], moe_b_router[i].reshape(n_experts)])
        b_r = jnp.pad(b_r, (0, LANES - b_r.shape[0])).reshape(1, LANES).astype(F32)
        x2, x2p, route, counts = _post_block(xf, f, ln_g[i], ln_b[i], mem_w_q[i].astype(BF16), kv,
                                             mem_w_o[i].astype(BF16), w_r, b_r, n_groups, per_group)

        plan = _dispatch_plan(route, counts, n_experts, MOE_BLOCK, d // 2 // LANES)
        yb = _moe_experts(x2p, plan, moe_w_gate, moe_w_up, moe_w_down, i)
        xf, xb = _moe_combine(x2, route, yb, plan.slot_of, ln_g[i, 2], ln_b[i, 2])

    return xf.reshape(batch, seq, d)
```

```python
import functools
from typing import NamedTuple

import jax
import jax.numpy as jnp
from jax import lax
from jax.experimental import pallas as pl
from jax.experimental.pallas import tpu as pltpu

F32 = jnp.float32
BF16 = jnp.bfloat16
U32 = jnp.uint32
I32 = jnp.int32

DEPTH = 4
N_MIXERS = 3
ATTN_HEADS = 32
CHUNK = 64
LEFT_CHUNKS = 8
POOL_WINDOWS = (2, 4, 8, 16)
MEM_HEADS = 4
TOP_K = 2
LN_EPS = 1e-5
DEEPNORM_ALPHA = (2 * DEPTH) ** 0.25
NEG_BIG = -1e30

V7X_VMEM_BYTES = 64 * 1024 * 1024
VMEM_LIMIT = V7X_VMEM_BYTES - 8 * 1024 * 1024
LANES = 128

ATTN_Q_TILE = 4 * CHUNK
ATTN_HEADS_PER_STEP = 8
MOE_BLOCK = 256
MOE_UP_PIECES = 3
MOE_DOWN_PIECES = 8
ROW_TILE = 256


def _params(n_grid_dims):
    return pltpu.CompilerParams(dimension_semantics=("arbitrary",) * n_grid_dims,
                                vmem_limit_bytes=VMEM_LIMIT)


def _ln_pieces(pieces, g_pieces, b_pieces):
    d = sum(p.shape[-1] for p in pieces)
    mu = sum(jnp.sum(p, axis=-1, keepdims=True) for p in pieces) * (1.0 / d)
    cen = [p - mu for p in pieces]
    var = sum(jnp.sum(c * c, axis=-1, keepdims=True) for c in cen) * (1.0 / d)
    inv = lax.rsqrt(var + LN_EPS)
    return [c * inv * g + b for c, g, b in zip(cen, g_pieces, b_pieces)]


def _pack_halves(lo, hi):
    half_ulp = jnp.uint32(0x8000)
    lo_bits = (lax.bitcast_convert_type(lo, U32) + half_ulp) >> 16
    hi_bits = (lax.bitcast_convert_type(hi, U32) + half_ulp) & jnp.uint32(0xFFFF0000)
    return lo_bits | hi_bits


def _unpack_halves(w):
    lo = lax.bitcast_convert_type(w << 16, F32)
    hi = lax.bitcast_convert_type(w & jnp.uint32(0xFFFF0000), F32)
    return lo, hi


GATHER_PITCH = 24


def _store_slabs(ref, first_slab, packed):
    rows = packed.shape[0]
    n_slabs = ref.shape[0] // rows
    for q in range(packed.shape[1] // LANES):
        ref[pl.ds(first_slab + q, rows, stride=n_slabs), :] = packed[:, q * LANES:(q + 1) * LANES]


def _load_slab_column(ref_2d, s, rows):
    return ref_2d[pl.ds(s, rows, stride=GATHER_PITCH), :]


MM_CAST_ROWS = 512


def _cast_weight_tile(w_ref, w_bf16_ref):
    k = w_ref.shape[0]
    step = min(MM_CAST_ROWS, k)
    for r in range(0, k, step):
        w_bf16_ref[r:r + step, :] = w_ref[r:r + step, :].astype(BF16)


def _mm_kernel(x_ref, w_ref, *rest, has_bias):
    if has_bias:
        b_ref, o_ref, w_bf16_ref = rest
    else:
        o_ref, w_bf16_ref = rest

    @pl.when(pl.program_id(1) == 0)
    def _():
        _cast_weight_tile(w_ref, w_bf16_ref)

    acc = jnp.dot(x_ref[...], w_bf16_ref[...], preferred_element_type=F32)
    if has_bias:
        acc = acc + b_ref[...]
    o_ref[...] = acc.astype(o_ref.dtype)


def _matmul(x, w_stack, layer, bias=None, *, out_dtype=BF16, tm=512, tn=1024):
    m, k = x.shape
    n = w_stack.shape[2]
    tm = min(tm, m)
    tn = min(tn, n)
    assert m % tm == 0 and n % tn == 0 and w_stack.shape[1] == k
    in_specs = [pl.BlockSpec((tm, k), lambda j, i: (i, 0)),
                pl.BlockSpec((None, k, tn), lambda j, i: (layer, 0, j))]
    args = [x, w_stack]
    if bias is not None:
        in_specs.append(pl.BlockSpec((1, tn), lambda j, i: (0, j)))
        args.append(bias.reshape(1, n).astype(F32))
    return pl.pallas_call(
        functools.partial(_mm_kernel, has_bias=bias is not None),
        grid=(n // tn, m // tm),
        in_specs=in_specs,
        out_specs=pl.BlockSpec((tm, tn), lambda j, i: (i, j)),
        out_shape=jax.ShapeDtypeStruct((m, n), out_dtype),
        scratch_shapes=[pltpu.VMEM((k, tn), BF16)],
        compiler_params=_params(2),
    )(*args)


def _attn_kernel(q_ref, k0_ref, k1_ref, k2_ref, v0_ref, v1_ref, v2_ref, u_ref, o_ref, bias_ref, *, hb, dh, qt):
    i = pl.program_id(1)

    @pl.when(i == 0)
    def _():
        shift = CHUNK.bit_length() - 1
        qc = lax.shift_right_logical(lax.broadcasted_iota(I32, (qt, 3 * qt), 0), shift)
        kc = lax.shift_right_logical(lax.broadcasted_iota(I32, (qt, 3 * qt), 1), shift)
        band = (kc >= qc) & (kc <= qc + LEFT_CHUNKS)
        for h in range(hb):
            rows = jnp.broadcast_to(u_ref[h:h + 1, :], (qt, u_ref.shape[1]))
            toep = pltpu.roll(rows, 0, 1, stride=1, stride_axis=0)
            bias_ref[h] = jnp.where(band, toep[:, 0:3 * qt], NEG_BIG)

    pens = [jnp.where(i + j >= 2, 0.0, NEG_BIG).astype(F32) for j in range(2)] + [None]
    scale = dh ** -0.5
    k_refs = (k0_ref, k1_ref, k2_ref)
    v_refs = (v0_ref, v1_ref, v2_ref)
    for h in range(hb):
        sl = slice(h * dh, (h + 1) * dh)
        q = (q_ref[:, sl].astype(F32) * scale).astype(BF16)
        s = []
        for j in range(3):
            sj = lax.dot_general(q, k_refs[j][:, sl], (((1,), (1,)), ((), ())), preferred_element_type=F32)
            sj = sj + bias_ref[h, :, j * qt:(j + 1) * qt]
            if pens[j] is not None:
                sj = sj + pens[j]
            s.append(sj)
        m = jnp.maximum(jnp.maximum(jnp.max(s[0], axis=-1, keepdims=True), jnp.max(s[1], axis=-1, keepdims=True)),
                        jnp.max(s[2], axis=-1, keepdims=True))
        p = [jnp.exp(sj - m) for sj in s]
        l = sum(jnp.sum(pj, axis=-1, keepdims=True) for pj in p)
        o = sum(jnp.dot(p[j].astype(BF16), v_refs[j][:, sl], preferred_element_type=F32) for j in range(3))
        o_ref[:, sl] = (o / l).astype(o_ref.dtype)


def _attn_offset_bias(rel_table, qt):
    max_rel = (rel_table.shape[1] - 1) // 2
    kw = 3 * qt
    period = 4 * qt
    m = jnp.arange(period)
    off = jnp.where(m < kw, m, m - period)
    idx = jnp.clip(2 * qt - off, -max_rel, max_rel) + max_rel
    return jnp.take(rel_table.astype(F32), idx, axis=1)


def _chunked_attention(qkv, rel_table, d):
    t = qkv.shape[0]
    qt = ATTN_Q_TILE
    hb = ATTN_HEADS_PER_STEP
    dh = d // ATTN_HEADS
    assert LEFT_CHUNKS * CHUNK == 2 * qt and t % qt == 0 and ATTN_HEADS % hb == 0
    w = hb * dh
    ng = d // w
    u = _attn_offset_bias(rel_table, qt)

    def kv_spec(j, base):
        return pl.BlockSpec((qt, w), lambda g, i: (jnp.maximum(i - 2 + j, 0), base + g))

    return pl.pallas_call(
        functools.partial(_attn_kernel, hb=hb, dh=dh, qt=qt),
        grid=(ng, t // qt),
        in_specs=[pl.BlockSpec((qt, w), lambda g, i: (i, g))]
                 + [kv_spec(j, ng) for j in range(3)]
                 + [kv_spec(j, 2 * ng) for j in range(3)]
                 + [pl.BlockSpec((hb, u.shape[1]), lambda g, i: (g, 0))],
        out_specs=pl.BlockSpec((qt, w), lambda g, i: (i, g)),
        out_shape=jax.ShapeDtypeStruct((t, d), BF16),
        scratch_shapes=[pltpu.VMEM((hb, qt, 3 * qt), F32)],
        compiler_params=_params(2),
    )(qkv, qkv, qkv, qkv, qkv, qkv, qkv, u)


def _glu_kernel(x_ref, wa_ref, wg_ref, ba_ref, bg_ref, o_ref, wa_bf16_ref, wg_bf16_ref):
    @pl.when(pl.program_id(1) == 0)
    def _():
        _cast_weight_tile(wa_ref, wa_bf16_ref)
        _cast_weight_tile(wg_ref, wg_bf16_ref)

    x = x_ref[...]
    a = jnp.dot(x, wa_bf16_ref[...], preferred_element_type=F32) + ba_ref[...]
    g = jnp.dot(x, wg_bf16_ref[...], preferred_element_type=F32) + bg_ref[...]
    o_ref[...] = (a * jax.nn.sigmoid(g)).astype(o_ref.dtype)


def _glu_matmul(x, w_stack, layer, b, *, tm=512, tn=512):
    m, k = x.shape
    n = w_stack.shape[2] // 2
    assert m % tm == 0 and n % tn == 0
    nb = n // tn
    b2 = b.reshape(1, 2 * n).astype(F32)
    return pl.pallas_call(
        _glu_kernel,
        grid=(nb, m // tm),
        in_specs=[pl.BlockSpec((tm, k), lambda j, i: (i, 0)),
                  pl.BlockSpec((None, k, tn), lambda j, i: (layer, 0, j)),
                  pl.BlockSpec((None, k, tn), lambda j, i: (layer, 0, nb + j)),
                  pl.BlockSpec((1, tn), lambda j, i: (0, j)),
                  pl.BlockSpec((1, tn), lambda j, i: (0, nb + j))],
        out_specs=pl.BlockSpec((tm, tn), lambda j, i: (i, j)),
        out_shape=jax.ShapeDtypeStruct((m, n), BF16),
        scratch_shapes=[pltpu.VMEM((k, tn), BF16), pltpu.VMEM((k, tn), BF16)],
        compiler_params=_params(2),
    )(x, w_stack, w_stack, b2, b2)


CONV_HALO = 32
CONV_ROWS = 64
CONV_COLS = 128


def _conv_kernel(u_ref, halo_ref, w_ref, bdw_ref, g_ref, b_ref, o_ref, ext_ref, acc_ref, *, width, tm):
    i = pl.program_id(0)
    d = u_ref.shape[1]
    halo = halo_ref[...].astype(F32)
    ext_ref[0:CONV_HALO, :] = jnp.where(i > 0, halo, 0.0)
    ext_ref[CONV_HALO:, :] = u_ref[...].astype(F32)
    off = CONV_HALO - (width - 1)

    def col_chunk(c, carry):
        c0 = pl.multiple_of(c * CONV_COLS, CONV_COLS)
        w = w_ref[:, pl.ds(c0, CONV_COLS)]
        for r0 in range(0, tm, CONV_ROWS):
            acc = jnp.zeros((CONV_ROWS, CONV_COLS), F32)
            for b in range(8):
                taps = [k for k in range(b, width, 8)]
                span = CONV_ROWS + (taps[-1] - b)
                v = ext_ref[pl.ds(r0 + off + b, span), pl.ds(c0, CONV_COLS)]
                for k in taps:
                    acc = acc + v[k - b:k - b + CONV_ROWS, :] * w[k:k + 1, :]
            acc_ref[r0:r0 + CONV_ROWS, pl.ds(c0, CONV_COLS)] = acc + bdw_ref[:, pl.ds(c0, CONV_COLS)]
        return carry

    lax.fori_loop(0, d // CONV_COLS, col_chunk, 0)
    (y,) = _ln_pieces([acc_ref[...]], [g_ref[...]], [b_ref[...]])
    o_ref[...] = (y * jax.nn.sigmoid(y)).astype(o_ref.dtype)


def _conv_ln_swish(u, w_dw, b_dw, ln_g, ln_b, *, tm=ROW_TILE):
    t, d = u.shape
    width = w_dw.shape[0]
    assert width - 1 <= CONV_HALO and t % tm == 0 and tm % CONV_HALO == 0 and d % CONV_COLS == 0
    hb = tm // CONV_HALO
    row = lambda a: a.reshape(1, d).astype(F32)
    return pl.pallas_call(
        functools.partial(_conv_kernel, width=width, tm=tm),
        grid=(t // tm,),
        in_specs=[pl.BlockSpec((tm, d), lambda i: (i, 0)),
                  pl.BlockSpec((CONV_HALO, d), lambda i: (jnp.maximum(i * hb - 1, 0), 0)),
                  pl.BlockSpec((width, d), lambda i: (0, 0)),
                  pl.BlockSpec((1, d), lambda i: (0, 0)),
                  pl.BlockSpec((1, d), lambda i: (0, 0)),
                  pl.BlockSpec((1, d), lambda i: (0, 0))],
        out_specs=pl.BlockSpec((tm, d), lambda i: (i, 0)),
        out_shape=jax.ShapeDtypeStruct((t, d), BF16),
        scratch_shapes=[pltpu.VMEM((tm + CONV_HALO, d), F32), pltpu.VMEM((tm, d), F32)],
        compiler_params=_params(1),
    )(u, u, w_dw.astype(F32), row(b_dw), row(ln_g), row(ln_b))


POOL_HALO = 16


def _pool_kernel(x_ref, halo_ref, w_ref, scale_ref, o_ref, ext_ref, *, tm, gw):
    i = pl.program_id(0)
    ext_ref[0:POOL_HALO, :] = jnp.where(i > 0, halo_ref[...], 0.0)
    ext_ref[POOL_HALO:, :] = x_ref[...]
    t_abs = i * tm + lax.broadcasted_iota(I32, (tm, 1), 0)
    for g, win in enumerate(POOL_WINDOWS):
        cols = slice(g * gw, (g + 1) * gw)
        acc = ext_ref[POOL_HALO:POOL_HALO + tm, cols]
        for j in range(1, win):
            acc = acc + ext_ref[POOL_HALO - j:POOL_HALO - j + tm, cols]
        cnt = jnp.minimum(t_abs + 1, win).astype(F32)
        pooled = acc / cnt - ext_ref[POOL_HALO:POOL_HALO + tm, cols]
        y = jnp.dot(pooled.astype(BF16), w_ref[g], preferred_element_type=F32)
        o_ref[:, cols] = (y * scale_ref[:, cols]).astype(o_ref.dtype)


def _multiscale_pool(x, w_pool, scale, *, tm=ROW_TILE):
    t, d = x.shape
    ng, gw, _ = w_pool.shape
    assert ng == len(POOL_WINDOWS) and max(POOL_WINDOWS) <= POOL_HALO and t % tm == 0
    hb = tm // POOL_HALO
    return pl.pallas_call(
        functools.partial(_pool_kernel, tm=tm, gw=gw),
        grid=(t // tm,),
        in_specs=[pl.BlockSpec((tm, d), lambda i: (i, 0)),
                  pl.BlockSpec((POOL_HALO, d), lambda i: (jnp.maximum(i * hb - 1, 0), 0)),
                  pl.BlockSpec((ng, gw, gw), lambda i: (0, 0, 0)),
                  pl.BlockSpec((1, d), lambda i: (0, 0))],
        out_specs=pl.BlockSpec((tm, d), lambda i: (i, 0)),
        out_shape=jax.ShapeDtypeStruct((t, d), BF16),
        scratch_shapes=[pltpu.VMEM((tm + POOL_HALO, d), F32)],
        compiler_params=_params(1),
    )(x, x, w_pool, scale.reshape(1, d).astype(F32))


def _post_kernel(x_ref, f_ref, g0_ref, b0_ref, g1_ref, b1_ref, wq_ref, kv_ref, wo_ref, wr_ref, br_ref,
                 x2_ref, x2p_ref, route_ref, counts_ref, *, n_groups, per_group):
    d = x_ref.shape[1]
    half = d // 2
    n_mem, inner2 = kv_ref.shape
    inner = inner2 // 2
    dh = inner // MEM_HEADS
    y = DEEPNORM_ALPHA * x_ref[...] + f_ref[...].astype(F32)
    (x1,) = _ln_pieces([y], [g0_ref[...]], [b0_ref[...]])
    q = jnp.dot(x1.astype(BF16), wq_ref[...], preferred_element_type=F32) * (dh ** -0.5)
    heads = []
    for h in range(MEM_HEADS):
        qh = q[:, h * dh:(h + 1) * dh].astype(BF16)
        kh = kv_ref[:, h * dh:(h + 1) * dh]
        vh = kv_ref[:, inner + h * dh:inner + (h + 1) * dh]
        s = lax.dot_general(qh, kh, (((1,), (1,)), ((), ())), preferred_element_type=F32)
        p = jnp.exp(s - jnp.max(s, axis=-1, keepdims=True))
        l = jnp.sum(p, axis=-1, keepdims=True)
        heads.append((jnp.dot(p.astype(BF16), vh, preferred_element_type=F32) / l).astype(BF16))
    o = jnp.concatenate(heads, axis=1)
    c = jnp.dot(o, wo_ref[...], preferred_element_type=F32)
    (x2,) = _ln_pieces([DEEPNORM_ALPHA * x1 + c], [g1_ref[...]], [b1_ref[...]])
    x2_ref[...] = x2
    _store_slabs(x2p_ref, 0, _pack_halves(x2[:, :half], x2[:, half:]))

    lg = jnp.dot(x2.astype(BF16), wr_ref[...], preferred_element_type=F32) + br_ref[...]
    lane = lax.broadcasted_iota(I32, lg.shape, 1).astype(F32)
    far = float(LANES)

    def first_lane_of_max(v):
        m = jnp.max(v, axis=-1, keepdims=True)
        return m, jnp.min(jnp.where(v == m, lane, far), axis=-1, keepdims=True)

    gl = jnp.where(lane < n_groups, lg, NEG_BIG)
    gm, gsel = first_lane_of_max(gl)
    gw = 1.0 / jnp.sum(jnp.exp(gl - gm), axis=-1, keepdims=True)
    lo = n_groups + gsel * per_group
    el = jnp.where((lane >= lo) & (lane < lo + per_group), lg, NEG_BIG)
    m1, i1 = first_lane_of_max(el)
    m2, i2 = first_lane_of_max(jnp.where(lane == i1, NEG_BIG, el))
    e2 = jnp.exp(m2 - m1)
    den = 1.0 + e2
    id1 = i1 - n_groups
    id2 = i2 - n_groups

    @pl.when(pl.program_id(0) == 0)
    def _():
        counts_ref[...] = jnp.zeros(counts_ref.shape, counts_ref.dtype)

    hot1 = lane == id1
    hot2 = lane == id2
    hot = jnp.where(hot1 | hot2, 1.0, 0.0)
    tm = lg.shape[0]
    strict_lower = (lax.broadcasted_iota(I32, (tm, tm), 0) > lax.broadcasted_iota(I32, (tm, tm), 1))
    before = jnp.dot(jnp.where(strict_lower, 1.0, 0.0).astype(BF16), hot.astype(BF16),
                     preferred_element_type=F32) + counts_ref[...]
    rank1 = jnp.sum(jnp.where(hot1, before, 0.0), axis=-1, keepdims=True)
    rank2 = jnp.sum(jnp.where(hot2, before, 0.0), axis=-1, keepdims=True)
    counts_ref[...] = counts_ref[...] + jnp.sum(hot, axis=0, keepdims=True)

    route = jnp.where(lane == 0, id1,
                      jnp.where(lane == 1, id2,
                                jnp.where(lane == 2, gw / den,
                                          jnp.where(lane == 3, gw * e2 / den,
                                                    jnp.where(lane == 4, rank1,
                                                              jnp.where(lane == 5, rank2, 0.0))))))
    route_ref[...] = route


def _post_block(x, f, ln_g, ln_b, w_q, kv, w_o, w_r, b_r, n_groups, per_group, *, tm=ROW_TILE):
    t, d = x.shape
    inner = w_q.shape[1]
    n_mem = kv.shape[0]
    n_slabs = d // 2 // LANES
    row = lambda a: a.reshape(1, d).astype(F32)
    full = lambda shape: pl.BlockSpec(shape, lambda i: (0,) * len(shape))
    tile = lambda w: pl.BlockSpec((tm, w), lambda i: (i, 0))
    return pl.pallas_call(
        functools.partial(_post_kernel, n_groups=n_groups, per_group=per_group),
        grid=(t // tm,),
        in_specs=[tile(d), tile(d), full((1, d)), full((1, d)), full((1, d)), full((1, d)),
                  full((d, inner)), full((n_mem, 2 * inner)), full((inner, d)), full((d, LANES)), full((1, LANES))],
        out_specs=[tile(d), pl.BlockSpec((tm * n_slabs, LANES), lambda i: (i, 0)), tile(LANES), full((1, LANES))],
        out_shape=[jax.ShapeDtypeStruct((t, d), F32), jax.ShapeDtypeStruct((t * n_slabs, LANES), U32),
                   jax.ShapeDtypeStruct((t, LANES), F32), jax.ShapeDtypeStruct((1, LANES), F32)],
        compiler_params=_params(1),
    )(x, f, row(ln_g[0]), row(ln_b[0]), row(ln_g[1]), row(ln_b[1]), w_q, kv, w_o, w_r, b_r)


MOE_CAST_ROWS = 128
MOE_WEIGHT_COPIES = 8


def _moe_kernel(be_ref, first_ref, next_ref, src_ref, nu_ref, xg_hbm, wg_hbm, wu_hbm, wd_hbm, yb_ref,
                xbuf0, xbuf1, stage_g, stage_u, stage_d, wgu_ref, wd_ref, part_ref, sem, wsem, *, blk, layer):
    i = pl.program_id(0)
    n_used = nu_ref[0]
    bufs = (xbuf0, xbuf1)
    n_slabs = yb_ref.shape[0] // blk
    half = n_slabs * LANES
    de = wd_ref.shape[0]
    d = wd_ref.shape[1]

    def weight_copies(e):
        copies = []
        for k, (w_hbm, stage) in enumerate(((wg_hbm, stage_g), (wu_hbm, stage_u), (wd_hbm, stage_d))):
            rows = stage.shape[0] // MOE_WEIGHT_COPIES
            for c in range(MOE_WEIGHT_COPIES):
                piece = pl.ds(c * rows, rows)
                copies.append(pltpu.make_async_copy(w_hbm.at[layer, e, piece, :], stage.at[piece, :], wsem.at[k]))
        return copies

    @pl.when(i == 0)
    def _():
        for c in weight_copies(be_ref[0]):
            c.start()

    @pl.when((i < n_used) & (first_ref[i] == 1))
    def _():
        for c in weight_copies(be_ref[i]):
            c.wait()

        def cast_up(r, carry):
            rows = pl.ds(pl.multiple_of(r * MOE_CAST_ROWS, MOE_CAST_ROWS), MOE_CAST_ROWS)
            wgu_ref[rows, 0:de] = stage_g[rows, :].astype(BF16)
            wgu_ref[rows, de:2 * de] = stage_u[rows, :].astype(BF16)
            return carry

        def cast_down(r, carry):
            rows = pl.ds(pl.multiple_of(r * MOE_CAST_ROWS, MOE_CAST_ROWS), MOE_CAST_ROWS)
            for c0 in range(0, d, 1024):
                wd_ref[rows, c0:c0 + 1024] = stage_d[rows, c0:c0 + 1024].astype(BF16)
            return carry

        lax.fori_loop(0, d // MOE_CAST_ROWS, cast_up, 0)
        lax.fori_loop(0, de // MOE_CAST_ROWS, cast_down, 0)

        @pl.when(next_ref[i] >= 0)
        def _():
            for c in weight_copies(next_ref[i]):
                c.start()

    def row_copy(j, r, s):
        src = pl.multiple_of(src_ref[j * blk + r], n_slabs)
        return pltpu.make_async_copy(xg_hbm.at[pl.ds(src, n_slabs), :],
                                     bufs[s].at[pl.ds(r * GATHER_PITCH, n_slabs), :], sem.at[s])

    def wait_block(s):
        pltpu.make_async_copy(xg_hbm.at[pl.ds(0, blk * n_slabs), :], bufs[s].at[pl.ds(0, blk * n_slabs), :],
                              sem.at[s]).wait()

    @pl.when(i == 0)
    def _():
        def body(r, carry):
            src = pl.multiple_of(src_ref[r], n_slabs)
            dst = pl.multiple_of(r * GATHER_PITCH, 8)
            pltpu.make_async_copy(xg_hbm.at[pl.ds(src, n_slabs), :], xbuf0.at[pl.ds(dst, n_slabs), :],
                                  sem.at[0]).start()
            return carry
        lax.fori_loop(0, blk, body, 0)

    def expert_block(s):
        wait_block(s)
        los, his = [], []
        for q in range(n_slabs):
            lo_q, hi_q = _unpack_halves(_load_slab_column(bufs[s], q, blk))
            los.append(lo_q.astype(BF16))
            his.append(hi_q.astype(BF16))
        lo = jnp.concatenate(los, axis=1)
        hi = jnp.concatenate(his, axis=1)
        for r in range(blk):
            row_copy(i + 1, r, 1 - s).start()
        wu = 2 * de // MOE_UP_PIECES
        for c in range(MOE_UP_PIECES):
            cols = slice(c * wu, (c + 1) * wu)
            part_ref[:, cols] = jnp.dot(lo, wgu_ref[0:half, cols], preferred_element_type=F32)
        au = []
        for c in range(MOE_UP_PIECES):
            cols = slice(c * wu, (c + 1) * wu)
            au.append(part_ref[:, cols] + jnp.dot(hi, wgu_ref[half:, cols], preferred_element_type=F32))
        au = jnp.concatenate(au, axis=1)
        a = au[:, 0:de]
        u = au[:, de:]
        hmid = (a * jax.nn.sigmoid(a) * u).astype(BF16)
        wdn = half // MOE_DOWN_PIECES
        for c in range(MOE_DOWN_PIECES):
            y_lo = jnp.dot(hmid, wd_ref[:, c * wdn:(c + 1) * wdn], preferred_element_type=F32)
            y_hi = jnp.dot(hmid, wd_ref[:, half + c * wdn:half + (c + 1) * wdn], preferred_element_type=F32)
            _store_slabs(yb_ref, c * (wdn // LANES), _pack_halves(y_lo, y_hi))

    for s in range(2):
        pl.when((i < n_used) & (i % 2 == s))(functools.partial(expert_block, s))
        pl.when((i == n_used) & (i % 2 == s))(functools.partial(wait_block, s))

    @pl.when(i >= n_used)
    def _():
        yb_ref[...] = jnp.zeros(yb_ref.shape, yb_ref.dtype)


def _moe_experts(xg, plan, w_gate, w_up, w_down, layer, *, blk=MOE_BLOCK):
    n_slots = plan.slot_src.shape[0]
    nb = n_slots // blk
    de, d = w_down.shape[2:]
    n_slabs = d // 2 // LANES
    assert n_slabs <= GATHER_PITCH and GATHER_PITCH % 8 == 0
    assert d % MOE_CAST_ROWS == 0 and de % MOE_CAST_ROWS == 0 and d % 1024 == 0
    buf = pltpu.VMEM((blk * GATHER_PITCH, LANES), U32)
    any_spec = pl.BlockSpec(memory_space=pl.ANY)
    grid_spec = pltpu.PrefetchScalarGridSpec(
        num_scalar_prefetch=5,
        grid=(nb,),
        in_specs=[any_spec, any_spec, any_spec, any_spec],
        out_specs=pl.BlockSpec((blk * n_slabs, LANES), lambda i, *_: (i, 0)),
        scratch_shapes=[buf, buf,
                        pltpu.VMEM((d, de), F32), pltpu.VMEM((d, de), F32), pltpu.VMEM((de, d), F32),
                        pltpu.VMEM((d, 2 * de), BF16), pltpu.VMEM((de, d), BF16), pltpu.VMEM((blk, 2 * de), F32),
                        pltpu.SemaphoreType.DMA((2,)), pltpu.SemaphoreType.DMA((3,))],
    )
    return pl.pallas_call(
        functools.partial(_moe_kernel, blk=blk, layer=layer),
        grid_spec=grid_spec,
        out_shape=jax.ShapeDtypeStruct((n_slots * n_slabs, LANES), U32),
        compiler_params=_params(1),
    )(plan.blk_expert, plan.seg_first, plan.seg_next, plan.slot_src, plan.n_used, xg, w_gate, w_up, w_down)


def _combine_kernel(src_ref, x2_ref, route_ref, g_ref, b_ref, yb_hbm, x3_ref, x3b_ref, ybuf0, ybuf1, sem, *, tm):
    i = pl.program_id(0)
    n = pl.num_programs(0)
    bufs = (ybuf0, ybuf1)
    d = x2_ref.shape[1]
    half = d // 2
    n_slabs = half // LANES

    def wait_tile(s):
        for k in range(TOP_K):
            pltpu.make_async_copy(yb_hbm.at[pl.ds(0, tm * n_slabs), :], bufs[s].at[k, pl.ds(0, tm * n_slabs), :],
                                  sem.at[s]).wait()

    @pl.when(i == 0)
    def _():
        def body(r, carry):
            dst = pl.multiple_of(r * GATHER_PITCH, 8)
            for k in range(TOP_K):
                src = pl.multiple_of(src_ref[TOP_K * r + k], n_slabs)
                pltpu.make_async_copy(yb_hbm.at[pl.ds(src, n_slabs), :], ybuf0.at[k, pl.ds(dst, n_slabs), :],
                                      sem.at[0]).start()
            return carry

        lax.fori_loop(0, tm, body, 0)

    def tile(s):
        wait_tile(s)
        gates = [route_ref[:, TOP_K + k:TOP_K + k + 1] for k in range(TOP_K)]
        y_lo, y_hi = [], []
        for q in range(n_slabs):
            lo = hi = None
            for k in range(TOP_K):
                lo_k, hi_k = _unpack_halves(bufs[s][k, pl.ds(q, tm, stride=GATHER_PITCH), :])
                lo = lo_k * gates[k] if lo is None else lo + lo_k * gates[k]
                hi = hi_k * gates[k] if hi is None else hi + hi_k * gates[k]
            y_lo.append(DEEPNORM_ALPHA * x2_ref[:, q * LANES:(q + 1) * LANES] + lo)
            y_hi.append(DEEPNORM_ALPHA * x2_ref[:, half + q * LANES:half + (q + 1) * LANES] + hi)
        base = (i + 1) * tm * TOP_K
        for r in range(tm):
            for k in range(TOP_K):
                src = pl.multiple_of(src_ref[base + TOP_K * r + k], n_slabs)
                pltpu.make_async_copy(yb_hbm.at[pl.ds(src, n_slabs), :],
                                      bufs[1 - s].at[k, pl.ds(r * GATHER_PITCH, n_slabs), :], sem.at[1 - s]).start()
        cols = [slice(c * LANES, (c + 1) * LANES) for c in range(2 * n_slabs)]
        out = _ln_pieces(y_lo + y_hi, [g_ref[:, c] for c in cols], [b_ref[:, c] for c in cols])
        for c, o in zip(cols, out):
            x3_ref[:, c] = o
            x3b_ref[:, c] = o.astype(BF16)

    for s in range(2):
        pl.when(i % 2 == s)(functools.partial(tile, s))
        pl.when((i == n - 1) & (i % 2 == s))(functools.partial(wait_tile, 1 - s))


def _moe_combine(x2, route, yb, slot_of, ln_g, ln_b, *, tm=ROW_TILE):
    t, d = x2.shape
    half = d // 2
    row = lambda a: a.reshape(1, d).astype(F32)
    grid_spec = pltpu.PrefetchScalarGridSpec(
        num_scalar_prefetch=1,
        grid=(t // tm,),
        in_specs=[pl.BlockSpec((tm, d), lambda i, dst: (i, 0)),
                  pl.BlockSpec((tm, LANES), lambda i, dst: (i, 0)),
                  pl.BlockSpec((1, d), lambda i, dst: (0, 0)),
                  pl.BlockSpec((1, d), lambda i, dst: (0, 0)),
                  pl.BlockSpec(memory_space=pl.ANY)],
        out_specs=[pl.BlockSpec((tm, d), lambda i, dst: (i, 0)), pl.BlockSpec((tm, d), lambda i, dst: (i, 0))],
        scratch_shapes=[pltpu.VMEM((TOP_K, tm * GATHER_PITCH, LANES), U32),
                        pltpu.VMEM((TOP_K, tm * GATHER_PITCH, LANES), U32), pltpu.SemaphoreType.DMA((2,))],
    )
    return pl.pallas_call(
        functools.partial(_combine_kernel, tm=tm),
        grid_spec=grid_spec,
        out_shape=[jax.ShapeDtypeStruct((t, d), F32), jax.ShapeDtypeStruct((t, d), BF16)],
        compiler_params=_params(1),
    )(jnp.concatenate([slot_of, jnp.zeros((tm * TOP_K,), I32)]), x2, route, row(ln_g), row(ln_b), yb)


class DispatchPlan(NamedTuple):
    slot_of: jax.Array
    slot_src: jax.Array
    blk_expert: jax.Array
    seg_first: jax.Array
    seg_next: jax.Array
    n_used: jax.Array


def _dispatch_plan(route, counts, n_experts, blk, n_slabs):
    t = route.shape[0]
    ids = route[:, 0:TOP_K].astype(I32)
    rank = route[:, 2 * TOP_K:3 * TOP_K].astype(I32)
    counts = counts[0, :n_experts].astype(I32)
    padded = (counts + blk - 1) // blk * blk
    ends = jnp.cumsum(padded)
    starts = ends - padded
    expert_iota = jnp.arange(n_experts, dtype=I32)
    seg_start = jnp.sum(jnp.where(ids[:, :, None] == expert_iota, starts, 0), axis=-1)
    nb = (t * TOP_K) // blk + n_experts
    dest = jnp.clip(seg_start + rank, 0, nb * blk - 1).reshape(-1)
    flat_src = jnp.arange(t * TOP_K, dtype=I32) // TOP_K * n_slabs
    slot_src = jnp.zeros((nb * blk,), I32).at[dest].set(flat_src, unique_indices=True)
    blk_start = jnp.arange(nb, dtype=I32) * blk
    blk_expert = jnp.minimum(jnp.sum((ends[None, :] <= blk_start[:, None]).astype(I32), axis=1), n_experts - 1)
    n_used = ends[-1] // blk
    blk_iota = jnp.arange(nb, dtype=I32)
    used = blk_iota < n_used
    prev_expert = jnp.concatenate([jnp.full((1,), -1, I32), blk_expert[:-1]])
    seg_first = (used & (blk_expert != prev_expert)).astype(I32)
    later_segment = (blk_iota[None, :] > blk_iota[:, None]) & used[None, :] & (blk_expert[None, :] != blk_expert[:, None])
    seg_next = jnp.where(jnp.any(later_segment, axis=1), blk_expert[jnp.argmax(later_segment, axis=1)], -1).astype(I32)
    return DispatchPlan(dest * n_slabs, slot_src, blk_expert.astype(I32), seg_first, seg_next,
                        n_used.astype(I32).reshape(1))


def kernel(x, mem, attn_w_qkv, attn_w_o, attn_rel_bias, conv_w_in, conv_b_in, conv_w_dw, conv_b_dw, conv_ln_g,
           conv_ln_b, conv_w_out, conv_b_out, pool_w, pool_scale, mem_w_q, mem_w_kv, mem_w_o, moe_w_group,
           moe_b_group, moe_w_router, moe_b_router, moe_w_gate, moe_w_up, moe_w_down, ln_g, ln_b):
    batch, seq, d = x.shape
    assert batch == 1 and ln_g.shape[0] == DEPTH
    n_groups = moe_w_group.shape[2]
    per_group = moe_w_router.shape[3]
    n_experts = n_groups * per_group
    assert n_groups + n_experts <= LANES

    xf = x.reshape(seq, d)
    xb = xf.astype(BF16)
    memb = mem.reshape(mem.shape[1], d).astype(BF16)

    for i in range(DEPTH):
        kind = i % N_MIXERS
        j = i // N_MIXERS
        if kind == 0:
            qkv = _matmul(xb, attn_w_qkv, j)
            o = _chunked_attention(qkv, attn_rel_bias[j], d)
            f = _matmul(o, attn_w_o, j)
        elif kind == 1:
            u = _glu_matmul(xb, conv_w_in, j, conv_b_in[j])
            v = _conv_ln_swish(u, conv_w_dw[j], conv_b_dw[j], conv_ln_g[j], conv_ln_b[j])
            f = _matmul(v, conv_w_out, j, conv_b_out[j])
        else:
            f = _multiscale_pool(xf, pool_w[j].astype(BF16), pool_scale[j])

        kv = _matmul(memb, mem_w_kv, i, tm=memb.shape[0])
        w_r = jnp.concatenate([moe_w_group[i], moe_w_router[i].transpose(1, 0, 2).reshape(d, n_experts)], axis=1)
        w_r = jnp.pad(w_r, ((0, 0), (0, LANES - w_r.shape[1]))).astype(BF16)
        b_r = jnp.concatenate([moe_b_group[i], moe_b_router[i].reshape(n_experts)])
        b_r = jnp.pad(b_r, (0, LANES - b_r.shape[0])).reshape(1, LANES).astype(F32)
        x2, x2p, route, counts = _post_block(xf, f, ln_g[i], ln_b[i], mem_w_q[i].astype(BF16), kv,
                                             mem_w_o[i].astype(BF16), w_r, b_r, n_groups, per_group)

        plan = _dispatch_plan(route, counts, n_experts, MOE_BLOCK, d // 2 // LANES)
        yb = _moe_experts(x2p, plan, moe_w_gate, moe_w_up, moe_w_down, i)
        xf, xb = _moe_combine(x2, route, yb, plan.slot_of, ln_g[i, 2], ln_b[i, 2])

    return xf.reshape(batch, seq, d)
```

```python
import functools
from typing import NamedTuple

import jax
import jax.numpy as jnp
from jax import lax
from jax.experimental import pallas as pl
from jax.experimental.pallas import tpu as pltpu

F32 = jnp.float32
BF16 = jnp.bfloat16
U32 = jnp.uint32
I32 = jnp.int32

DEPTH = 4
N_MIXERS = 3
ATTN_HEADS = 32
CHUNK = 64
LEFT_CHUNKS = 8
POOL_WINDOWS = (2, 4, 8, 16)
MEM_HEADS = 4
TOP_K = 2
LN_EPS = 1e-5
DEEPNORM_ALPHA = (2 * DEPTH) ** 0.25
NEG_BIG = -1e30
LOG2_E = 1.4426950408889634

V7X_VMEM_BYTES = 64 * 1024 * 1024
VMEM_LIMIT = V7X_VMEM_BYTES - 8 * 1024 * 1024
LANES = 128

ATTN_Q_TILE = 4 * CHUNK
ATTN_HEADS_PER_STEP = 8
MOE_BLOCK = 256
MOE_UP_PIECES = 3
MOE_DOWN_PIECES = 8
ROW_TILE = 256


def _params(n_grid_dims):
    return pltpu.CompilerParams(dimension_semantics=("arbitrary",) * n_grid_dims,
                                vmem_limit_bytes=VMEM_LIMIT)


def _ln_pieces(pieces, g_pieces, b_pieces):
    d = sum(p.shape[-1] for p in pieces)
    mu = sum(jnp.sum(p, axis=-1, keepdims=True) for p in pieces) * (1.0 / d)
    cen = [p - mu for p in pieces]
    var = sum(jnp.sum(c * c, axis=-1, keepdims=True) for c in cen) * (1.0 / d)
    inv = lax.rsqrt(var + LN_EPS)
    return [c * inv * g + b for c, g, b in zip(cen, g_pieces, b_pieces)]


def _pack_halves(lo, hi):
    half_ulp = jnp.uint32(0x8000)
    lo_bits = (lax.bitcast_convert_type(lo, U32) + half_ulp) >> 16
    hi_bits = (lax.bitcast_convert_type(hi, U32) + half_ulp) & jnp.uint32(0xFFFF0000)
    return lo_bits | hi_bits


def _unpack_halves(w):
    lo = lax.bitcast_convert_type(w << 16, F32)
    hi = lax.bitcast_convert_type(w & jnp.uint32(0xFFFF0000), F32)
    return lo, hi


GATHER_PITCH = 24


def _store_slabs(ref, first_slab, packed):
    rows = packed.shape[0]
    n_slabs = ref.shape[0] // rows
    for q in range(packed.shape[1] // LANES):
        ref[pl.ds(first_slab + q, rows, stride=n_slabs), :] = packed[:, q * LANES:(q + 1) * LANES]


def _load_slab_column(ref_2d, s, rows):
    return ref_2d[pl.ds(s, rows, stride=GATHER_PITCH), :]


MM_CAST_ROWS = 512


def _cast_weight_tile(w_ref, w_bf16_ref):
    k = w_ref.shape[0]
    step = min(MM_CAST_ROWS, k)
    for r in range(0, k, step):
        w_bf16_ref[r:r + step, :] = w_ref[r:r + step, :].astype(BF16)


def _mm_kernel(x_ref, w_ref, *rest, has_bias):
    if has_bias:
        b_ref, o_ref, w_bf16_ref = rest
    else:
        o_ref, w_bf16_ref = rest

    @pl.when(pl.program_id(1) == 0)
    def _():
        _cast_weight_tile(w_ref, w_bf16_ref)

    acc = jnp.dot(x_ref[...], w_bf16_ref[...], preferred_element_type=F32)
    if has_bias:
        acc = acc + b_ref[...]
    o_ref[...] = acc.astype(o_ref.dtype)


def _matmul(x, w_stack, layer, bias=None, *, out_dtype=BF16, tm=512, tn=1024):
    m, k = x.shape
    n = w_stack.shape[2]
    tm = min(tm, m)
    tn = min(tn, n)
    assert m % tm == 0 and n % tn == 0 and w_stack.shape[1] == k
    in_specs = [pl.BlockSpec((tm, k), lambda j, i: (i, 0)),
                pl.BlockSpec((None, k, tn), lambda j, i: (layer, 0, j))]
    args = [x, w_stack]
    if bias is not None:
        in_specs.append(pl.BlockSpec((1, tn), lambda j, i: (0, j)))
        args.append(bias.reshape(1, n).astype(F32))
    return pl.pallas_call(
        functools.partial(_mm_kernel, has_bias=bias is not None),
        grid=(n // tn, m // tm),
        in_specs=in_specs,
        out_specs=pl.BlockSpec((tm, tn), lambda j, i: (i, j)),
        out_shape=jax.ShapeDtypeStruct((m, n), out_dtype),
        scratch_shapes=[pltpu.VMEM((k, tn), BF16)],
        compiler_params=_params(2),
    )(*args)


def _attn_kernel(q_ref, k0_ref, k1_ref, k2_ref, v0_ref, v1_ref, v2_ref, u_ref, o_ref, bias_ref, *, hb, dh, qt):
    i = pl.program_id(1)

    @pl.when(i == 0)
    def _():
        shift = CHUNK.bit_length() - 1
        qc = lax.shift_right_logical(lax.broadcasted_iota(I32, (qt, 3 * qt), 0), shift)
        kc = lax.shift_right_logical(lax.broadcasted_iota(I32, (qt, 3 * qt), 1), shift)
        band = (kc >= qc) & (kc <= qc + LEFT_CHUNKS)
        for h in range(hb):
            rows = jnp.broadcast_to(u_ref[h:h + 1, :], (qt, u_ref.shape[1]))
            toep = pltpu.roll(rows, 0, 1, stride=1, stride_axis=0)
            bias_ref[h] = jnp.where(band, toep[:, 0:3 * qt] * LOG2_E, NEG_BIG)

    scale = dh ** -0.5 * LOG2_E
    k_refs = (k0_ref, k1_ref, k2_ref)
    v_refs = (v0_ref, v1_ref, v2_ref)

    def heads(at_sequence_start):
        pens = [jnp.where(i + j >= 2, 0.0, NEG_BIG).astype(F32) if at_sequence_start else None for j in range(2)]
        pens.append(None)
        for h in range(hb):
            sl = slice(h * dh, (h + 1) * dh)
            q = (q_ref[:, sl].astype(F32) * scale).astype(BF16)
            s = []
            for j in range(3):
                sj = lax.dot_general(q, k_refs[j][:, sl], (((1,), (1,)), ((), ())), preferred_element_type=F32)
                sj = sj + bias_ref[h, :, j * qt:(j + 1) * qt]
                if pens[j] is not None:
                    sj = sj + pens[j]
                s.append(sj)
            m = jnp.maximum(jnp.maximum(jnp.max(s[0], axis=-1, keepdims=True),
                                        jnp.max(s[1], axis=-1, keepdims=True)),
                            jnp.max(s[2], axis=-1, keepdims=True))
            p = [jnp.exp2(sj - m) for sj in s]
            l = sum(jnp.sum(pj, axis=-1, keepdims=True) for pj in p)
            o = sum(jnp.dot(p[j].astype(BF16), v_refs[j][:, sl], preferred_element_type=F32) for j in range(3))
            o_ref[:, sl] = (o / l).astype(o_ref.dtype)

    pl.when(i < 2)(functools.partial(heads, True))
    pl.when(i >= 2)(functools.partial(heads, False))


def _attn_offset_bias(rel_table, qt):
    max_rel = (rel_table.shape[1] - 1) // 2
    kw = 3 * qt
    period = 4 * qt
    m = jnp.arange(period)
    off = jnp.where(m < kw, m, m - period)
    idx = jnp.clip(2 * qt - off, -max_rel, max_rel) + max_rel
    return jnp.take(rel_table.astype(F32), idx, axis=1)


def _chunked_attention(qkv, rel_table, d):
    t = qkv.shape[0]
    qt = ATTN_Q_TILE
    hb = ATTN_HEADS_PER_STEP
    dh = d // ATTN_HEADS
    assert LEFT_CHUNKS * CHUNK == 2 * qt and t % qt == 0 and ATTN_HEADS % hb == 0
    w = hb * dh
    ng = d // w
    u = _attn_offset_bias(rel_table, qt)

    def kv_spec(j, base):
        return pl.BlockSpec((qt, w), lambda g, i: (jnp.maximum(i - 2 + j, 0), base + g))

    return pl.pallas_call(
        functools.partial(_attn_kernel, hb=hb, dh=dh, qt=qt),
        grid=(ng, t // qt),
        in_specs=[pl.BlockSpec((qt, w), lambda g, i: (i, g))]
                 + [kv_spec(j, ng) for j in range(3)]
                 + [kv_spec(j, 2 * ng) for j in range(3)]
                 + [pl.BlockSpec((hb, u.shape[1]), lambda g, i: (g, 0))],
        out_specs=pl.BlockSpec((qt, w), lambda g, i: (i, g)),
        out_shape=jax.ShapeDtypeStruct((t, d), BF16),
        scratch_shapes=[pltpu.VMEM((hb, qt, 3 * qt), F32)],
        compiler_params=_params(2),
    )(qkv, qkv, qkv, qkv, qkv, qkv, qkv, u)


def _glu_kernel(x_ref, wa_ref, wg_ref, ba_ref, bg_ref, o_ref, wa_bf16_ref, wg_bf16_ref):
    @pl.when(pl.program_id(1) == 0)
    def _():
        _cast_weight_tile(wa_ref, wa_bf16_ref)
        _cast_weight_tile(wg_ref, wg_bf16_ref)

    x = x_ref[...]
    a = jnp.dot(x, wa_bf16_ref[...], preferred_element_type=F32) + ba_ref[...]
    g = jnp.dot(x, wg_bf16_ref[...], preferred_element_type=F32) + bg_ref[...]
    o_ref[...] = (a * jax.nn.sigmoid(g)).astype(o_ref.dtype)


def _glu_matmul(x, w_stack, layer, b, *, tm=512, tn=512):
    m, k = x.shape
    n = w_stack.shape[2] // 2
    assert m % tm == 0 and n % tn == 0
    nb = n // tn
    b2 = b.reshape(1, 2 * n).astype(F32)
    return pl.pallas_call(
        _glu_kernel,
        grid=(nb, m // tm),
        in_specs=[pl.BlockSpec((tm, k), lambda j, i: (i, 0)),
                  pl.BlockSpec((None, k, tn), lambda j, i: (layer, 0, j)),
                  pl.BlockSpec((None, k, tn), lambda j, i: (layer, 0, nb + j)),
                  pl.BlockSpec((1, tn), lambda j, i: (0, j)),
                  pl.BlockSpec((1, tn), lambda j, i: (0, nb + j))],
        out_specs=pl.BlockSpec((tm, tn), lambda j, i: (i, j)),
        out_shape=jax.ShapeDtypeStruct((m, n), BF16),
        scratch_shapes=[pltpu.VMEM((k, tn), BF16), pltpu.VMEM((k, tn), BF16)],
        compiler_params=_params(2),
    )(x, w_stack, w_stack, b2, b2)


CONV_HALO = 32
CONV_ROWS = 64
CONV_COLS = 128


def _conv_kernel(u_ref, halo_ref, w_ref, bdw_ref, g_ref, b_ref, o_ref, ext_ref, acc_ref, shift_ref, *, width, tm):
    i = pl.program_id(0)
    d = u_ref.shape[1]
    halo = halo_ref[...].astype(F32)
    ext_ref[0:CONV_HALO, :] = jnp.where(i > 0, halo, 0.0)
    ext_ref[CONV_HALO:, :] = u_ref[...].astype(F32)
    off = CONV_HALO - (width - 1)

    def col_chunk(c, carry):
        c0 = pl.multiple_of(c * CONV_COLS, CONV_COLS)
        w = w_ref[:, pl.ds(c0, CONV_COLS)]
        for b in range(8):
            rows = tm + (width - 1 - b) // 8 * 8
            shift_ref[b, 0:rows, :] = ext_ref[pl.ds(off + b, rows), pl.ds(c0, CONV_COLS)]
        for r0 in range(0, tm, CONV_ROWS):
            acc = jnp.zeros((CONV_ROWS, CONV_COLS), F32)
            for k in range(width):
                a8 = k - k % 8
                acc = acc + shift_ref[k % 8, r0 + a8:r0 + a8 + CONV_ROWS, :] * w[k:k + 1, :]
            acc_ref[r0:r0 + CONV_ROWS, pl.ds(c0, CONV_COLS)] = acc + bdw_ref[:, pl.ds(c0, CONV_COLS)]
        return carry

    lax.fori_loop(0, d // CONV_COLS, col_chunk, 0)
    (y,) = _ln_pieces([acc_ref[...]], [g_ref[...]], [b_ref[...]])
    o_ref[...] = (y * jax.nn.sigmoid(y)).astype(o_ref.dtype)


def _conv_ln_swish(u, w_dw, b_dw, ln_g, ln_b, *, tm=ROW_TILE):
    t, d = u.shape
    width = w_dw.shape[0]
    assert width - 1 <= CONV_HALO and t % tm == 0 and tm % CONV_HALO == 0 and d % CONV_COLS == 0
    hb = tm // CONV_HALO
    row = lambda a: a.reshape(1, d).astype(F32)
    return pl.pallas_call(
        functools.partial(_conv_kernel, width=width, tm=tm),
        grid=(t // tm,),
        in_specs=[pl.BlockSpec((tm, d), lambda i: (i, 0)),
                  pl.BlockSpec((CONV_HALO, d), lambda i: (jnp.maximum(i * hb - 1, 0), 0)),
                  pl.BlockSpec((width, d), lambda i: (0, 0)),
                  pl.BlockSpec((1, d), lambda i: (0, 0)),
                  pl.BlockSpec((1, d), lambda i: (0, 0)),
                  pl.BlockSpec((1, d), lambda i: (0, 0))],
        out_specs=pl.BlockSpec((tm, d), lambda i: (i, 0)),
        out_shape=jax.ShapeDtypeStruct((t, d), BF16),
        scratch_shapes=[pltpu.VMEM((tm + CONV_HALO, d), F32), pltpu.VMEM((tm, d), F32),
                        pltpu.VMEM((8, tm + (width - 1) // 8 * 8, CONV_COLS), F32)],
        compiler_params=_params(1),
    )(u, u, w_dw.astype(F32), row(b_dw), row(ln_g), row(ln_b))


POOL_HALO = 16


def _pool_kernel(x_ref, halo_ref, w_ref, scale_ref, o_ref, ext_ref, *, tm, gw):
    i = pl.program_id(0)
    ext_ref[0:POOL_HALO, :] = jnp.where(i > 0, halo_ref[...], 0.0)
    ext_ref[POOL_HALO:, :] = x_ref[...]
    t_abs = i * tm + lax.broadcasted_iota(I32, (tm, 1), 0)
    for g, win in enumerate(POOL_WINDOWS):
        cols = slice(g * gw, (g + 1) * gw)
        acc = ext_ref[POOL_HALO:POOL_HALO + tm, cols]
        for j in range(1, win):
            acc = acc + ext_ref[POOL_HALO - j:POOL_HALO - j + tm, cols]
        cnt = jnp.minimum(t_abs + 1, win).astype(F32)
        pooled = acc / cnt - ext_ref[POOL_HALO:POOL_HALO + tm, cols]
        y = jnp.dot(pooled.astype(BF16), w_ref[g], preferred_element_type=F32)
        o_ref[:, cols] = (y * scale_ref[:, cols]).astype(o_ref.dtype)


def _multiscale_pool(x, w_pool, scale, *, tm=ROW_TILE):
    t, d = x.shape
    ng, gw, _ = w_pool.shape
    assert ng == len(POOL_WINDOWS) and max(POOL_WINDOWS) <= POOL_HALO and t % tm == 0
    hb = tm // POOL_HALO
    return pl.pallas_call(
        functools.partial(_pool_kernel, tm=tm, gw=gw),
        grid=(t // tm,),
        in_specs=[pl.BlockSpec((tm, d), lambda i: (i, 0)),
                  pl.BlockSpec((POOL_HALO, d), lambda i: (jnp.maximum(i * hb - 1, 0), 0)),
                  pl.BlockSpec((ng, gw, gw), lambda i: (0, 0, 0)),
                  pl.BlockSpec((1, d), lambda i: (0, 0))],
        out_specs=pl.BlockSpec((tm, d), lambda i: (i, 0)),
        out_shape=jax.ShapeDtypeStruct((t, d), BF16),
        scratch_shapes=[pltpu.VMEM((tm + POOL_HALO, d), F32)],
        compiler_params=_params(1),
    )(x, x, w_pool, scale.reshape(1, d).astype(F32))


def _post_kernel(x_ref, f_ref, g0_ref, b0_ref, g1_ref, b1_ref, wq_ref, kv_ref, wo_ref, wr_ref, br_ref,
                 x2_ref, x2p_ref, route_ref, counts_ref, *, n_groups, per_group):
    d = x_ref.shape[1]
    half = d // 2
    n_mem, inner2 = kv_ref.shape
    inner = inner2 // 2
    dh = inner // MEM_HEADS
    y = DEEPNORM_ALPHA * x_ref[...] + f_ref[...].astype(F32)
    (x1,) = _ln_pieces([y], [g0_ref[...]], [b0_ref[...]])
    q = jnp.dot(x1.astype(BF16), wq_ref[...], preferred_element_type=F32) * (dh ** -0.5)
    heads = []
    for h in range(MEM_HEADS):
        qh = q[:, h * dh:(h + 1) * dh].astype(BF16)
        kh = kv_ref[:, h * dh:(h + 1) * dh]
        vh = kv_ref[:, inner + h * dh:inner + (h + 1) * dh]
        s = lax.dot_general(qh, kh, (((1,), (1,)), ((), ())), preferred_element_type=F32)
        p = jnp.exp(s - jnp.max(s, axis=-1, keepdims=True))
        l = jnp.sum(p, axis=-1, keepdims=True)
        heads.append((jnp.dot(p.astype(BF16), vh, preferred_element_type=F32) / l).astype(BF16))
    o = jnp.concatenate(heads, axis=1)
    c = jnp.dot(o, wo_ref[...], preferred_element_type=F32)
    (x2,) = _ln_pieces([DEEPNORM_ALPHA * x1 + c], [g1_ref[...]], [b1_ref[...]])
    x2_ref[...] = x2
    _store_slabs(x2p_ref, 0, _pack_halves(x2[:, :half], x2[:, half:]))

    lg = jnp.dot(x2.astype(BF16), wr_ref[...], preferred_element_type=F32) + br_ref[...]
    lane = lax.broadcasted_iota(I32, lg.shape, 1).astype(F32)
    far = float(LANES)

    def first_lane_of_max(v):
        m = jnp.max(v, axis=-1, keepdims=True)
        return m, jnp.min(jnp.where(v == m, lane, far), axis=-1, keepdims=True)

    gl = jnp.where(lane < n_groups, lg, NEG_BIG)
    gm, gsel = first_lane_of_max(gl)
    gw = 1.0 / jnp.sum(jnp.exp(gl - gm), axis=-1, keepdims=True)
    lo = n_groups + gsel * per_group
    el = jnp.where((lane >= lo) & (lane < lo + per_group), lg, NEG_BIG)
    m1, i1 = first_lane_of_max(el)
    m2, i2 = first_lane_of_max(jnp.where(lane == i1, NEG_BIG, el))
    e2 = jnp.exp(m2 - m1)
    den = 1.0 + e2
    id1 = i1 - n_groups
    id2 = i2 - n_groups

    @pl.when(pl.program_id(0) == 0)
    def _():
        counts_ref[...] = jnp.zeros(counts_ref.shape, counts_ref.dtype)

    hot1 = lane == id1
    hot2 = lane == id2
    hot = jnp.where(hot1 | hot2, 1.0, 0.0)
    tm = lg.shape[0]
    strict_lower = (lax.broadcasted_iota(I32, (tm, tm), 0) > lax.broadcasted_iota(I32, (tm, tm), 1))
    before = jnp.dot(jnp.where(strict_lower, 1.0, 0.0).astype(BF16), hot.astype(BF16),
                     preferred_element_type=F32) + counts_ref[...]
    rank1 = jnp.sum(jnp.where(hot1, before, 0.0), axis=-1, keepdims=True)
    rank2 = jnp.sum(jnp.where(hot2, before, 0.0), axis=-1, keepdims=True)
    counts_ref[...] = counts_ref[...] + jnp.sum(hot, axis=0, keepdims=True)

    route = jnp.where(lane == 0, id1,
                      jnp.where(lane == 1, id2,
                                jnp.where(lane == 2, gw / den,
                                          jnp.where(lane == 3, gw * e2 / den,
                                                    jnp.where(lane == 4, rank1,
                                                              jnp.where(lane == 5, rank2, 0.0))))))
    route_ref[...] = route


def _post_block(x, f, ln_g, ln_b, w_q, kv, w_o, w_r, b_r, n_groups, per_group, *, tm=ROW_TILE):
    t, d = x.shape
    inner = w_q.shape[1]
    n_mem = kv.shape[0]
    n_slabs = d // 2 // LANES
    row = lambda a: a.reshape(1, d).astype(F32)
    full = lambda shape: pl.BlockSpec(shape, lambda i: (0,) * len(shape))
    tile = lambda w: pl.BlockSpec((tm, w), lambda i: (i, 0))
    return pl.pallas_call(
        functools.partial(_post_kernel, n_groups=n_groups, per_group=per_group),
        grid=(t // tm,),
        in_specs=[tile(d), tile(d), full((1, d)), full((1, d)), full((1, d)), full((1, d)),
                  full((d, inner)), full((n_mem, 2 * inner)), full((inner, d)), full((d, LANES)), full((1, LANES))],
        out_specs=[tile(d), pl.BlockSpec((tm * n_slabs, LANES), lambda i: (i, 0)), tile(LANES), full((1, LANES))],
        out_shape=[jax.ShapeDtypeStruct((t, d), F32), jax.ShapeDtypeStruct((t * n_slabs, LANES), U32),
                   jax.ShapeDtypeStruct((t, LANES), F32), jax.ShapeDtypeStruct((1, LANES), F32)],
        compiler_params=_params(1),
    )(x, f, row(ln_g[0]), row(ln_b[0]), row(ln_g[1]), row(ln_b[1]), w_q, kv, w_o, w_r, b_r)


MOE_CAST_ROWS = 128
MOE_WEIGHT_COPIES = 8


def _moe_kernel(be_ref, first_ref, next_ref, src_ref, nu_ref, xg_hbm, wg_hbm, wu_hbm, wd_hbm, yb_ref,
                xbuf0, xbuf1, stage_g, stage_u, stage_d, wgu_ref, wd_ref, part_ref, sem, wsem, *, blk, layer):
    i = pl.program_id(0)
    n_used = nu_ref[0]
    bufs = (xbuf0, xbuf1)
    n_slabs = yb_ref.shape[0] // blk
    half = n_slabs * LANES
    de = wd_ref.shape[0]
    d = wd_ref.shape[1]

    def weight_copies(e):
        copies = []
        for k, (w_hbm, stage) in enumerate(((wg_hbm, stage_g), (wu_hbm, stage_u), (wd_hbm, stage_d))):
            rows = stage.shape[0] // MOE_WEIGHT_COPIES
            for c in range(MOE_WEIGHT_COPIES):
                piece = pl.ds(c * rows, rows)
                copies.append(pltpu.make_async_copy(w_hbm.at[layer, e, piece, :], stage.at[piece, :], wsem.at[k]))
        return copies

    @pl.when(i == 0)
    def _():
        for c in weight_copies(be_ref[0]):
            c.start()

    @pl.when((i < n_used) & (first_ref[i] == 1))
    def _():
        for c in weight_copies(be_ref[i]):
            c.wait()

        def cast_up(r, carry):
            rows = pl.ds(pl.multiple_of(r * MOE_CAST_ROWS, MOE_CAST_ROWS), MOE_CAST_ROWS)
            wgu_ref[rows, 0:de] = stage_g[rows, :].astype(BF16)
            wgu_ref[rows, de:2 * de] = stage_u[rows, :].astype(BF16)
            return carry

        def cast_down(r, carry):
            rows = pl.ds(pl.multiple_of(r * MOE_CAST_ROWS, MOE_CAST_ROWS), MOE_CAST_ROWS)
            for c0 in range(0, d, 1024):
                wd_ref[rows, c0:c0 + 1024] = stage_d[rows, c0:c0 + 1024].astype(BF16)
            return carry

        lax.fori_loop(0, d // MOE_CAST_ROWS, cast_up, 0)
        lax.fori_loop(0, de // MOE_CAST_ROWS, cast_down, 0)

        @pl.when(next_ref[i] >= 0)
        def _():
            for c in weight_copies(next_ref[i]):
                c.start()

    def row_copy(j, r, s):
        src = pl.multiple_of(src_ref[j * blk + r], n_slabs)
        return pltpu.make_async_copy(xg_hbm.at[pl.ds(src, n_slabs), :],
                                     bufs[s].at[pl.ds(r * GATHER_PITCH, n_slabs), :], sem.at[s])

    def wait_block(s):
        pltpu.make_async_copy(xg_hbm.at[pl.ds(0, blk * n_slabs), :], bufs[s].at[pl.ds(0, blk * n_slabs), :],
                              sem.at[s]).wait()

    @pl.when(i == 0)
    def _():
        def body(r, carry):
            src = pl.multiple_of(src_ref[r], n_slabs)
            dst = pl.multiple_of(r * GATHER_PITCH, 8)
            pltpu.make_async_copy(xg_hbm.at[pl.ds(src, n_slabs), :], xbuf0.at[pl.ds(dst, n_slabs), :],
                                  sem.at[0]).start()
            return carry
        lax.fori_loop(0, blk, body, 0)

    def expert_block(s):
        wait_block(s)
        los, his = [], []
        for q in range(n_slabs):
            lo_q, hi_q = _unpack_halves(_load_slab_column(bufs[s], q, blk))
            los.append(lo_q.astype(BF16))
            his.append(hi_q.astype(BF16))
        lo = jnp.concatenate(los, axis=1)
        hi = jnp.concatenate(his, axis=1)
        for r in range(blk):
            row_copy(i + 1, r, 1 - s).start()
        wu = 2 * de // MOE_UP_PIECES
        for c in range(MOE_UP_PIECES):
            cols = slice(c * wu, (c + 1) * wu)
            part_ref[:, cols] = jnp.dot(lo, wgu_ref[0:half, cols], preferred_element_type=F32)
        au = []
        for c in range(MOE_UP_PIECES):
            cols = slice(c * wu, (c + 1) * wu)
            au.append(part_ref[:, cols] + jnp.dot(hi, wgu_ref[half:, cols], preferred_element_type=F32))
        au = jnp.concatenate(au, axis=1)
        a = au[:, 0:de]
        u = au[:, de:]
        hmid = (a * jax.nn.sigmoid(a) * u).astype(BF16)
        wdn = half // MOE_DOWN_PIECES
        for c in range(MOE_DOWN_PIECES):
            y_lo = jnp.dot(hmid, wd_ref[:, c * wdn:(c + 1) * wdn], preferred_element_type=F32)
            y_hi = jnp.dot(hmid, wd_ref[:, half + c * wdn:half + (c + 1) * wdn], preferred_element_type=F32)
            _store_slabs(yb_ref, c * (wdn // LANES), _pack_halves(y_lo, y_hi))

    for s in range(2):
        pl.when((i < n_used) & (i % 2 == s))(functools.partial(expert_block, s))
        pl.when((i == n_used) & (i % 2 == s))(functools.partial(wait_block, s))

    @pl.when(i >= n_used)
    def _():
        yb_ref[...] = jnp.zeros(yb_ref.shape, yb_ref.dtype)


def _moe_experts(xg, plan, w_gate, w_up, w_down, layer, *, blk=MOE_BLOCK):
    n_slots = plan.slot_src.shape[0]
    nb = n_slots // blk
    de, d = w_down.shape[2:]
    n_slabs = d // 2 // LANES
    assert n_slabs <= GATHER_PITCH and GATHER_PITCH % 8 == 0
    assert d % MOE_CAST_ROWS == 0 and de % MOE_CAST_ROWS == 0 and d % 1024 == 0
    buf = pltpu.VMEM((blk * GATHER_PITCH, LANES), U32)
    any_spec = pl.BlockSpec(memory_space=pl.ANY)
    grid_spec = pltpu.PrefetchScalarGridSpec(
        num_scalar_prefetch=5,
        grid=(nb,),
        in_specs=[any_spec, any_spec, any_spec, any_spec],
        out_specs=pl.BlockSpec((blk * n_slabs, LANES), lambda i, *_: (i, 0)),
        scratch_shapes=[buf, buf,
                        pltpu.VMEM((d, de), F32), pltpu.VMEM((d, de), F32), pltpu.VMEM((de, d), F32),
                        pltpu.VMEM((d, 2 * de), BF16), pltpu.VMEM((de, d), BF16), pltpu.VMEM((blk, 2 * de), F32),
                        pltpu.SemaphoreType.DMA((2,)), pltpu.SemaphoreType.DMA((3,))],
    )
    return pl.pallas_call(
        functools.partial(_moe_kernel, blk=blk, layer=layer),
        grid_spec=grid_spec,
        out_shape=jax.ShapeDtypeStruct((n_slots * n_slabs, LANES), U32),
        compiler_params=_params(1),
    )(plan.blk_expert, plan.seg_first, plan.seg_next, plan.slot_src, plan.n_used, xg, w_gate, w_up, w_down)


def _combine_kernel(src_ref, x2_ref, route_ref, g_ref, b_ref, yb_hbm, x3_ref, x3b_ref, ybuf0, ybuf1, sem, *, tm):
    i = pl.program_id(0)
    n = pl.num_programs(0)
    bufs = (ybuf0, ybuf1)
    d = x2_ref.shape[1]
    half = d // 2
    n_slabs = half // LANES

    def wait_tile(s):
        for k in range(TOP_K):
            pltpu.make_async_copy(yb_hbm.at[pl.ds(0, tm * n_slabs), :], bufs[s].at[k, pl.ds(0, tm * n_slabs), :],
                                  sem.at[s]).wait()

    @pl.when(i == 0)
    def _():
        def body(r, carry):
            dst = pl.multiple_of(r * GATHER_PITCH, 8)
            for k in range(TOP_K):
                src = pl.multiple_of(src_ref[TOP_K * r + k], n_slabs)
                pltpu.make_async_copy(yb_hbm.at[pl.ds(src, n_slabs), :], ybuf0.at[k, pl.ds(dst, n_slabs), :],
                                      sem.at[0]).start()
            return carry

        lax.fori_loop(0, tm, body, 0)

    def tile(s):
        wait_tile(s)
        gates = [route_ref[:, TOP_K + k:TOP_K + k + 1] for k in range(TOP_K)]
        y_lo, y_hi = [], []
        for q in range(n_slabs):
            lo = hi = None
            for k in range(TOP_K):
                lo_k, hi_k = _unpack_halves(bufs[s][k, pl.ds(q, tm, stride=GATHER_PITCH), :])
                lo = lo_k * gates[k] if lo is None else lo + lo_k * gates[k]
                hi = hi_k * gates[k] if hi is None else hi + hi_k * gates[k]
            y_lo.append(DEEPNORM_ALPHA * x2_ref[:, q * LANES:(q + 1) * LANES] + lo)
            y_hi.append(DEEPNORM_ALPHA * x2_ref[:, half + q * LANES:half + (q + 1) * LANES] + hi)
        base = (i + 1) * tm * TOP_K
        for r in range(tm):
            for k in range(TOP_K):
                src = pl.multiple_of(src_ref[base + TOP_K * r + k], n_slabs)
                pltpu.make_async_copy(yb_hbm.at[pl.ds(src, n_slabs), :],
                                      bufs[1 - s].at[k, pl.ds(r * GATHER_PITCH, n_slabs), :], sem.at[1 - s]).start()
        cols = [slice(c * LANES, (c + 1) * LANES) for c in range(2 * n_slabs)]
        out = _ln_pieces(y_lo + y_hi, [g_ref[:, c] for c in cols], [b_ref[:, c] for c in cols])
        for c, o in zip(cols, out):
            x3_ref[:, c] = o
            x3b_ref[:, c] = o.astype(BF16)

    for s in range(2):
        pl.when(i % 2 == s)(functools.partial(tile, s))
        pl.when((i == n - 1) & (i % 2 == s))(functools.partial(wait_tile, 1 - s))


def _moe_combine(x2, route, yb, slot_of, ln_g, ln_b, *, tm=ROW_TILE):
    t, d = x2.shape
    half = d // 2
    row = lambda a: a.reshape(1, d).astype(F32)
    grid_spec = pltpu.PrefetchScalarGridSpec(
        num_scalar_prefetch=1,
        grid=(t // tm,),
        in_specs=[pl.BlockSpec((tm, d), lambda i, dst: (i, 0)),
                  pl.BlockSpec((tm, LANES), lambda i, dst: (i, 0)),
                  pl.BlockSpec((1, d), lambda i, dst: (0, 0)),
                  pl.BlockSpec((1, d), lambda i, dst: (0, 0)),
                  pl.BlockSpec(memory_space=pl.ANY)],
        out_specs=[pl.BlockSpec((tm, d), lambda i, dst: (i, 0)), pl.BlockSpec((tm, d), lambda i, dst: (i, 0))],
        scratch_shapes=[pltpu.VMEM((TOP_K, tm * GATHER_PITCH, LANES), U32),
                        pltpu.VMEM((TOP_K, tm * GATHER_PITCH, LANES), U32), pltpu.SemaphoreType.DMA((2,))],
    )
    return pl.pallas_call(
        functools.partial(_combine_kernel, tm=tm),
        grid_spec=grid_spec,
        out_shape=[jax.ShapeDtypeStruct((t, d), F32), jax.ShapeDtypeStruct((t, d), BF16)],
        compiler_params=_params(1),
    )(jnp.concatenate([slot_of, jnp.zeros((tm * TOP_K,), I32)]), x2, route, row(ln_g), row(ln_b), yb)


class DispatchPlan(NamedTuple):
    slot_of: jax.Array
    slot_src: jax.Array
    blk_expert: jax.Array
    seg_first: jax.Array
    seg_next: jax.Array
    n_used: jax.Array


def _dispatch_plan(route, counts, n_experts, blk, n_slabs):
    t = route.shape[0]
    ids = route[:, 0:TOP_K].astype(I32)
    rank = route[:, 2 * TOP_K:3 * TOP_K].astype(I32)
    counts = counts[0, :n_experts].astype(I32)
    padded = (counts + blk - 1) // blk * blk
    ends = jnp.cumsum(padded)
    starts = ends - padded
    expert_iota = jnp.arange(n_experts, dtype=I32)
    seg_start = jnp.sum(jnp.where(ids[:, :, None] == expert_iota, starts, 0), axis=-1)
    nb = (t * TOP_K) // blk + n_experts
    dest = jnp.clip(seg_start + rank, 0, nb * blk - 1).reshape(-1)
    flat_src = jnp.arange(t * TOP_K, dtype=I32) // TOP_K * n_slabs
    slot_src = jnp.zeros((nb * blk,), I32).at[dest].set(flat_src, unique_indices=True)
    blk_start = jnp.arange(nb, dtype=I32) * blk
    blk_expert = jnp.minimum(jnp.sum((ends[None, :] <= blk_start[:, None]).astype(I32), axis=1), n_experts - 1)
    n_used = ends[-1] // blk
    blk_iota = jnp.arange(nb, dtype=I32)
    used = blk_iota < n_used
    prev_expert = jnp.concatenate([jnp.full((1,), -1, I32), blk_expert[:-1]])
    seg_first = (used & (blk_expert != prev_expert)).astype(I32)
    later_segment = (blk_iota[None, :] > blk_iota[:, None]) & used[None, :] & (blk_expert[None, :] != blk_expert[:, None])
    seg_next = jnp.where(jnp.any(later_segment, axis=1), blk_expert[jnp.argmax(later_segment, axis=1)], -1).astype(I32)
    return DispatchPlan(dest * n_slabs, slot_src, blk_expert.astype(I32), seg_first, seg_next,
                        n_used.astype(I32).reshape(1))


def kernel(x, mem, attn_w_qkv, attn_w_o, attn_rel_bias, conv_w_in, conv_b_in, conv_w_dw, conv_b_dw, conv_ln_g,
           conv_ln_b, conv_w_out, conv_b_out, pool_w, pool_scale, mem_w_q, mem_w_kv, mem_w_o, moe_w_group,
           moe_b_group, moe_w_router, moe_b_router, moe_w_gate, moe_w_up, moe_w_down, ln_g, ln_b):
    batch, seq, d = x.shape
    assert batch == 1 and ln_g.shape[0] == DEPTH
    n_groups = moe_w_group.shape[2]
    per_group = moe_w_router.shape[3]
    n_experts = n_groups * per_group
    assert n_groups + n_experts <= LANES

    xf = x.reshape(seq, d)
    xb = xf.astype(BF16)
    memb = mem.reshape(mem.shape[1], d).astype(BF16)

    for i in range(DEPTH):
        kind = i % N_MIXERS
        j = i // N_MIXERS
        if kind == 0:
            qkv = _matmul(xb, attn_w_qkv, j)
            o = _chunked_attention(qkv, attn_rel_bias[j], d)
            f = _matmul(o, attn_w_o, j)
        elif kind == 1:
            u = _glu_matmul(xb, conv_w_in, j, conv_b_in[j])
            v = _conv_ln_swish(u, conv_w_dw[j], conv_b_dw[j], conv_ln_g[j], conv_ln_b[j])
            f = _matmul(v, conv_w_out, j, conv_b_out[j])
        else:
            f = _multiscale_pool(xf, pool_w[j].astype(BF16), pool_scale[j])

        kv = _matmul(memb, mem_w_kv, i, tm=memb.shape[0])
        w_r = jnp.concatenate([moe_w_group[i], moe_w_router[i].transpose(1, 0, 2).reshape(d, n_experts)], axis=1)
        w_r = jnp.pad(w_r, ((0, 0), (0, LANES - w_r.shape[1]))).astype(BF16)
        b_r = jnp.concatenate([moe_b_group[i], moe_b_router[i].reshape(n_experts)])
        b_r = jnp.pad(b_r, (0, LANES - b_r.shape[0])).reshape(1, LANES).astype(F32)
        x2, x2p, route, counts = _post_block(xf, f, ln_g[i], ln_b[i], mem_w_q[i].astype(BF16), kv,
                                             mem_w_o[i].astype(BF16), w_r, b_r, n_groups, per_group)

        plan = _dispatch_plan(route, counts, n_experts, MOE_BLOCK, d // 2 // LANES)
        yb = _moe_experts(x2p, plan, moe_w_gate, moe_w_up, moe_w_down, i)
        xf, xb = _moe_combine(x2, route, yb, plan.slot_of, ln_g[i, 2], ln_b[i, 2])

    return xf.reshape(batch, seq, d)
```

```python
import functools
from typing import NamedTuple

import jax
import jax.numpy as jnp
from jax import lax
from jax.experimental import pallas as pl
from jax.experimental.pallas import tpu as pltpu

F32 = jnp.float32
BF16 = jnp.bfloat16
U32 = jnp.uint32
I32 = jnp.int32

DEPTH = 4
N_MIXERS = 3
ATTN_HEADS = 32
CHUNK = 64
LEFT_CHUNKS = 8
POOL_WINDOWS = (2, 4, 8, 16)
MEM_HEADS = 4
TOP_K = 2
LN_EPS = 1e-5
DEEPNORM_ALPHA = (2 * DEPTH) ** 0.25
NEG_BIG = -1e30
LOG2_E = 1.4426950408889634

V7X_VMEM_BYTES = 64 * 1024 * 1024
VMEM_LIMIT = V7X_VMEM_BYTES - 8 * 1024 * 1024
LANES = 128

ATTN_Q_TILE = 4 * CHUNK
ATTN_HEADS_PER_STEP = 8
MOE_BLOCK = 256
MOE_UP_PIECES = 3
MOE_DOWN_PIECES = 8
ROW_TILE = 256
POST_ROW_CHUNKS = 1


def _params(n_grid_dims):
    return pltpu.CompilerParams(dimension_semantics=("arbitrary",) * n_grid_dims,
                                vmem_limit_bytes=VMEM_LIMIT)


def _ln_pieces(pieces, g_pieces, b_pieces):
    d = sum(p.shape[-1] for p in pieces)
    mu = sum(jnp.sum(p, axis=-1, keepdims=True) for p in pieces) * (1.0 / d)
    cen = [p - mu for p in pieces]
    var = sum(jnp.sum(c * c, axis=-1, keepdims=True) for c in cen) * (1.0 / d)
    inv = lax.rsqrt(var + LN_EPS)
    return [c * inv * g + b for c, g, b in zip(cen, g_pieces, b_pieces)]


def _pack_halves(lo, hi):
    half_ulp = jnp.uint32(0x8000)
    lo_bits = (lax.bitcast_convert_type(lo, U32) + half_ulp) >> 16
    hi_bits = (lax.bitcast_convert_type(hi, U32) + half_ulp) & jnp.uint32(0xFFFF0000)
    return lo_bits | hi_bits


def _unpack_halves(w):
    lo = lax.bitcast_convert_type(w << 16, F32)
    hi = lax.bitcast_convert_type(w & jnp.uint32(0xFFFF0000), F32)
    return lo, hi


GATHER_PITCH = 24


def _store_slabs(ref, first_slab, packed, n_slabs=None, row0=0):
    rows = packed.shape[0]
    if n_slabs is None:
        n_slabs = ref.shape[0] // rows
    for q in range(packed.shape[1] // LANES):
        ref[pl.ds(row0 + first_slab + q, rows, stride=n_slabs), :] = packed[:, q * LANES:(q + 1) * LANES]


def _load_slab_column(ref_2d, s, rows):
    return ref_2d[pl.ds(s, rows, stride=GATHER_PITCH), :]


MM_CAST_ROWS = 512


def _cast_weight_tile(w_ref, w_bf16_ref):
    k = w_ref.shape[0]
    step = min(MM_CAST_ROWS, k)
    for r in range(0, k, step):
        w_bf16_ref[r:r + step, :] = w_ref[r:r + step, :].astype(BF16)


def _mm_kernel(x_ref, w_ref, *rest, has_bias):
    if has_bias:
        b_ref, o_ref, w_bf16_ref = rest
    else:
        o_ref, w_bf16_ref = rest

    @pl.when(pl.program_id(1) == 0)
    def _():
        _cast_weight_tile(w_ref, w_bf16_ref)

    acc = jnp.dot(x_ref[...], w_bf16_ref[...], preferred_element_type=F32)
    if has_bias:
        acc = acc + b_ref[...]
    o_ref[...] = acc.astype(o_ref.dtype)


def _matmul(x, w_stack, layer, bias=None, *, out_dtype=BF16, tm=512, tn=1024, column_blocked=False):
    m, k = x.shape
    n = w_stack.shape[2]
    tm = min(tm, m)
    tn = min(tn, n)
    assert m % tm == 0 and n % tn == 0 and w_stack.shape[1] == k
    in_specs = [pl.BlockSpec((tm, k), lambda j, i: (i, 0)),
                pl.BlockSpec((None, k, tn), lambda j, i: (layer, 0, j))]
    args = [x, w_stack]
    if bias is not None:
        in_specs.append(pl.BlockSpec((1, tn), lambda j, i: (0, j)))
        args.append(bias.reshape(1, n).astype(F32))
    return pl.pallas_call(
        functools.partial(_mm_kernel, has_bias=bias is not None),
        grid=(n // tn, m // tm),
        in_specs=in_specs,
        out_specs=(pl.BlockSpec((None, tm, tn), lambda j, i: (j, i, 0)) if column_blocked
                   else pl.BlockSpec((tm, tn), lambda j, i: (i, j))),
        out_shape=jax.ShapeDtypeStruct((n // tn, m, tn) if column_blocked else (m, n), out_dtype),
        scratch_shapes=[pltpu.VMEM((k, tn), BF16)],
        compiler_params=_params(2),
    )(*args)


def _attn_kernel(q_ref, k0_ref, k1_ref, k2_ref, v0_ref, v1_ref, v2_ref, u_ref, o_ref, bias_ref, *, hb, dh, qt):
    i = pl.program_id(1)

    @pl.when(i == 0)
    def _():
        shift = CHUNK.bit_length() - 1
        qc = lax.shift_right_logical(lax.broadcasted_iota(I32, (qt, 3 * qt), 0), shift)
        kc = lax.shift_right_logical(lax.broadcasted_iota(I32, (qt, 3 * qt), 1), shift)
        band = (kc >= qc) & (kc <= qc + LEFT_CHUNKS)
        for h in range(hb):
            rows = jnp.broadcast_to(u_ref[h:h + 1, :], (qt, u_ref.shape[1]))
            toep = pltpu.roll(rows, 0, 1, stride=1, stride_axis=0)
            bias_ref[h] = jnp.where(band, toep[:, 0:3 * qt] * LOG2_E, NEG_BIG)

    scale = dh ** -0.5 * LOG2_E
    k_refs = (k0_ref, k1_ref, k2_ref)
    v_refs = (v0_ref, v1_ref, v2_ref)

    def heads(at_sequence_start):
        pens = [jnp.where(i + j >= 2, 0.0, NEG_BIG).astype(F32) if at_sequence_start else None for j in range(2)]
        pens.append(None)
        for h in range(hb):
            sl = slice(h * dh, (h + 1) * dh)
            q = (q_ref[:, sl].astype(F32) * scale).astype(BF16)
            s = []
            for j in range(3):
                sj = lax.dot_general(q, k_refs[j][:, sl], (((1,), (1,)), ((), ())), preferred_element_type=F32)
                sj = sj + bias_ref[h, :, j * qt:(j + 1) * qt]
                if pens[j] is not None:
                    sj = sj + pens[j]
                s.append(sj)
            m = jnp.maximum(jnp.maximum(jnp.max(s[0], axis=-1, keepdims=True),
                                        jnp.max(s[1], axis=-1, keepdims=True)),
                            jnp.max(s[2], axis=-1, keepdims=True))
            p = [jnp.exp2(sj - m) for sj in s]
            l = sum(jnp.sum(pj, axis=-1, keepdims=True) for pj in p)
            o = sum(jnp.dot(p[j].astype(BF16), v_refs[j][:, sl], preferred_element_type=F32) for j in range(3))
            o_ref[:, sl] = (o / l).astype(o_ref.dtype)

    pl.when(i < 2)(functools.partial(heads, True))
    pl.when(i >= 2)(functools.partial(heads, False))


def _attn_offset_bias(rel_table, qt):
    max_rel = (rel_table.shape[1] - 1) // 2
    kw = 3 * qt
    period = 4 * qt
    m = jnp.arange(period)
    off = jnp.where(m < kw, m, m - period)
    idx = jnp.clip(2 * qt - off, -max_rel, max_rel) + max_rel
    return jnp.take(rel_table.astype(F32), idx, axis=1)


def _chunked_attention(qkv, rel_table, d):
    t = qkv.shape[1]
    qt = ATTN_Q_TILE
    hb = ATTN_HEADS_PER_STEP
    dh = d // ATTN_HEADS
    assert LEFT_CHUNKS * CHUNK == 2 * qt and t % qt == 0 and ATTN_HEADS % hb == 0
    w = hb * dh
    ng = d // w
    assert qkv.shape == (3 * ng, t, w)
    u = _attn_offset_bias(rel_table, qt)

    def kv_spec(j, base):
        return pl.BlockSpec((None, qt, w), lambda g, i: (base + g, jnp.maximum(i - 2 + j, 0), 0))

    return pl.pallas_call(
        functools.partial(_attn_kernel, hb=hb, dh=dh, qt=qt),
        grid=(ng, t // qt),
        in_specs=[pl.BlockSpec((None, qt, w), lambda g, i: (g, i, 0))]
                 + [kv_spec(j, ng) for j in range(3)]
                 + [kv_spec(j, 2 * ng) for j in range(3)]
                 + [pl.BlockSpec((hb, u.shape[1]), lambda g, i: (g, 0))],
        out_specs=pl.BlockSpec((qt, w), lambda g, i: (i, g)),
        out_shape=jax.ShapeDtypeStruct((t, d), BF16),
        scratch_shapes=[pltpu.VMEM((hb, qt, 3 * qt), F32)],
        compiler_params=_params(2),
    )(qkv, qkv, qkv, qkv, qkv, qkv, qkv, u)


def _glu_kernel(x_ref, wa_ref, wg_ref, ba_ref, bg_ref, o_ref, wa_bf16_ref, wg_bf16_ref):
    @pl.when(pl.program_id(1) == 0)
    def _():
        _cast_weight_tile(wa_ref, wa_bf16_ref)
        _cast_weight_tile(wg_ref, wg_bf16_ref)

    x = x_ref[...]
    a = jnp.dot(x, wa_bf16_ref[...], preferred_element_type=F32) + ba_ref[...]
    g = jnp.dot(x, wg_bf16_ref[...], preferred_element_type=F32) + bg_ref[...]
    o_ref[...] = (a * jax.nn.sigmoid(g)).astype(o_ref.dtype)


def _glu_matmul(x, w_stack, layer, b, *, tm=512, tn=512):
    m, k = x.shape
    n = w_stack.shape[2] // 2
    assert m % tm == 0 and n % tn == 0
    nb = n // tn
    b2 = b.reshape(1, 2 * n).astype(F32)
    return pl.pallas_call(
        _glu_kernel,
        grid=(nb, m // tm),
        in_specs=[pl.BlockSpec((tm, k), lambda j, i: (i, 0)),
                  pl.BlockSpec((None, k, tn), lambda j, i: (layer, 0, j)),
                  pl.BlockSpec((None, k, tn), lambda j, i: (layer, 0, nb + j)),
                  pl.BlockSpec((1, tn), lambda j, i: (0, j)),
                  pl.BlockSpec((1, tn), lambda j, i: (0, nb + j))],
        out_specs=pl.BlockSpec((tm, tn), lambda j, i: (i, j)),
        out_shape=jax.ShapeDtypeStruct((m, n), BF16),
        scratch_shapes=[pltpu.VMEM((k, tn), BF16), pltpu.VMEM((k, tn), BF16)],
        compiler_params=_params(2),
    )(x, w_stack, w_stack, b2, b2)


CONV_HALO = 32
CONV_ROWS = 64
CONV_COLS = 128


def _conv_kernel(u_ref, halo_ref, w_ref, bdw_ref, g_ref, b_ref, o_ref, ext_ref, acc_ref, shift_ref, *, width, tm):
    i = pl.program_id(0)
    d = u_ref.shape[1]
    halo = halo_ref[...].astype(F32)
    ext_ref[0:CONV_HALO, :] = jnp.where(i > 0, halo, 0.0)
    ext_ref[CONV_HALO:, :] = u_ref[...].astype(F32)
    off = CONV_HALO - (width - 1)

    def col_chunk(c, carry):
        c0 = pl.multiple_of(c * CONV_COLS, CONV_COLS)
        w = w_ref[:, pl.ds(c0, CONV_COLS)]
        for b in range(8):
            rows = tm + (width - 1 - b) // 8 * 8
            shift_ref[b, 0:rows, :] = ext_ref[pl.ds(off + b, rows), pl.ds(c0, CONV_COLS)]
        for r0 in range(0, tm, CONV_ROWS):
            acc = jnp.zeros((CONV_ROWS, CONV_COLS), F32)
            for k in range(width):
                a8 = k - k % 8
                acc = acc + shift_ref[k % 8, r0 + a8:r0 + a8 + CONV_ROWS, :] * w[k:k + 1, :]
            acc_ref[r0:r0 + CONV_ROWS, pl.ds(c0, CONV_COLS)] = acc + bdw_ref[:, pl.ds(c0, CONV_COLS)]
        return carry

    lax.fori_loop(0, d // CONV_COLS, col_chunk, 0)
    (y,) = _ln_pieces([acc_ref[...]], [g_ref[...]], [b_ref[...]])
    o_ref[...] = (y * jax.nn.sigmoid(y)).astype(o_ref.dtype)


def _conv_ln_swish(u, w_dw, b_dw, ln_g, ln_b, *, tm=ROW_TILE):
    t, d = u.shape
    width = w_dw.shape[0]
    assert width - 1 <= CONV_HALO and t % tm == 0 and tm % CONV_HALO == 0 and d % CONV_COLS == 0
    hb = tm // CONV_HALO
    row = lambda a: a.reshape(1, d).astype(F32)
    return pl.pallas_call(
        functools.partial(_conv_kernel, width=width, tm=tm),
        grid=(t // tm,),
        in_specs=[pl.BlockSpec((tm, d), lambda i: (i, 0)),
                  pl.BlockSpec((CONV_HALO, d), lambda i: (jnp.maximum(i * hb - 1, 0), 0)),
                  pl.BlockSpec((width, d), lambda i: (0, 0)),
                  pl.BlockSpec((1, d), lambda i: (0, 0)),
                  pl.BlockSpec((1, d), lambda i: (0, 0)),
                  pl.BlockSpec((1, d), lambda i: (0, 0))],
        out_specs=pl.BlockSpec((tm, d), lambda i: (i, 0)),
        out_shape=jax.ShapeDtypeStruct((t, d), BF16),
        scratch_shapes=[pltpu.VMEM((tm + CONV_HALO, d), F32), pltpu.VMEM((tm, d), F32),
                        pltpu.VMEM((8, tm + (width - 1) // 8 * 8, CONV_COLS), F32)],
        compiler_params=_params(1),
    )(u, u, w_dw.astype(F32), row(b_dw), row(ln_g), row(ln_b))


POOL_HALO = 16


def _pool_kernel(x_ref, halo_ref, w_ref, scale_ref, o_ref, ext_ref, *, tm, gw):
    i = pl.program_id(0)
    ext_ref[0:POOL_HALO, :] = jnp.where(i > 0, halo_ref[...], 0.0)
    ext_ref[POOL_HALO:, :] = x_ref[...]
    t_abs = i * tm + lax.broadcasted_iota(I32, (tm, 1), 0)
    for g, win in enumerate(POOL_WINDOWS):
        cols = slice(g * gw, (g + 1) * gw)
        acc = ext_ref[POOL_HALO:POOL_HALO + tm, cols]
        for j in range(1, win):
            acc = acc + ext_ref[POOL_HALO - j:POOL_HALO - j + tm, cols]
        cnt = jnp.minimum(t_abs + 1, win).astype(F32)
        pooled = acc / cnt - ext_ref[POOL_HALO:POOL_HALO + tm, cols]
        y = jnp.dot(pooled.astype(BF16), w_ref[g], preferred_element_type=F32)
        o_ref[:, cols] = (y * scale_ref[:, cols]).astype(o_ref.dtype)


def _multiscale_pool(x, w_pool, scale, *, tm=ROW_TILE):
    t, d = x.shape
    ng, gw, _ = w_pool.shape
    assert ng == len(POOL_WINDOWS) and max(POOL_WINDOWS) <= POOL_HALO and t % tm == 0
    hb = tm // POOL_HALO
    return pl.pallas_call(
        functools.partial(_pool_kernel, tm=tm, gw=gw),
        grid=(t // tm,),
        in_specs=[pl.BlockSpec((tm, d), lambda i: (i, 0)),
                  pl.BlockSpec((POOL_HALO, d), lambda i: (jnp.maximum(i * hb - 1, 0), 0)),
                  pl.BlockSpec((ng, gw, gw), lambda i: (0, 0, 0)),
                  pl.BlockSpec((1, d), lambda i: (0, 0))],
        out_specs=pl.BlockSpec((tm, d), lambda i: (i, 0)),
        out_shape=jax.ShapeDtypeStruct((t, d), BF16),
        scratch_shapes=[pltpu.VMEM((tm + POOL_HALO, d), F32)],
        compiler_params=_params(1),
    )(x, x, w_pool, scale.reshape(1, d).astype(F32))


def _post_kernel(x_ref, f_ref, g0_ref, b0_ref, g1_ref, b1_ref, wq_ref, kv_ref, wo_ref, wr_ref, br_ref,
                 x2_ref, x2p_ref, route_ref, counts_ref, *, n_groups, per_group):
    d = x_ref.shape[1]
    half = d // 2
    n_mem, inner2 = kv_ref.shape
    inner = inner2 // 2
    dh = inner // MEM_HEADS
    n_slabs = half // LANES

    @pl.when(pl.program_id(0) == 0)
    def _():
        counts_ref[...] = jnp.zeros(counts_ref.shape, counts_ref.dtype)

    tm = x_ref.shape[0]
    rc = tm // POST_ROW_CHUNKS
    for r0 in range(0, tm, rc):
        _post_rows(slice(r0, r0 + rc), r0 * n_slabs, x_ref, f_ref, g0_ref, b0_ref, g1_ref, b1_ref, wq_ref, kv_ref,
                   wo_ref, wr_ref, br_ref, x2_ref, x2p_ref, route_ref, counts_ref, n_groups, per_group)


def _post_rows(rows, slab_row0, x_ref, f_ref, g0_ref, b0_ref, g1_ref, b1_ref, wq_ref, kv_ref, wo_ref, wr_ref, br_ref,
               x2_ref, x2p_ref, route_ref, counts_ref, n_groups, per_group):
    d = x_ref.shape[1]
    half = d // 2
    n_slabs = half // LANES
    inner = kv_ref.shape[1] // 2
    dh = inner // MEM_HEADS
    y = DEEPNORM_ALPHA * x_ref[rows, :] + f_ref[rows, :].astype(F32)
    (x1,) = _ln_pieces([y], [g0_ref[...]], [b0_ref[...]])
    q = jnp.dot(x1.astype(BF16), wq_ref[...], preferred_element_type=F32) * (dh ** -0.5)
    heads = []
    for h in range(MEM_HEADS):
        qh = q[:, h * dh:(h + 1) * dh].astype(BF16)
        kh = kv_ref[:, h * dh:(h + 1) * dh]
        vh = kv_ref[:, inner + h * dh:inner + (h + 1) * dh]
        s = lax.dot_general(qh, kh, (((1,), (1,)), ((), ())), preferred_element_type=F32)
        p = jnp.exp(s - jnp.max(s, axis=-1, keepdims=True))
        l = jnp.sum(p, axis=-1, keepdims=True)
        heads.append((jnp.dot(p.astype(BF16), vh, preferred_element_type=F32) / l).astype(BF16))
    o = jnp.concatenate(heads, axis=1)
    c = jnp.dot(o, wo_ref[...], preferred_element_type=F32)
    (x2,) = _ln_pieces([DEEPNORM_ALPHA * x1 + c], [g1_ref[...]], [b1_ref[...]])
    x2_ref[rows, :] = x2
    _store_slabs(x2p_ref, 0, _pack_halves(x2[:, :half], x2[:, half:]), n_slabs=n_slabs, row0=slab_row0)

    lg = jnp.dot(x2.astype(BF16), wr_ref[...], preferred_element_type=F32) + br_ref[...]
    lane = lax.broadcasted_iota(I32, lg.shape, 1).astype(F32)
    far = float(LANES)

    def first_lane_of_max(v):
        m = jnp.max(v, axis=-1, keepdims=True)
        return m, jnp.min(jnp.where(v == m, lane, far), axis=-1, keepdims=True)

    gl = jnp.where(lane < n_groups, lg, NEG_BIG)
    gm, gsel = first_lane_of_max(gl)
    gw = 1.0 / jnp.sum(jnp.exp(gl - gm), axis=-1, keepdims=True)
    lo = n_groups + gsel * per_group
    el = jnp.where((lane >= lo) & (lane < lo + per_group), lg, NEG_BIG)
    m1, i1 = first_lane_of_max(el)
    m2, i2 = first_lane_of_max(jnp.where(lane == i1, NEG_BIG, el))
    e2 = jnp.exp(m2 - m1)
    den = 1.0 + e2
    id1 = i1 - n_groups
    id2 = i2 - n_groups

    hot1 = lane == id1
    hot2 = lane == id2
    hot = jnp.where(hot1 | hot2, 1.0, 0.0)
    tm = lg.shape[0]
    strict_lower = (lax.broadcasted_iota(I32, (tm, tm), 0) > lax.broadcasted_iota(I32, (tm, tm), 1))
    before = jnp.dot(jnp.where(strict_lower, 1.0, 0.0).astype(BF16), hot.astype(BF16),
                     preferred_element_type=F32) + counts_ref[...]
    rank1 = jnp.sum(jnp.where(hot1, before, 0.0), axis=-1, keepdims=True)
    rank2 = jnp.sum(jnp.where(hot2, before, 0.0), axis=-1, keepdims=True)
    counts_ref[...] = counts_ref[...] + jnp.sum(hot, axis=0, keepdims=True)

    route = jnp.where(lane == 0, id1,
                      jnp.where(lane == 1, id2,
                                jnp.where(lane == 2, gw / den,
                                          jnp.where(lane == 3, gw * e2 / den,
                                                    jnp.where(lane == 4, rank1,
                                                              jnp.where(lane == 5, rank2, 0.0))))))
    route_ref[rows, :] = route


def _post_block(x, f, ln_g, ln_b, w_q, kv, w_o, w_r, b_r, n_groups, per_group, *, tm=ROW_TILE):
    t, d = x.shape
    inner = w_q.shape[1]
    n_mem = kv.shape[0]
    n_slabs = d // 2 // LANES
    row = lambda a: a.reshape(1, d).astype(F32)
    full = lambda shape: pl.BlockSpec(shape, lambda i: (0,) * len(shape))
    tile = lambda w: pl.BlockSpec((tm, w), lambda i: (i, 0))
    return pl.pallas_call(
        functools.partial(_post_kernel, n_groups=n_groups, per_group=per_group),
        grid=(t // tm,),
        in_specs=[tile(d), tile(d), full((1, d)), full((1, d)), full((1, d)), full((1, d)),
                  full((d, inner)), full((n_mem, 2 * inner)), full((inner, d)), full((d, LANES)), full((1, LANES))],
        out_specs=[tile(d), pl.BlockSpec((tm * n_slabs, LANES), lambda i: (i, 0)), tile(LANES), full((1, LANES))],
        out_shape=[jax.ShapeDtypeStruct((t, d), F32), jax.ShapeDtypeStruct((t * n_slabs, LANES), U32),
                   jax.ShapeDtypeStruct((t, LANES), F32), jax.ShapeDtypeStruct((1, LANES), F32)],
        compiler_params=_params(1),
    )(x, f, row(ln_g[0]), row(ln_b[0]), row(ln_g[1]), row(ln_b[1]), w_q, kv, w_o, w_r, b_r)


MOE_CAST_ROWS = 128
MOE_WEIGHT_COPIES = 8


def _moe_kernel(be_ref, first_ref, next_ref, src_ref, nu_ref, xg_hbm, wg_hbm, wu_hbm, wd_hbm, yb_ref,
                xbuf0, xbuf1, stage_g, stage_u, stage_d, wgu_ref, wd_ref, part_ref, sem, wsem, *, blk, layer):
    i = pl.program_id(0)
    n_used = nu_ref[0]
    bufs = (xbuf0, xbuf1)
    n_slabs = yb_ref.shape[0] // blk
    half = n_slabs * LANES
    de = wd_ref.shape[0]
    d = wd_ref.shape[1]

    def weight_copies(e):
        copies = []
        for k, (w_hbm, stage) in enumerate(((wg_hbm, stage_g), (wu_hbm, stage_u), (wd_hbm, stage_d))):
            rows = stage.shape[0] // MOE_WEIGHT_COPIES
            for c in range(MOE_WEIGHT_COPIES):
                piece = pl.ds(c * rows, rows)
                copies.append(pltpu.make_async_copy(w_hbm.at[layer, e, piece, :], stage.at[piece, :], wsem.at[k]))
        return copies

    @pl.when(i == 0)
    def _():
        for c in weight_copies(be_ref[0]):
            c.start()

    @pl.when((i < n_used) & (first_ref[i] == 1))
    def _():
        for c in weight_copies(be_ref[i]):
            c.wait()

        def cast_up(r, carry):
            rows = pl.ds(pl.multiple_of(r * MOE_CAST_ROWS, MOE_CAST_ROWS), MOE_CAST_ROWS)
            wgu_ref[rows, 0:de] = stage_g[rows, :].astype(BF16)
            wgu_ref[rows, de:2 * de] = stage_u[rows, :].astype(BF16)
            return carry

        def cast_down(r, carry):
            rows = pl.ds(pl.multiple_of(r * MOE_CAST_ROWS, MOE_CAST_ROWS), MOE_CAST_ROWS)
            for c0 in range(0, d, 1024):
                wd_ref[rows, c0:c0 + 1024] = stage_d[rows, c0:c0 + 1024].astype(BF16)
            return carry

        lax.fori_loop(0, d // MOE_CAST_ROWS, cast_up, 0)
        lax.fori_loop(0, de // MOE_CAST_ROWS, cast_down, 0)

        @pl.when(next_ref[i] >= 0)
        def _():
            for c in weight_copies(next_ref[i]):
                c.start()

    def row_copy(j, r, s):
        src = pl.multiple_of(src_ref[j * blk + r], n_slabs)
        return pltpu.make_async_copy(xg_hbm.at[pl.ds(src, n_slabs), :],
                                     bufs[s].at[pl.ds(r * GATHER_PITCH, n_slabs), :], sem.at[s])

    def wait_block(s):
        pltpu.make_async_copy(xg_hbm.at[pl.ds(0, blk * n_slabs), :], bufs[s].at[pl.ds(0, blk * n_slabs), :],
                              sem.at[s]).wait()

    @pl.when(i == 0)
    def _():
        def body(r, carry):
            src = pl.multiple_of(src_ref[r], n_slabs)
            dst = pl.multiple_of(r * GATHER_PITCH, 8)
            pltpu.make_async_copy(xg_hbm.at[pl.ds(src, n_slabs), :], xbuf0.at[pl.ds(dst, n_slabs), :],
                                  sem.at[0]).start()
            return carry
        lax.fori_loop(0, blk, body, 0)

    def expert_block(s):
        wait_block(s)
        los, his = [], []
        for q in range(n_slabs):
            lo_q, hi_q = _unpack_halves(_load_slab_column(bufs[s], q, blk))
            los.append(lo_q.astype(BF16))
            his.append(hi_q.astype(BF16))
        lo = jnp.concatenate(los, axis=1)
        hi = jnp.concatenate(his, axis=1)
        for r in range(blk):
            row_copy(i + 1, r, 1 - s).start()
        wu = 2 * de // MOE_UP_PIECES
        for c in range(MOE_UP_PIECES):
            cols = slice(c * wu, (c + 1) * wu)
            part_ref[:, cols] = jnp.dot(lo, wgu_ref[0:half, cols], preferred_element_type=F32)
        au = []
        for c in range(MOE_UP_PIECES):
            cols = slice(c * wu, (c + 1) * wu)
            au.append(part_ref[:, cols] + jnp.dot(hi, wgu_ref[half:, cols], preferred_element_type=F32))
        au = jnp.concatenate(au, axis=1)
        a = au[:, 0:de]
        u = au[:, de:]
        hmid = (a * jax.nn.sigmoid(a) * u).astype(BF16)
        wdn = half // MOE_DOWN_PIECES
        for c in range(MOE_DOWN_PIECES):
            y_lo = jnp.dot(hmid, wd_ref[:, c * wdn:(c + 1) * wdn], preferred_element_type=F32)
            y_hi = jnp.dot(hmid, wd_ref[:, half + c * wdn:half + (c + 1) * wdn], preferred_element_type=F32)
            _store_slabs(yb_ref, c * (wdn // LANES), _pack_halves(y_lo, y_hi))

    for s in range(2):
        pl.when((i < n_used) & (i % 2 == s))(functools.partial(expert_block, s))
        pl.when((i == n_used) & (i % 2 == s))(functools.partial(wait_block, s))

    @pl.when(i >= n_used)
    def _():
        yb_ref[...] = jnp.zeros(yb_ref.shape, yb_ref.dtype)


def _moe_experts(xg, plan, w_gate, w_up, w_down, layer, *, blk=MOE_BLOCK):
    n_slots = plan.slot_src.shape[0]
    nb = n_slots // blk
    de, d = w_down.shape[2:]
    n_slabs = d // 2 // LANES
    assert n_slabs <= GATHER_PITCH and GATHER_PITCH % 8 == 0
    assert d % MOE_CAST_ROWS == 0 and de % MOE_CAST_ROWS == 0 and d % 1024 == 0
    buf = pltpu.VMEM((blk * GATHER_PITCH, LANES), U32)
    any_spec = pl.BlockSpec(memory_space=pl.ANY)
    grid_spec = pltpu.PrefetchScalarGridSpec(
        num_scalar_prefetch=5,
        grid=(nb,),
        in_specs=[any_spec, any_spec, any_spec, any_spec],
        out_specs=pl.BlockSpec((blk * n_slabs, LANES), lambda i, *_: (i, 0)),
        scratch_shapes=[buf, buf,
                        pltpu.VMEM((d, de), F32), pltpu.VMEM((d, de), F32), pltpu.VMEM((de, d), F32),
                        pltpu.VMEM((d, 2 * de), BF16), pltpu.VMEM((de, d), BF16), pltpu.VMEM((blk, 2 * de), F32),
                        pltpu.SemaphoreType.DMA((2,)), pltpu.SemaphoreType.DMA((3,))],
    )
    return pl.pallas_call(
        functools.partial(_moe_kernel, blk=blk, layer=layer),
        grid_spec=grid_spec,
        out_shape=jax.ShapeDtypeStruct((n_slots * n_slabs, LANES), U32),
        compiler_params=_params(1),
    )(plan.blk_expert, plan.seg_first, plan.seg_next, plan.slot_src, plan.n_used, xg, w_gate, w_up, w_down)


def _combine_kernel(src_ref, x2_ref, route_ref, g_ref, b_ref, yb_hbm, x3_ref, x3b_ref, ybuf0, ybuf1, sem, *, tm):
    i = pl.program_id(0)
    n = pl.num_programs(0)
    bufs = (ybuf0, ybuf1)
    d = x2_ref.shape[1]
    half = d // 2
    n_slabs = half // LANES

    def wait_tile(s):
        for k in range(TOP_K):
            pltpu.make_async_copy(yb_hbm.at[pl.ds(0, tm * n_slabs), :], bufs[s].at[k, pl.ds(0, tm * n_slabs), :],
                                  sem.at[s]).wait()

    @pl.when(i == 0)
    def _():
        def body(r, carry):
            dst = pl.multiple_of(r * GATHER_PITCH, 8)
            for k in range(TOP_K):
                src = pl.multiple_of(src_ref[TOP_K * r + k], n_slabs)
                pltpu.make_async_copy(yb_hbm.at[pl.ds(src, n_slabs), :], ybuf0.at[k, pl.ds(dst, n_slabs), :],
                                      sem.at[0]).start()
            return carry

        lax.fori_loop(0, tm, body, 0)

    def tile(s):
        wait_tile(s)
        gates = [route_ref[:, TOP_K + k:TOP_K + k + 1] for k in range(TOP_K)]
        y_lo, y_hi = [], []
        for q in range(n_slabs):
            lo = hi = None
            for k in range(TOP_K):
                lo_k, hi_k = _unpack_halves(bufs[s][k, pl.ds(q, tm, stride=GATHER_PITCH), :])
                lo = lo_k * gates[k] if lo is None else lo + lo_k * gates[k]
                hi = hi_k * gates[k] if hi is None else hi + hi_k * gates[k]
            y_lo.append(DEEPNORM_ALPHA * x2_ref[:, q * LANES:(q + 1) * LANES] + lo)
            y_hi.append(DEEPNORM_ALPHA * x2_ref[:, half + q * LANES:half + (q + 1) * LANES] + hi)
        base = (i + 1) * tm * TOP_K
        for r in range(tm):
            for k in range(TOP_K):
                src = pl.multiple_of(src_ref[base + TOP_K * r + k], n_slabs)
                pltpu.make_async_copy(yb_hbm.at[pl.ds(src, n_slabs), :],
                                      bufs[1 - s].at[k, pl.ds(r * GATHER_PITCH, n_slabs), :], sem.at[1 - s]).start()
        cols = [slice(c * LANES, (c + 1) * LANES) for c in range(2 * n_slabs)]
        out = _ln_pieces(y_lo + y_hi, [g_ref[:, c] for c in cols], [b_ref[:, c] for c in cols])
        for c, o in zip(cols, out):
            x3_ref[:, c] = o
            x3b_ref[:, c] = o.astype(BF16)

    for s in range(2):
        pl.when(i % 2 == s)(functools.partial(tile, s))
        pl.when((i == n - 1) & (i % 2 == s))(functools.partial(wait_tile, 1 - s))


def _moe_combine(x2, route, yb, slot_of, ln_g, ln_b, *, tm=ROW_TILE):
    t, d = x2.shape
    half = d // 2
    row = lambda a: a.reshape(1, d).astype(F32)
    grid_spec = pltpu.PrefetchScalarGridSpec(
        num_scalar_prefetch=1,
        grid=(t // tm,),
        in_specs=[pl.BlockSpec((tm, d), lambda i, dst: (i, 0)),
                  pl.BlockSpec((tm, LANES), lambda i, dst: (i, 0)),
                  pl.BlockSpec((1, d), lambda i, dst: (0, 0)),
                  pl.BlockSpec((1, d), lambda i, dst: (0, 0)),
                  pl.BlockSpec(memory_space=pl.ANY)],
        out_specs=[pl.BlockSpec((tm, d), lambda i, dst: (i, 0)), pl.BlockSpec((tm, d), lambda i, dst: (i, 0))],
        scratch_shapes=[pltpu.VMEM((TOP_K, tm * GATHER_PITCH, LANES), U32),
                        pltpu.VMEM((TOP_K, tm * GATHER_PITCH, LANES), U32), pltpu.SemaphoreType.DMA((2,))],
    )
    return pl.pallas_call(
        functools.partial(_combine_kernel, tm=tm),
        grid_spec=grid_spec,
        out_shape=[jax.ShapeDtypeStruct((t, d), F32), jax.ShapeDtypeStruct((t, d), BF16)],
        compiler_params=_params(1),
    )(jnp.concatenate([slot_of, jnp.zeros((tm * TOP_K,), I32)]), x2, route, row(ln_g), row(ln_b), yb)


class DispatchPlan(NamedTuple):
    slot_of: jax.Array
    slot_src: jax.Array
    blk_expert: jax.Array
    seg_first: jax.Array
    seg_next: jax.Array
    n_used: jax.Array


def _dispatch_plan(route, counts, n_experts, blk, n_slabs):
    t = route.shape[0]
    ids = route[:, 0:TOP_K].astype(I32)
    rank = route[:, 2 * TOP_K:3 * TOP_K].astype(I32)
    counts = counts[0, :n_experts].astype(I32)
    padded = (counts + blk - 1) // blk * blk
    ends = jnp.cumsum(padded)
    starts = ends - padded
    expert_iota = jnp.arange(n_experts, dtype=I32)
    seg_start = jnp.sum(jnp.where(ids[:, :, None] == expert_iota, starts, 0), axis=-1)
    nb = (t * TOP_K) // blk + n_experts
    dest = jnp.clip(seg_start + rank, 0, nb * blk - 1).reshape(-1)
    flat_src = jnp.arange(t * TOP_K, dtype=I32) // TOP_K * n_slabs
    slot_src = jnp.zeros((nb * blk,), I32).at[dest].set(flat_src, unique_indices=True)
    blk_start = jnp.arange(nb, dtype=I32) * blk
    blk_expert = jnp.minimum(jnp.sum((ends[None, :] <= blk_start[:, None]).astype(I32), axis=1), n_experts - 1)
    n_used = ends[-1] // blk
    blk_iota = jnp.arange(nb, dtype=I32)
    used = blk_iota < n_used
    prev_expert = jnp.concatenate([jnp.full((1,), -1, I32), blk_expert[:-1]])
    seg_first = (used & (blk_expert != prev_expert)).astype(I32)
    later_segment = (blk_iota[None, :] > blk_iota[:, None]) & used[None, :] & (blk_expert[None, :] != blk_expert[:, None])
    seg_next = jnp.where(jnp.any(later_segment, axis=1), blk_expert[jnp.argmax(later_segment, axis=1)], -1).astype(I32)
    return DispatchPlan(dest * n_slabs, slot_src, blk_expert.astype(I32), seg_first, seg_next,
                        n_used.astype(I32).reshape(1))


def kernel(x, mem, attn_w_qkv, attn_w_o, attn_rel_bias, conv_w_in, conv_b_in, conv_w_dw, conv_b_dw, conv_ln_g,
           conv_ln_b, conv_w_out, conv_b_out, pool_w, pool_scale, mem_w_q, mem_w_kv, mem_w_o, moe_w_group,
           moe_b_group, moe_w_router, moe_b_router, moe_w_gate, moe_w_up, moe_w_down, ln_g, ln_b):
    batch, seq, d = x.shape
    assert batch == 1 and ln_g.shape[0] == DEPTH
    n_groups = moe_w_group.shape[2]
    per_group = moe_w_router.shape[3]
    n_experts = n_groups * per_group
    assert n_groups + n_experts <= LANES

    xf = x.reshape(seq, d)
    xb = xf.astype(BF16)
    memb = mem.reshape(mem.shape[1], d).astype(BF16)

    for i in range(DEPTH):
        kind = i % N_MIXERS
        j = i // N_MIXERS
        if kind == 0:
            qkv = _matmul(xb, attn_w_qkv, j, tn=ATTN_HEADS_PER_STEP * (d // ATTN_HEADS), column_blocked=True)
            o = _chunked_attention(qkv, attn_rel_bias[j], d)
            f = _matmul(o, attn_w_o, j)
        elif kind == 1:
            u = _glu_matmul(xb, conv_w_in, j, conv_b_in[j])
            v = _conv_ln_swish(u, conv_w_dw[j], conv_b_dw[j], conv_ln_g[j], conv_ln_b[j])
            f = _matmul(v, conv_w_out, j, conv_b_out[j])
        else:
            f = _multiscale_pool(xf, pool_w[j].astype(BF16), pool_scale[j])

        kv = _matmul(memb, mem_w_kv, i, tm=memb.shape[0])
        w_r = jnp.concatenate([moe_w_group[i], moe_w_router[i].transpose(1, 0, 2).reshape(d, n_experts)], axis=1)
        w_r = jnp.pad(w_r, ((0, 0), (0, LANES - w_r.shape[1]))).astype(BF16)
        b_r = jnp.concatenate([moe_b_group[i], moe_b_router[i].reshape(n_experts)])
        b_r = jnp.pad(b_r, (0, LANES - b_r.shape[0])).reshape(1, LANES).astype(F32)
        x2, x2p, route, counts = _post_block(xf, f, ln_g[i], ln_b[i], mem_w_q[i].astype(BF16), kv,
                                             mem_w_o[i].astype(BF16), w_r, b_r, n_groups, per_group)

        plan = _dispatch_plan(route, counts, n_experts, MOE_BLOCK, d // 2 // LANES)
        yb = _moe_experts(x2p, plan, moe_w_gate, moe_w_up, moe_w_down, i)
        xf, xb = _moe_combine(x2, route, yb, plan.slot_of, ln_g[i, 2], ln_b[i, 2])

    return xf.reshape(batch, seq, d)
```

```python
import functools
from typing import NamedTuple

import jax
import jax.numpy as jnp
from jax import lax
from jax.experimental import pallas as pl
from jax.experimental.pallas import tpu as pltpu

F32 = jnp.float32
BF16 = jnp.bfloat16
U32 = jnp.uint32
I32 = jnp.int32

DEPTH = 4
N_MIXERS = 3
ATTN_HEADS = 32
CHUNK = 64
LEFT_CHUNKS = 8
POOL_WINDOWS = (2, 4, 8, 16)
MEM_HEADS = 4
TOP_K = 2
LN_EPS = 1e-5
DEEPNORM_ALPHA = (2 * DEPTH) ** 0.25
NEG_BIG = -1e30
LOG2_E = 1.4426950408889634

V7X_VMEM_BYTES = 64 * 1024 * 1024
VMEM_LIMIT = V7X_VMEM_BYTES - 8 * 1024 * 1024
LANES = 128

ATTN_Q_TILE = 4 * CHUNK
ATTN_HEADS_PER_STEP = 8
MOE_BLOCK = 256
MOE_UP_PIECES = 3
MOE_DOWN_PIECES = 8
ROW_TILE = 256
POST_ROW_CHUNKS = 1


def _params(n_grid_dims):
    return pltpu.CompilerParams(dimension_semantics=("arbitrary",) * n_grid_dims,
                                vmem_limit_bytes=VMEM_LIMIT)


def _ln_pieces(pieces, g_pieces, b_pieces):
    d = sum(p.shape[-1] for p in pieces)
    mu = sum(jnp.sum(p, axis=-1, keepdims=True) for p in pieces) * (1.0 / d)
    cen = [p - mu for p in pieces]
    var = sum(jnp.sum(c * c, axis=-1, keepdims=True) for c in cen) * (1.0 / d)
    inv = lax.rsqrt(var + LN_EPS)
    return [c * inv * g + b for c, g, b in zip(cen, g_pieces, b_pieces)]


def _pack_halves(lo, hi):
    half_ulp = jnp.uint32(0x8000)
    lo_bits = (lax.bitcast_convert_type(lo, U32) + half_ulp) >> 16
    hi_bits = (lax.bitcast_convert_type(hi, U32) + half_ulp) & jnp.uint32(0xFFFF0000)
    return lo_bits | hi_bits


def _unpack_halves(w):
    lo = lax.bitcast_convert_type(w << 16, F32)
    hi = lax.bitcast_convert_type(w & jnp.uint32(0xFFFF0000), F32)
    return lo, hi


GATHER_PITCH = 24


def _store_slabs(ref, first_slab, packed, n_slabs=None, row0=0):
    rows = packed.shape[0]
    if n_slabs is None:
        n_slabs = ref.shape[0] // rows
    for q in range(packed.shape[1] // LANES):
        ref[pl.ds(row0 + first_slab + q, rows, stride=n_slabs), :] = packed[:, q * LANES:(q + 1) * LANES]


def _load_slab_column(ref_2d, s, rows):
    return ref_2d[pl.ds(s, rows, stride=GATHER_PITCH), :]


MM_CAST_ROWS = 512


def _cast_weight_tile(w_ref, w_bf16_ref):
    k = w_ref.shape[0]
    step = min(MM_CAST_ROWS, k)
    for r in range(0, k, step):
        w_bf16_ref[r:r + step, :] = w_ref[r:r + step, :].astype(BF16)


def _mm_kernel(x_ref, w_ref, *rest, has_bias):
    if has_bias:
        b_ref, o_ref, w_bf16_ref = rest
    else:
        o_ref, w_bf16_ref = rest

    @pl.when(pl.program_id(1) == 0)
    def _():
        _cast_weight_tile(w_ref, w_bf16_ref)

    acc = jnp.dot(x_ref[...], w_bf16_ref[...], preferred_element_type=F32)
    if has_bias:
        acc = acc + b_ref[...]
    o_ref[...] = acc.astype(o_ref.dtype)


def _matmul(x, w_stack, layer, bias=None, *, out_dtype=BF16, tm=512, tn=1024, column_blocked=False):
    m, k = x.shape
    n = w_stack.shape[2]
    tm = min(tm, m)
    tn = min(tn, n)
    assert m % tm == 0 and n % tn == 0 and w_stack.shape[1] == k
    in_specs = [pl.BlockSpec((tm, k), lambda j, i: (i, 0)),
                pl.BlockSpec((None, k, tn), lambda j, i: (layer, 0, j))]
    args = [x, w_stack]
    if bias is not None:
        in_specs.append(pl.BlockSpec((1, tn), lambda j, i: (0, j)))
        args.append(bias.reshape(1, n).astype(F32))
    return pl.pallas_call(
        functools.partial(_mm_kernel, has_bias=bias is not None),
        grid=(n // tn, m // tm),
        in_specs=in_specs,
        out_specs=(pl.BlockSpec((None, tm, tn), lambda j, i: (j, i, 0)) if column_blocked
                   else pl.BlockSpec((tm, tn), lambda j, i: (i, j))),
        out_shape=jax.ShapeDtypeStruct((n // tn, m, tn) if column_blocked else (m, n), out_dtype),
        scratch_shapes=[pltpu.VMEM((k, tn), BF16)],
        compiler_params=_params(2),
    )(*args)


def _attn_kernel(q_ref, k0_ref, k1_ref, k2_ref, v0_ref, v1_ref, v2_ref, u_ref, o_ref, bias_ref, *, hb, dh, qt):
    i = pl.program_id(1)

    @pl.when(i == 0)
    def _():
        shift = CHUNK.bit_length() - 1
        qc = lax.shift_right_logical(lax.broadcasted_iota(I32, (qt, 3 * qt), 0), shift)
        kc = lax.shift_right_logical(lax.broadcasted_iota(I32, (qt, 3 * qt), 1), shift)
        band = (kc >= qc) & (kc <= qc + LEFT_CHUNKS)
        for h in range(hb):
            rows = jnp.broadcast_to(u_ref[h:h + 1, :], (qt, u_ref.shape[1]))
            toep = pltpu.roll(rows, 0, 1, stride=1, stride_axis=0)
            bias_ref[h] = jnp.where(band, toep[:, 0:3 * qt] * LOG2_E, NEG_BIG)

    scale = dh ** -0.5 * LOG2_E
    k_refs = (k0_ref, k1_ref, k2_ref)
    v_refs = (v0_ref, v1_ref, v2_ref)

    def heads(at_sequence_start):
        pens = [jnp.where(i + j >= 2, 0.0, NEG_BIG).astype(F32) if at_sequence_start else None for j in range(2)]
        pens.append(None)
        for h in range(hb):
            sl = slice(h * dh, (h + 1) * dh)
            q = (q_ref[:, sl].astype(F32) * scale).astype(BF16)
            s = []
            for j in range(3):
                sj = lax.dot_general(q, k_refs[j][:, sl], (((1,), (1,)), ((), ())), preferred_element_type=F32)
                sj = sj + bias_ref[h, :, j * qt:(j + 1) * qt]
                if pens[j] is not None:
                    sj = sj + pens[j]
                s.append(sj)
            m = jnp.maximum(jnp.maximum(jnp.max(s[0], axis=-1, keepdims=True),
                                        jnp.max(s[1], axis=-1, keepdims=True)),
                            jnp.max(s[2], axis=-1, keepdims=True))
            p = [jnp.exp2(sj - m) for sj in s]
            l = sum(jnp.sum(pj, axis=-1, keepdims=True) for pj in p)
            o = sum(jnp.dot(p[j].astype(BF16), v_refs[j][:, sl], preferred_element_type=F32) for j in range(3))
            o_ref[:, sl] = (o / l).astype(o_ref.dtype)

    pl.when(i < 2)(functools.partial(heads, True))
    pl.when(i >= 2)(functools.partial(heads, False))


def _attn_offset_bias(rel_table, qt):
    max_rel = (rel_table.shape[1] - 1) // 2
    kw = 3 * qt
    period = 4 * qt
    m = jnp.arange(period)
    off = jnp.where(m < kw, m, m - period)
    idx = jnp.clip(2 * qt - off, -max_rel, max_rel) + max_rel
    return jnp.take(rel_table.astype(F32), idx, axis=1)


def _chunked_attention(qkv, rel_table, d):
    t = qkv.shape[1]
    qt = ATTN_Q_TILE
    hb = ATTN_HEADS_PER_STEP
    dh = d // ATTN_HEADS
    assert LEFT_CHUNKS * CHUNK == 2 * qt and t % qt == 0 and ATTN_HEADS % hb == 0
    w = hb * dh
    ng = d // w
    assert qkv.shape == (3 * ng, t, w)
    u = _attn_offset_bias(rel_table, qt)

    def kv_spec(j, base):
        return pl.BlockSpec((None, qt, w), lambda g, i: (base + g, jnp.maximum(i - 2 + j, 0), 0))

    return pl.pallas_call(
        functools.partial(_attn_kernel, hb=hb, dh=dh, qt=qt),
        grid=(ng, t // qt),
        in_specs=[pl.BlockSpec((None, qt, w), lambda g, i: (g, i, 0))]
                 + [kv_spec(j, ng) for j in range(3)]
                 + [kv_spec(j, 2 * ng) for j in range(3)]
                 + [pl.BlockSpec((hb, u.shape[1]), lambda g, i: (g, 0))],
        out_specs=pl.BlockSpec((qt, w), lambda g, i: (i, g)),
        out_shape=jax.ShapeDtypeStruct((t, d), BF16),
        scratch_shapes=[pltpu.VMEM((hb, qt, 3 * qt), F32)],
        compiler_params=_params(2),
    )(qkv, qkv, qkv, qkv, qkv, qkv, qkv, u)


def _glu_kernel(x_ref, wa_ref, wg_ref, ba_ref, bg_ref, o_ref, wa_bf16_ref, wg_bf16_ref):
    @pl.when(pl.program_id(1) == 0)
    def _():
        _cast_weight_tile(wa_ref, wa_bf16_ref)
        _cast_weight_tile(wg_ref, wg_bf16_ref)

    x = x_ref[...]
    a = jnp.dot(x, wa_bf16_ref[...], preferred_element_type=F32) + ba_ref[...]
    g = jnp.dot(x, wg_bf16_ref[...], preferred_element_type=F32) + bg_ref[...]
    o_ref[...] = (a * jax.nn.sigmoid(g)).astype(o_ref.dtype)


def _glu_matmul(x, w_stack, layer, b, *, tm=512, tn=512):
    m, k = x.shape
    n = w_stack.shape[2] // 2
    assert m % tm == 0 and n % tn == 0
    nb = n // tn
    b2 = b.reshape(1, 2 * n).astype(F32)
    return pl.pallas_call(
        _glu_kernel,
        grid=(nb, m // tm),
        in_specs=[pl.BlockSpec((tm, k), lambda j, i: (i, 0)),
                  pl.BlockSpec((None, k, tn), lambda j, i: (layer, 0, j)),
                  pl.BlockSpec((None, k, tn), lambda j, i: (layer, 0, nb + j)),
                  pl.BlockSpec((1, tn), lambda j, i: (0, j)),
                  pl.BlockSpec((1, tn), lambda j, i: (0, nb + j))],
        out_specs=pl.BlockSpec((tm, tn), lambda j, i: (i, j)),
        out_shape=jax.ShapeDtypeStruct((m, n), BF16),
        scratch_shapes=[pltpu.VMEM((k, tn), BF16), pltpu.VMEM((k, tn), BF16)],
        compiler_params=_params(2),
    )(x, w_stack, w_stack, b2, b2)


CONV_HALO = 32
CONV_ROWS = 64
CONV_COLS = 128


def _conv_kernel(u_ref, halo_ref, w_ref, bdw_ref, g_ref, b_ref, o_ref, ext_ref, acc_ref, shift_ref, *, width, tm):
    i = pl.program_id(0)
    d = u_ref.shape[1]
    halo = halo_ref[...].astype(F32)
    ext_ref[0:CONV_HALO, :] = jnp.where(i > 0, halo, 0.0)
    ext_ref[CONV_HALO:, :] = u_ref[...].astype(F32)
    off = CONV_HALO - (width - 1)

    def col_chunk(c, carry):
        c0 = pl.multiple_of(c * CONV_COLS, CONV_COLS)
        w = w_ref[:, pl.ds(c0, CONV_COLS)]
        for b in range(8):
            rows = tm + (width - 1 - b) // 8 * 8
            shift_ref[b, 0:rows, :] = ext_ref[pl.ds(off + b, rows), pl.ds(c0, CONV_COLS)]
        for r0 in range(0, tm, CONV_ROWS):
            acc = jnp.zeros((CONV_ROWS, CONV_COLS), F32)
            for k in range(width):
                a8 = k - k % 8
                acc = acc + shift_ref[k % 8, r0 + a8:r0 + a8 + CONV_ROWS, :] * w[k:k + 1, :]
            acc_ref[r0:r0 + CONV_ROWS, pl.ds(c0, CONV_COLS)] = acc + bdw_ref[:, pl.ds(c0, CONV_COLS)]
        return carry

    lax.fori_loop(0, d // CONV_COLS, col_chunk, 0)
    (y,) = _ln_pieces([acc_ref[...]], [g_ref[...]], [b_ref[...]])
    o_ref[...] = (y * jax.nn.sigmoid(y)).astype(o_ref.dtype)


def _conv_ln_swish(u, w_dw, b_dw, ln_g, ln_b, *, tm=ROW_TILE):
    t, d = u.shape
    width = w_dw.shape[0]
    assert width - 1 <= CONV_HALO and t % tm == 0 and tm % CONV_HALO == 0 and d % CONV_COLS == 0
    hb = tm // CONV_HALO
    row = lambda a: a.reshape(1, d).astype(F32)
    return pl.pallas_call(
        functools.partial(_conv_kernel, width=width, tm=tm),
        grid=(t // tm,),
        in_specs=[pl.BlockSpec((tm, d), lambda i: (i, 0)),
                  pl.BlockSpec((CONV_HALO, d), lambda i: (jnp.maximum(i * hb - 1, 0), 0)),
                  pl.BlockSpec((width, d), lambda i: (0, 0)),
                  pl.BlockSpec((1, d), lambda i: (0, 0)),
                  pl.BlockSpec((1, d), lambda i: (0, 0)),
                  pl.BlockSpec((1, d), lambda i: (0, 0))],
        out_specs=pl.BlockSpec((tm, d), lambda i: (i, 0)),
        out_shape=jax.ShapeDtypeStruct((t, d), BF16),
        scratch_shapes=[pltpu.VMEM((tm + CONV_HALO, d), F32), pltpu.VMEM((tm, d), F32),
                        pltpu.VMEM((8, tm + (width - 1) // 8 * 8, CONV_COLS), F32)],
        compiler_params=_params(1),
    )(u, u, w_dw.astype(F32), row(b_dw), row(ln_g), row(ln_b))


POOL_HALO = 16


def _pool_kernel(x_ref, halo_ref, w_ref, scale_ref, o_ref, ext_ref, *, tm, gw):
    i = pl.program_id(0)
    ext_ref[0:POOL_HALO, :] = jnp.where(i > 0, halo_ref[...], 0.0)
    ext_ref[POOL_HALO:, :] = x_ref[...]
    t_abs = i * tm + lax.broadcasted_iota(I32, (tm, 1), 0)
    for g, win in enumerate(POOL_WINDOWS):
        cols = slice(g * gw, (g + 1) * gw)
        acc = ext_ref[POOL_HALO:POOL_HALO + tm, cols]
        for j in range(1, win):
            acc = acc + ext_ref[POOL_HALO - j:POOL_HALO - j + tm, cols]
        cnt = jnp.minimum(t_abs + 1, win).astype(F32)
        pooled = acc / cnt - ext_ref[POOL_HALO:POOL_HALO + tm, cols]
        y = jnp.dot(pooled.astype(BF16), w_ref[g], preferred_element_type=F32)
        o_ref[:, cols] = (y * scale_ref[:, cols]).astype(o_ref.dtype)


def _multiscale_pool(x, w_pool, scale, *, tm=ROW_TILE):
    t, d = x.shape
    ng, gw, _ = w_pool.shape
    assert ng == len(POOL_WINDOWS) and max(POOL_WINDOWS) <= POOL_HALO and t % tm == 0
    hb = tm // POOL_HALO
    return pl.pallas_call(
        functools.partial(_pool_kernel, tm=tm, gw=gw),
        grid=(t // tm,),
        in_specs=[pl.BlockSpec((tm, d), lambda i: (i, 0)),
                  pl.BlockSpec((POOL_HALO, d), lambda i: (jnp.maximum(i * hb - 1, 0), 0)),
                  pl.BlockSpec((ng, gw, gw), lambda i: (0, 0, 0)),
                  pl.BlockSpec((1, d), lambda i: (0, 0))],
        out_specs=pl.BlockSpec((tm, d), lambda i: (i, 0)),
        out_shape=jax.ShapeDtypeStruct((t, d), BF16),
        scratch_shapes=[pltpu.VMEM((tm + POOL_HALO, d), F32)],
        compiler_params=_params(1),
    )(x, x, w_pool, scale.reshape(1, d).astype(F32))


def _post_kernel(x_ref, f_ref, g0_ref, b0_ref, g1_ref, b1_ref, wq_ref, kv_ref, wo_ref, wr_ref, br_ref,
                 x2_ref, x2p_ref, route_ref, counts_ref, *, n_groups, per_group):
    d = x_ref.shape[1]
    half = d // 2
    n_mem, inner2 = kv_ref.shape
    inner = inner2 // 2
    dh = inner // MEM_HEADS
    n_slabs = half // LANES

    @pl.when(pl.program_id(0) == 0)
    def _():
        counts_ref[...] = jnp.zeros(counts_ref.shape, counts_ref.dtype)

    tm = x_ref.shape[0]
    rc = tm // POST_ROW_CHUNKS
    for r0 in range(0, tm, rc):
        _post_rows(slice(r0, r0 + rc), r0 * n_slabs, x_ref, f_ref, g0_ref, b0_ref, g1_ref, b1_ref, wq_ref, kv_ref,
                   wo_ref, wr_ref, br_ref, x2_ref, x2p_ref, route_ref, counts_ref, n_groups, per_group)


def _post_rows(rows, slab_row0, x_ref, f_ref, g0_ref, b0_ref, g1_ref, b1_ref, wq_ref, kv_ref, wo_ref, wr_ref, br_ref,
               x2_ref, x2p_ref, route_ref, counts_ref, n_groups, per_group):
    d = x_ref.shape[1]
    half = d // 2
    n_slabs = half // LANES
    inner = kv_ref.shape[1] // 2
    dh = inner // MEM_HEADS
    y = DEEPNORM_ALPHA * x_ref[rows, :] + f_ref[rows, :].astype(F32)
    (x1,) = _ln_pieces([y], [g0_ref[...]], [b0_ref[...]])
    q = jnp.dot(x1.astype(BF16), wq_ref[...], preferred_element_type=F32) * (dh ** -0.5)
    heads = []
    for h in range(MEM_HEADS):
        qh = q[:, h * dh:(h + 1) * dh].astype(BF16)
        kh = kv_ref[:, h * dh:(h + 1) * dh]
        vh = kv_ref[:, inner + h * dh:inner + (h + 1) * dh]
        s = lax.dot_general(qh, kh, (((1,), (1,)), ((), ())), preferred_element_type=F32)
        p = jnp.exp(s - jnp.max(s, axis=-1, keepdims=True))
        l = jnp.sum(p, axis=-1, keepdims=True)
        heads.append((jnp.dot(p.astype(BF16), vh, preferred_element_type=F32) / l).astype(BF16))
    o = jnp.concatenate(heads, axis=1)
    c = jnp.dot(o, wo_ref[...], preferred_element_type=F32)
    (x2,) = _ln_pieces([DEEPNORM_ALPHA * x1 + c], [g1_ref[...]], [b1_ref[...]])
    x2_ref[rows, :] = x2
    _store_slabs(x2p_ref, 0, _pack_halves(x2[:, :half], x2[:, half:]), n_slabs=n_slabs, row0=slab_row0)

    lg = jnp.dot(x2.astype(BF16), wr_ref[...], preferred_element_type=F32) + br_ref[...]
    lane = lax.broadcasted_iota(I32, lg.shape, 1).astype(F32)
    far = float(LANES)

    def first_lane_of_max(v):
        m = jnp.max(v, axis=-1, keepdims=True)
        return m, jnp.min(jnp.where(v == m, lane, far), axis=-1, keepdims=True)

    gl = jnp.where(lane < n_groups, lg, NEG_BIG)
    gm, gsel = first_lane_of_max(gl)
    gw = 1.0 / jnp.sum(jnp.exp(gl - gm), axis=-1, keepdims=True)
    lo = n_groups + gsel * per_group
    el = jnp.where((lane >= lo) & (lane < lo + per_group), lg, NEG_BIG)
    m1, i1 = first_lane_of_max(el)
    m2, i2 = first_lane_of_max(jnp.where(lane == i1, NEG_BIG, el))
    e2 = jnp.exp(m2 - m1)
    den = 1.0 + e2
    id1 = i1 - n_groups
    id2 = i2 - n_groups

    hot1 = lane == id1
    hot2 = lane == id2
    hot = jnp.where(hot1 | hot2, 1.0, 0.0)
    tm = lg.shape[0]
    strict_lower = (lax.broadcasted_iota(I32, (tm, tm), 0) > lax.broadcasted_iota(I32, (tm, tm), 1))
    before = jnp.dot(jnp.where(strict_lower, 1.0, 0.0).astype(BF16), hot.astype(BF16),
                     preferred_element_type=F32) + counts_ref[...]
    rank1 = jnp.sum(jnp.where(hot1, before, 0.0), axis=-1, keepdims=True)
    rank2 = jnp.sum(jnp.where(hot2, before, 0.0), axis=-1, keepdims=True)
    counts_ref[...] = counts_ref[...] + jnp.sum(hot, axis=0, keepdims=True)

    route = jnp.where(lane == 0, id1,
                      jnp.where(lane == 1, id2,
                                jnp.where(lane == 2, gw / den,
                                          jnp.where(lane == 3, gw * e2 / den,
                                                    jnp.where(lane == 4, rank1,
                                                              jnp.where(lane == 5, rank2, 0.0))))))
    route_ref[rows, :] = route


def _post_block(x, f, ln_g, ln_b, w_q, kv, w_o, w_r, b_r, n_groups, per_group, *, tm=ROW_TILE):
    t, d = x.shape
    inner = w_q.shape[1]
    n_mem = kv.shape[0]
    n_slabs = d // 2 // LANES
    row = lambda a: a.reshape(1, d).astype(F32)
    full = lambda shape: pl.BlockSpec(shape, lambda i: (0,) * len(shape))
    tile = lambda w: pl.BlockSpec((tm, w), lambda i: (i, 0))
    return pl.pallas_call(
        functools.partial(_post_kernel, n_groups=n_groups, per_group=per_group),
        grid=(t // tm,),
        in_specs=[tile(d), tile(d), full((1, d)), full((1, d)), full((1, d)), full((1, d)),
                  full((d, inner)), full((n_mem, 2 * inner)), full((inner, d)), full((d, LANES)), full((1, LANES))],
        out_specs=[tile(d), pl.BlockSpec((tm * n_slabs, LANES), lambda i: (i, 0)), tile(LANES), full((1, LANES))],
        out_shape=[jax.ShapeDtypeStruct((t, d), F32), jax.ShapeDtypeStruct((t * n_slabs, LANES), U32),
                   jax.ShapeDtypeStruct((t, LANES), F32), jax.ShapeDtypeStruct((1, LANES), F32)],
        compiler_params=_params(1),
    )(x, f, row(ln_g[0]), row(ln_b[0]), row(ln_g[1]), row(ln_b[1]), w_q, kv, w_o, w_r, b_r)


MOE_CAST_ROWS = 128
MOE_WEIGHT_COPIES = 8
WEIGHT_DMA_PRIORITY = 1


def _moe_kernel(be_ref, first_ref, next_ref, src_ref, nu_ref, xg_hbm, wg_hbm, wu_hbm, wd_hbm, yb_ref,
                xbuf0, xbuf1, stage_g, stage_u, stage_d, wgu_ref, wd_ref, part_ref, sem, wsem, *, blk, layer):
    i = pl.program_id(0)
    n_used = nu_ref[0]
    bufs = (xbuf0, xbuf1)
    n_slabs = yb_ref.shape[0] // blk
    half = n_slabs * LANES
    de = wd_ref.shape[0]
    d = wd_ref.shape[1]

    def weight_copies(e):
        copies = []
        for k, (w_hbm, stage) in enumerate(((wg_hbm, stage_g), (wu_hbm, stage_u), (wd_hbm, stage_d))):
            rows = stage.shape[0] // MOE_WEIGHT_COPIES
            for c in range(MOE_WEIGHT_COPIES):
                piece = pl.ds(c * rows, rows)
                copies.append(pltpu.make_async_copy(w_hbm.at[layer, e, piece, :], stage.at[piece, :], wsem.at[k]))
        return copies

    @pl.when(i == 0)
    def _():
        for c in weight_copies(be_ref[0]):
            c.start(priority=WEIGHT_DMA_PRIORITY)

    @pl.when((i < n_used) & (first_ref[i] == 1))
    def _():
        for c in weight_copies(be_ref[i]):
            c.wait()

        def cast_up(r, carry):
            rows = pl.ds(pl.multiple_of(r * MOE_CAST_ROWS, MOE_CAST_ROWS), MOE_CAST_ROWS)
            wgu_ref[rows, 0:de] = stage_g[rows, :].astype(BF16)
            wgu_ref[rows, de:2 * de] = stage_u[rows, :].astype(BF16)
            return carry

        def cast_down(r, carry):
            rows = pl.ds(pl.multiple_of(r * MOE_CAST_ROWS, MOE_CAST_ROWS), MOE_CAST_ROWS)
            for c0 in range(0, d, 1024):
                wd_ref[rows, c0:c0 + 1024] = stage_d[rows, c0:c0 + 1024].astype(BF16)
            return carry

        lax.fori_loop(0, d // MOE_CAST_ROWS, cast_up, 0)
        lax.fori_loop(0, de // MOE_CAST_ROWS, cast_down, 0)

        @pl.when(next_ref[i] >= 0)
        def _():
            for c in weight_copies(next_ref[i]):
                c.start(priority=WEIGHT_DMA_PRIORITY)

    def row_copy(j, r, s):
        src = pl.multiple_of(src_ref[j * blk + r], n_slabs)
        return pltpu.make_async_copy(xg_hbm.at[pl.ds(src, n_slabs), :],
                                     bufs[s].at[pl.ds(r * GATHER_PITCH, n_slabs), :], sem.at[s])

    def wait_block(s):
        pltpu.make_async_copy(xg_hbm.at[pl.ds(0, blk * n_slabs), :], bufs[s].at[pl.ds(0, blk * n_slabs), :],
                              sem.at[s]).wait()

    @pl.when(i == 0)
    def _():
        def body(r, carry):
            src = pl.multiple_of(src_ref[r], n_slabs)
            dst = pl.multiple_of(r * GATHER_PITCH, 8)
            pltpu.make_async_copy(xg_hbm.at[pl.ds(src, n_slabs), :], xbuf0.at[pl.ds(dst, n_slabs), :],
                                  sem.at[0]).start()
            return carry
        lax.fori_loop(0, blk, body, 0)

    def expert_block(s):
        wait_block(s)
        los, his = [], []
        for q in range(n_slabs):
            lo_q, hi_q = _unpack_halves(_load_slab_column(bufs[s], q, blk))
            los.append(lo_q.astype(BF16))
            his.append(hi_q.astype(BF16))
        lo = jnp.concatenate(los, axis=1)
        hi = jnp.concatenate(his, axis=1)
        for r in range(blk):
            row_copy(i + 1, r, 1 - s).start()
        wu = 2 * de // MOE_UP_PIECES
        for c in range(MOE_UP_PIECES):
            cols = slice(c * wu, (c + 1) * wu)
            part_ref[:, cols] = jnp.dot(lo, wgu_ref[0:half, cols], preferred_element_type=F32)
        au = []
        for c in range(MOE_UP_PIECES):
            cols = slice(c * wu, (c + 1) * wu)
            au.append(part_ref[:, cols] + jnp.dot(hi, wgu_ref[half:, cols], preferred_element_type=F32))
        au = jnp.concatenate(au, axis=1)
        a = au[:, 0:de]
        u = au[:, de:]
        hmid = (a * jax.nn.sigmoid(a) * u).astype(BF16)
        wdn = half // MOE_DOWN_PIECES
        for c in range(MOE_DOWN_PIECES):
            y_lo = jnp.dot(hmid, wd_ref[:, c * wdn:(c + 1) * wdn], preferred_element_type=F32)
            y_hi = jnp.dot(hmid, wd_ref[:, half + c * wdn:half + (c + 1) * wdn], preferred_element_type=F32)
            _store_slabs(yb_ref, c * (wdn // LANES), _pack_halves(y_lo, y_hi))

    for s in range(2):
        pl.when((i < n_used) & (i % 2 == s))(functools.partial(expert_block, s))
        pl.when((i == n_used) & (i % 2 == s))(functools.partial(wait_block, s))

    @pl.when(i >= n_used)
    def _():
        yb_ref[...] = jnp.zeros(yb_ref.shape, yb_ref.dtype)


def _moe_experts(xg, plan, w_gate, w_up, w_down, layer, *, blk=MOE_BLOCK):
    n_slots = plan.slot_src.shape[0]
    nb = n_slots // blk
    de, d = w_down.shape[2:]
    n_slabs = d // 2 // LANES
    assert n_slabs <= GATHER_PITCH and GATHER_PITCH % 8 == 0
    assert d % MOE_CAST_ROWS == 0 and de % MOE_CAST_ROWS == 0 and d % 1024 == 0
    buf = pltpu.VMEM((blk * GATHER_PITCH, LANES), U32)
    any_spec = pl.BlockSpec(memory_space=pl.ANY)
    grid_spec = pltpu.PrefetchScalarGridSpec(
        num_scalar_prefetch=5,
        grid=(nb,),
        in_specs=[any_spec, any_spec, any_spec, any_spec],
        out_specs=pl.BlockSpec((blk * n_slabs, LANES), lambda i, *_: (i, 0)),
        scratch_shapes=[buf, buf,
                        pltpu.VMEM((d, de), F32), pltpu.VMEM((d, de), F32), pltpu.VMEM((de, d), F32),
                        pltpu.VMEM((d, 2 * de), BF16), pltpu.VMEM((de, d), BF16), pltpu.VMEM((blk, 2 * de), F32),
                        pltpu.SemaphoreType.DMA((2,)), pltpu.SemaphoreType.DMA((3,))],
    )
    return pl.pallas_call(
        functools.partial(_moe_kernel, blk=blk, layer=layer),
        grid_spec=grid_spec,
        out_shape=jax.ShapeDtypeStruct((n_slots * n_slabs, LANES), U32),
        compiler_params=_params(1),
    )(plan.blk_expert, plan.seg_first, plan.seg_next, plan.slot_src, plan.n_used, xg, w_gate, w_up, w_down)


def _combine_kernel(src_ref, x2_ref, route_ref, g_ref, b_ref, yb_hbm, x3_ref, x3b_ref, ybuf0, ybuf1, sem, *, tm):
    i = pl.program_id(0)
    n = pl.num_programs(0)
    bufs = (ybuf0, ybuf1)
    d = x2_ref.shape[1]
    half = d // 2
    n_slabs = half // LANES

    def wait_tile(s):
        for k in range(TOP_K):
            pltpu.make_async_copy(yb_hbm.at[pl.ds(0, tm * n_slabs), :], bufs[s].at[k, pl.ds(0, tm * n_slabs), :],
                                  sem.at[s]).wait()

    @pl.when(i == 0)
    def _():
        def body(r, carry):
            dst = pl.multiple_of(r * GATHER_PITCH, 8)
            for k in range(TOP_K):
                src = pl.multiple_of(src_ref[TOP_K * r + k], n_slabs)
                pltpu.make_async_copy(yb_hbm.at[pl.ds(src, n_slabs), :], ybuf0.at[k, pl.ds(dst, n_slabs), :],
                                      sem.at[0]).start()
            return carry

        lax.fori_loop(0, tm, body, 0)

    def tile(s):
        wait_tile(s)
        gates = [route_ref[:, TOP_K + k:TOP_K + k + 1] for k in range(TOP_K)]
        y_lo, y_hi = [], []
        for q in range(n_slabs):
            lo = hi = None
            for k in range(TOP_K):
                lo_k, hi_k = _unpack_halves(bufs[s][k, pl.ds(q, tm, stride=GATHER_PITCH), :])
                lo = lo_k * gates[k] if lo is None else lo + lo_k * gates[k]
                hi = hi_k * gates[k] if hi is None else hi + hi_k * gates[k]
            y_lo.append(DEEPNORM_ALPHA * x2_ref[:, q * LANES:(q + 1) * LANES] + lo)
            y_hi.append(DEEPNORM_ALPHA * x2_ref[:, half + q * LANES:half + (q + 1) * LANES] + hi)
        base = (i + 1) * tm * TOP_K
        for r in range(tm):
            for k in range(TOP_K):
                src = pl.multiple_of(src_ref[base + TOP_K * r + k], n_slabs)
                pltpu.make_async_copy(yb_hbm.at[pl.ds(src, n_slabs), :],
                                      bufs[1 - s].at[k, pl.ds(r * GATHER_PITCH, n_slabs), :],
                                      sem.at[1 - s]).start(priority=k % 2)
        cols = [slice(c * LANES, (c + 1) * LANES) for c in range(2 * n_slabs)]
        out = _ln_pieces(y_lo + y_hi, [g_ref[:, c] for c in cols], [b_ref[:, c] for c in cols])
        for c, o in zip(cols, out):
            x3_ref[:, c] = o
            x3b_ref[:, c] = o.astype(BF16)

    for s in range(2):
        pl.when(i % 2 == s)(functools.partial(tile, s))
        pl.when((i == n - 1) & (i % 2 == s))(functools.partial(wait_tile, 1 - s))


def _moe_combine(x2, route, yb, slot_of, ln_g, ln_b, *, tm=ROW_TILE):
    t, d = x2.shape
    half = d // 2
    row = lambda a: a.reshape(1, d).astype(F32)
    grid_spec = pltpu.PrefetchScalarGridSpec(
        num_scalar_prefetch=1,
        grid=(t // tm,),
        in_specs=[pl.BlockSpec((tm, d), lambda i, dst: (i, 0)),
                  pl.BlockSpec((tm, LANES), lambda i, dst: (i, 0)),
                  pl.BlockSpec((1, d), lambda i, dst: (0, 0)),
                  pl.BlockSpec((1, d), lambda i, dst: (0, 0)),
                  pl.BlockSpec(memory_space=pl.ANY)],
        out_specs=[pl.BlockSpec((tm, d), lambda i, dst: (i, 0)), pl.BlockSpec((tm, d), lambda i, dst: (i, 0))],
        scratch_shapes=[pltpu.VMEM((TOP_K, tm * GATHER_PITCH, LANES), U32),
                        pltpu.VMEM((TOP_K, tm * GATHER_PITCH, LANES), U32), pltpu.SemaphoreType.DMA((2,))],
    )
    return pl.pallas_call(
        functools.partial(_combine_kernel, tm=tm),
        grid_spec=grid_spec,
        out_shape=[jax.ShapeDtypeStruct((t, d), F32), jax.ShapeDtypeStruct((t, d), BF16)],
        compiler_params=_params(1),
    )(jnp.concatenate([slot_of, jnp.zeros((tm * TOP_K,), I32)]), x2, route, row(ln_g), row(ln_b), yb)


class DispatchPlan(NamedTuple):
    slot_of: jax.Array
    slot_src: jax.Array
    blk_expert: jax.Array
    seg_first: jax.Array
    seg_next: jax.Array
    n_used: jax.Array


def _dispatch_plan(route, counts, n_experts, blk, n_slabs):
    t = route.shape[0]
    ids = route[:, 0:TOP_K].astype(I32)
    rank = route[:, 2 * TOP_K:3 * TOP_K].astype(I32)
    counts = counts[0, :n_experts].astype(I32)
    padded = (counts + blk - 1) // blk * blk
    ends = jnp.cumsum(padded)
    starts = ends - padded
    expert_iota = jnp.arange(n_experts, dtype=I32)
    seg_start = jnp.sum(jnp.where(ids[:, :, None] == expert_iota, starts, 0), axis=-1)
    nb = (t * TOP_K) // blk + n_experts
    dest = jnp.clip(seg_start + rank, 0, nb * blk - 1).reshape(-1)
    flat_src = jnp.arange(t * TOP_K, dtype=I32) // TOP_K * n_slabs
    slot_src = jnp.zeros((nb * blk,), I32).at[dest].set(flat_src, unique_indices=True)
    blk_start = jnp.arange(nb, dtype=I32) * blk
    blk_expert = jnp.minimum(jnp.sum((ends[None, :] <= blk_start[:, None]).astype(I32), axis=1), n_experts - 1)
    n_used = ends[-1] // blk
    blk_iota = jnp.arange(nb, dtype=I32)
    used = blk_iota < n_used
    prev_expert = jnp.concatenate([jnp.full((1,), -1, I32), blk_expert[:-1]])
    seg_first = (used & (blk_expert != prev_expert)).astype(I32)
    later_segment = (blk_iota[None, :] > blk_iota[:, None]) & used[None, :] & (blk_expert[None, :] != blk_expert[:, None])
    seg_next = jnp.where(jnp.any(later_segment, axis=1), blk_expert[jnp.argmax(later_segment, axis=1)], -1).astype(I32)
    return DispatchPlan(dest * n_slabs, slot_src, blk_expert.astype(I32), seg_first, seg_next,
                        n_used.astype(I32).reshape(1))


def kernel(x, mem, attn_w_qkv, attn_w_o, attn_rel_bias, conv_w_in, conv_b_in, conv_w_dw, conv_b_dw, conv_ln_g,
           conv_ln_b, conv_w_out, conv_b_out, pool_w, pool_scale, mem_w_q, mem_w_kv, mem_w_o, moe_w_group,
           moe_b_group, moe_w_router, moe_b_router, moe_w_gate, moe_w_up, moe_w_down, ln_g, ln_b):
    batch, seq, d = x.shape
    assert batch == 1 and ln_g.shape[0] == DEPTH
    n_groups = moe_w_group.shape[2]
    per_group = moe_w_router.shape[3]
    n_experts = n_groups * per_group
    assert n_groups + n_experts <= LANES

    xf = x.reshape(seq, d)
    xb = xf.astype(BF16)
    memb = mem.reshape(mem.shape[1], d).astype(BF16)

    for i in range(DEPTH):
        kind = i % N_MIXERS
        j = i // N_MIXERS
        if kind == 0:
            qkv = _matmul(xb, attn_w_qkv, j, tn=ATTN_HEADS_PER_STEP * (d // ATTN_HEADS), column_blocked=True)
            o = _chunked_attention(qkv, attn_rel_bias[j], d)
            f = _matmul(o, attn_w_o, j)
        elif kind == 1:
            u = _glu_matmul(xb, conv_w_in, j, conv_b_in[j])
            v = _conv_ln_swish(u, conv_w_dw[j], conv_b_dw[j], conv_ln_g[j], conv_ln_b[j])
            f = _matmul(v, conv_w_out, j, conv_b_out[j])
        else:
            f = _multiscale_pool(xf, pool_w[j].astype(BF16), pool_scale[j])

        kv = _matmul(memb, mem_w_kv, i, tm=memb.shape[0])
        w_r = jnp.concatenate([moe_w_group[i], moe_w_router[i].transpose(1, 0, 2).reshape(d, n_experts)], axis=1)
        w_r = jnp.pad(w_r, ((0, 0), (0, LANES - w_r.shape[1]))).astype(BF16)
        b_r = jnp.concatenate([moe_b_group[i], moe_b_router[i].reshape(n_experts)])
        b_r = jnp.pad(b_r, (0, LANES - b_r.shape[0])).reshape(1, LANES).astype(F32)
        x2, x2p, route, counts = _post_block(xf, f, ln_g[i], ln_b[i], mem_w_q[i].astype(BF16), kv,
                                             mem_w_o[i].astype(BF16), w_r, b_r, n_groups, per_group)

        plan = _dispatch_plan(route, counts, n_experts, MOE_BLOCK, d // 2 // LANES)
        yb = _moe_experts(x2p, plan, moe_w_gate, moe_w_up, moe_w_down, i)
        xf, xb = _moe_combine(x2, route, yb, plan.slot_of, ln_g[i, 2], ln_b[i, 2])

    return xf.reshape(batch, seq, d)
```

```python
import functools
from typing import NamedTuple

import jax
import jax.numpy as jnp
from jax import lax
from jax.experimental import pallas as pl
from jax.experimental.pallas import tpu as pltpu

F32 = jnp.float32
BF16 = jnp.bfloat16
U32 = jnp.uint32
I32 = jnp.int32

DEPTH = 4
N_MIXERS = 3
ATTN_HEADS = 32
CHUNK = 64
LEFT_CHUNKS = 8
POOL_WINDOWS = (2, 4, 8, 16)
MEM_HEADS = 4
TOP_K = 2
LN_EPS = 1e-5
DEEPNORM_ALPHA = (2 * DEPTH) ** 0.25
NEG_BIG = -1e30
LOG2_E = 1.4426950408889634

V7X_VMEM_BYTES = 64 * 1024 * 1024
VMEM_LIMIT = V7X_VMEM_BYTES - 8 * 1024 * 1024
LANES = 128

ATTN_Q_TILE = 4 * CHUNK
ATTN_HEADS_PER_STEP = 8
MOE_BLOCK = 256
MOE_UP_PIECES = 3
MOE_DOWN_PIECES = 8
ROW_TILE = 256
POST_ROW_CHUNKS = 1


def _params(n_grid_dims):
    return pltpu.CompilerParams(dimension_semantics=("arbitrary",) * n_grid_dims,
                                vmem_limit_bytes=VMEM_LIMIT)


def _ln_pieces(pieces, g_pieces, b_pieces):
    d = sum(p.shape[-1] for p in pieces)
    mu = sum(jnp.sum(p, axis=-1, keepdims=True) for p in pieces) * (1.0 / d)
    cen = [p - mu for p in pieces]
    var = sum(jnp.sum(c * c, axis=-1, keepdims=True) for c in cen) * (1.0 / d)
    inv = lax.rsqrt(var + LN_EPS)
    return [c * inv * g + b for c, g, b in zip(cen, g_pieces, b_pieces)]


def _pack_halves(lo, hi):
    half_ulp = jnp.uint32(0x8000)
    lo_bits = (lax.bitcast_convert_type(lo, U32) + half_ulp) >> 16
    hi_bits = (lax.bitcast_convert_type(hi, U32) + half_ulp) & jnp.uint32(0xFFFF0000)
    return lo_bits | hi_bits


def _unpack_halves(w):
    lo = lax.bitcast_convert_type(w << 16, F32)
    hi = lax.bitcast_convert_type(w & jnp.uint32(0xFFFF0000), F32)
    return lo, hi


GATHER_PITCH = 24


def _store_slabs(ref, first_slab, packed, n_slabs=None, row0=0):
    rows = packed.shape[0]
    if n_slabs is None:
        n_slabs = ref.shape[0] // rows
    for q in range(packed.shape[1] // LANES):
        ref[pl.ds(row0 + first_slab + q, rows, stride=n_slabs), :] = packed[:, q * LANES:(q + 1) * LANES]


def _load_slab_column(ref_2d, s, rows):
    return ref_2d[pl.ds(s, rows, stride=GATHER_PITCH), :]


MM_CAST_ROWS = 512


def _cast_weight_tile(w_ref, w_bf16_ref):
    k = w_ref.shape[0]
    step = min(MM_CAST_ROWS, k)
    for r in range(0, k, step):
        w_bf16_ref[r:r + step, :] = w_ref[r:r + step, :].astype(BF16)


def _mm_kernel(x_ref, w_ref, *rest, has_bias):
    if has_bias:
        b_ref, o_ref, w_bf16_ref = rest
    else:
        o_ref, w_bf16_ref = rest

    @pl.when(pl.program_id(1) == 0)
    def _():
        _cast_weight_tile(w_ref, w_bf16_ref)

    acc = jnp.dot(x_ref[...], w_bf16_ref[...], preferred_element_type=F32)
    if has_bias:
        acc = acc + b_ref[...]
    o_ref[...] = acc.astype(o_ref.dtype)


def _matmul(x, w_stack, layer, bias=None, *, out_dtype=BF16, tm=512, tn=1024, column_blocked=False):
    m, k = x.shape
    n = w_stack.shape[2]
    tm = min(tm, m)
    tn = min(tn, n)
    assert m % tm == 0 and n % tn == 0 and w_stack.shape[1] == k
    in_specs = [pl.BlockSpec((tm, k), lambda j, i: (i, 0)),
                pl.BlockSpec((None, k, tn), lambda j, i: (layer, 0, j))]
    args = [x, w_stack]
    if bias is not None:
        in_specs.append(pl.BlockSpec((1, tn), lambda j, i: (0, j)))
        args.append(bias.reshape(1, n).astype(F32))
    return pl.pallas_call(
        functools.partial(_mm_kernel, has_bias=bias is not None),
        grid=(n // tn, m // tm),
        in_specs=in_specs,
        out_specs=(pl.BlockSpec((None, tm, tn), lambda j, i: (j, i, 0)) if column_blocked
                   else pl.BlockSpec((tm, tn), lambda j, i: (i, j))),
        out_shape=jax.ShapeDtypeStruct((n // tn, m, tn) if column_blocked else (m, n), out_dtype),
        scratch_shapes=[pltpu.VMEM((k, tn), BF16)],
        compiler_params=_params(2),
    )(*args)


def _attn_kernel(q_ref, k0_ref, k1_ref, k2_ref, v0_ref, v1_ref, v2_ref, u_ref, o_ref, bias_ref, *, hb, dh, qt):
    i = pl.program_id(1)

    @pl.when(i == 0)
    def _():
        shift = CHUNK.bit_length() - 1
        qc = lax.shift_right_logical(lax.broadcasted_iota(I32, (qt, 3 * qt), 0), shift)
        kc = lax.shift_right_logical(lax.broadcasted_iota(I32, (qt, 3 * qt), 1), shift)
        band = (kc >= qc) & (kc <= qc + LEFT_CHUNKS)
        for h in range(hb):
            rows = jnp.broadcast_to(u_ref[h:h + 1, :], (qt, u_ref.shape[1]))
            toep = pltpu.roll(rows, 0, 1, stride=1, stride_axis=0)
            bias_ref[h] = jnp.where(band, toep[:, 0:3 * qt] * LOG2_E, NEG_BIG)

    scale = dh ** -0.5 * LOG2_E
    k_refs = (k0_ref, k1_ref, k2_ref)
    v_refs = (v0_ref, v1_ref, v2_ref)

    def heads(at_sequence_start):
        pens = [jnp.where(i + j >= 2, 0.0, NEG_BIG).astype(F32) if at_sequence_start else None for j in range(2)]
        pens.append(None)
        for h in range(hb):
            sl = slice(h * dh, (h + 1) * dh)
            q = (q_ref[:, sl].astype(F32) * scale).astype(BF16)
            s = []
            for j in range(3):
                sj = lax.dot_general(q, k_refs[j][:, sl], (((1,), (1,)), ((), ())), preferred_element_type=F32)
                sj = sj + bias_ref[h, :, j * qt:(j + 1) * qt]
                if pens[j] is not None:
                    sj = sj + pens[j]
                s.append(sj)
            m = jnp.maximum(jnp.maximum(jnp.max(s[0], axis=-1, keepdims=True),
                                        jnp.max(s[1], axis=-1, keepdims=True)),
                            jnp.max(s[2], axis=-1, keepdims=True))
            p = [jnp.exp2(sj - m) for sj in s]
            l = sum(jnp.sum(pj, axis=-1, keepdims=True) for pj in p)
            o = sum(jnp.dot(p[j].astype(BF16), v_refs[j][:, sl], preferred_element_type=F32) for j in range(3))
            o_ref[:, sl] = (o / l).astype(o_ref.dtype)

    pl.when(i < 2)(functools.partial(heads, True))
    pl.when(i >= 2)(functools.partial(heads, False))


def _attn_offset_bias(rel_table, qt):
    max_rel = (rel_table.shape[1] - 1) // 2
    kw = 3 * qt
    period = 4 * qt
    m = jnp.arange(period)
    off = jnp.where(m < kw, m, m - period)
    idx = jnp.clip(2 * qt - off, -max_rel, max_rel) + max_rel
    return jnp.take(rel_table.astype(F32), idx, axis=1)


def _chunked_attention(qkv, rel_table, d):
    t = qkv.shape[1]
    qt = ATTN_Q_TILE
    hb = ATTN_HEADS_PER_STEP
    dh = d // ATTN_HEADS
    assert LEFT_CHUNKS * CHUNK == 2 * qt and t % qt == 0 and ATTN_HEADS % hb == 0
    w = hb * dh
    ng = d // w
    assert qkv.shape == (3 * ng, t, w)
    u = _attn_offset_bias(rel_table, qt)

    def kv_spec(j, base):
        return pl.BlockSpec((None, qt, w), lambda g, i: (base + g, jnp.maximum(i - 2 + j, 0), 0))

    return pl.pallas_call(
        functools.partial(_attn_kernel, hb=hb, dh=dh, qt=qt),
        grid=(ng, t // qt),
        in_specs=[pl.BlockSpec((None, qt, w), lambda g, i: (g, i, 0))]
                 + [kv_spec(j, ng) for j in range(3)]
                 + [kv_spec(j, 2 * ng) for j in range(3)]
                 + [pl.BlockSpec((hb, u.shape[1]), lambda g, i: (g, 0))],
        out_specs=pl.BlockSpec((qt, w), lambda g, i: (i, g)),
        out_shape=jax.ShapeDtypeStruct((t, d), BF16),
        scratch_shapes=[pltpu.VMEM((hb, qt, 3 * qt), F32)],
        compiler_params=_params(2),
    )(qkv, qkv, qkv, qkv, qkv, qkv, qkv, u)


def _glu_kernel(x_ref, wa_ref, wg_ref, ba_ref, bg_ref, o_ref, wa_bf16_ref, wg_bf16_ref):
    @pl.when(pl.program_id(1) == 0)
    def _():
        _cast_weight_tile(wa_ref, wa_bf16_ref)
        _cast_weight_tile(wg_ref, wg_bf16_ref)

    x = x_ref[...]
    a = jnp.dot(x, wa_bf16_ref[...], preferred_element_type=F32) + ba_ref[...]
    g = jnp.dot(x, wg_bf16_ref[...], preferred_element_type=F32) + bg_ref[...]
    o_ref[...] = (a * jax.nn.sigmoid(g)).astype(o_ref.dtype)


def _glu_matmul(x, w_stack, layer, b, *, tm=512, tn=512):
    m, k = x.shape
    n = w_stack.shape[2] // 2
    assert m % tm == 0 and n % tn == 0
    nb = n // tn
    b2 = b.reshape(1, 2 * n).astype(F32)
    return pl.pallas_call(
        _glu_kernel,
        grid=(nb, m // tm),
        in_specs=[pl.BlockSpec((tm, k), lambda j, i: (i, 0)),
                  pl.BlockSpec((None, k, tn), lambda j, i: (layer, 0, j)),
                  pl.BlockSpec((None, k, tn), lambda j, i: (layer, 0, nb + j)),
                  pl.BlockSpec((1, tn), lambda j, i: (0, j)),
                  pl.BlockSpec((1, tn), lambda j, i: (0, nb + j))],
        out_specs=pl.BlockSpec((tm, tn), lambda j, i: (i, j)),
        out_shape=jax.ShapeDtypeStruct((m, n), BF16),
        scratch_shapes=[pltpu.VMEM((k, tn), BF16), pltpu.VMEM((k, tn), BF16)],
        compiler_params=_params(2),
    )(x, w_stack, w_stack, b2, b2)


CONV_HALO = 32
CONV_ROWS = 64
CONV_COLS = 128


def _conv_kernel(u_ref, halo_ref, w_ref, bdw_ref, g_ref, b_ref, o_ref, ext_ref, acc_ref, shift_ref, *, width, tm):
    i = pl.program_id(0)
    d = u_ref.shape[1]
    halo = halo_ref[...].astype(F32)
    ext_ref[0:CONV_HALO, :] = jnp.where(i > 0, halo, 0.0)
    ext_ref[CONV_HALO:, :] = u_ref[...].astype(F32)
    off = CONV_HALO - (width - 1)

    def col_chunk(c, carry):
        c0 = pl.multiple_of(c * CONV_COLS, CONV_COLS)
        w = w_ref[:, pl.ds(c0, CONV_COLS)]
        for b in range(8):
            rows = tm + (width - 1 - b) // 8 * 8
            shift_ref[b, 0:rows, :] = ext_ref[pl.ds(off + b, rows), pl.ds(c0, CONV_COLS)]
        for r0 in range(0, tm, CONV_ROWS):
            acc = jnp.zeros((CONV_ROWS, CONV_COLS), F32)
            for k in range(width):
                a8 = k - k % 8
                acc = acc + shift_ref[k % 8, r0 + a8:r0 + a8 + CONV_ROWS, :] * w[k:k + 1, :]
            acc_ref[r0:r0 + CONV_ROWS, pl.ds(c0, CONV_COLS)] = acc + bdw_ref[:, pl.ds(c0, CONV_COLS)]
        return carry

    lax.fori_loop(0, d // CONV_COLS, col_chunk, 0)
    (y,) = _ln_pieces([acc_ref[...]], [g_ref[...]], [b_ref[...]])
    o_ref[...] = (y * jax.nn.sigmoid(y)).astype(o_ref.dtype)


def _conv_ln_swish(u, w_dw, b_dw, ln_g, ln_b, *, tm=ROW_TILE):
    t, d = u.shape
    width = w_dw.shape[0]
    assert width - 1 <= CONV_HALO and t % tm == 0 and tm % CONV_HALO == 0 and d % CONV_COLS == 0
    hb = tm // CONV_HALO
    row = lambda a: a.reshape(1, d).astype(F32)
    return pl.pallas_call(
        functools.partial(_conv_kernel, width=width, tm=tm),
        grid=(t // tm,),
        in_specs=[pl.BlockSpec((tm, d), lambda i: (i, 0)),
                  pl.BlockSpec((CONV_HALO, d), lambda i: (jnp.maximum(i * hb - 1, 0), 0)),
                  pl.BlockSpec((width, d), lambda i: (0, 0)),
                  pl.BlockSpec((1, d), lambda i: (0, 0)),
                  pl.BlockSpec((1, d), lambda i: (0, 0)),
                  pl.BlockSpec((1, d), lambda i: (0, 0))],
        out_specs=pl.BlockSpec((tm, d), lambda i: (i, 0)),
        out_shape=jax.ShapeDtypeStruct((t, d), BF16),
        scratch_shapes=[pltpu.VMEM((tm + CONV_HALO, d), F32), pltpu.VMEM((tm, d), F32),
                        pltpu.VMEM((8, tm + (width - 1) // 8 * 8, CONV_COLS), F32)],
        compiler_params=_params(1),
    )(u, u, w_dw.astype(F32), row(b_dw), row(ln_g), row(ln_b))


POOL_HALO = 32


def _pool_kernel(x_ref, halo_ref, w_ref, scale_ref, o_ref, ext_ref, sum0_ref, sum1_ref, *, tm, gw):
    i = pl.program_id(0)
    ext_ref[0:POOL_HALO, :] = jnp.where(i > 0, halo_ref[...], 0.0)
    ext_ref[POOL_HALO:, :] = x_ref[...]
    n = tm + POOL_HALO
    t_abs = i * tm + lax.broadcasted_iota(I32, (tm, 1), 0)
    level_bufs = (sum0_ref, sum1_ref)
    for g, win in enumerate(POOL_WINDOWS):
        cols = slice(g * gw, (g + 1) * gw)
        read = lambda start, size: ext_ref[pl.ds(start, size), cols]
        for l in range(1, win.bit_length()):
            lo = 8 * l
            buf = level_bufs[l % 2]
            buf[lo:n, :] = read(lo, n - lo) + read(lo - (1 << (l - 1)), n - lo)
            read = lambda start, size, buf=buf: buf[pl.ds(start, size), :]
        acc = read(POOL_HALO, tm)
        cnt = jnp.minimum(t_abs + 1, win).astype(F32)
        pooled = acc / cnt - ext_ref[POOL_HALO:POOL_HALO + tm, cols]
        y = jnp.dot(pooled.astype(BF16), w_ref[g], preferred_element_type=F32)
        o_ref[:, cols] = (y * scale_ref[:, cols]).astype(o_ref.dtype)


def _multiscale_pool(x, w_pool, scale, *, tm=ROW_TILE):
    t, d = x.shape
    ng, gw, _ = w_pool.shape
    assert ng == len(POOL_WINDOWS) and t % tm == 0 and tm % POOL_HALO == 0
    assert all(w & (w - 1) == 0 for w in POOL_WINDOWS) and 8 * (max(POOL_WINDOWS).bit_length() - 1) <= POOL_HALO
    hb = tm // POOL_HALO
    return pl.pallas_call(
        functools.partial(_pool_kernel, tm=tm, gw=gw),
        grid=(t // tm,),
        in_specs=[pl.BlockSpec((tm, d), lambda i: (i, 0)),
                  pl.BlockSpec((POOL_HALO, d), lambda i: (jnp.maximum(i * hb - 1, 0), 0)),
                  pl.BlockSpec((ng, gw, gw), lambda i: (0, 0, 0)),
                  pl.BlockSpec((1, d), lambda i: (0, 0))],
        out_specs=pl.BlockSpec((tm, d), lambda i: (i, 0)),
        out_shape=jax.ShapeDtypeStruct((t, d), BF16),
        scratch_shapes=[pltpu.VMEM((tm + POOL_HALO, d), F32), pltpu.VMEM((tm + POOL_HALO, gw), F32),
                        pltpu.VMEM((tm + POOL_HALO, gw), F32)],
        compiler_params=_params(1),
    )(x, x, w_pool, scale.reshape(1, d).astype(F32))


def _post_kernel(x_ref, f_ref, g0_ref, b0_ref, g1_ref, b1_ref, wq_ref, kv_ref, wo_ref, wr_ref, br_ref,
                 x2_ref, x2p_ref, route_ref, counts_ref, *, n_groups, per_group):
    d = x_ref.shape[1]
    half = d // 2
    n_mem, inner2 = kv_ref.shape
    inner = inner2 // 2
    dh = inner // MEM_HEADS
    n_slabs = half // LANES

    @pl.when(pl.program_id(0) == 0)
    def _():
        counts_ref[...] = jnp.zeros(counts_ref.shape, counts_ref.dtype)

    tm = x_ref.shape[0]
    rc = tm // POST_ROW_CHUNKS
    for r0 in range(0, tm, rc):
        _post_rows(slice(r0, r0 + rc), r0 * n_slabs, x_ref, f_ref, g0_ref, b0_ref, g1_ref, b1_ref, wq_ref, kv_ref,
                   wo_ref, wr_ref, br_ref, x2_ref, x2p_ref, route_ref, counts_ref, n_groups, per_group)


def _post_rows(rows, slab_row0, x_ref, f_ref, g0_ref, b0_ref, g1_ref, b1_ref, wq_ref, kv_ref, wo_ref, wr_ref, br_ref,
               x2_ref, x2p_ref, route_ref, counts_ref, n_groups, per_group):
    d = x_ref.shape[1]
    half = d // 2
    n_slabs = half // LANES
    inner = kv_ref.shape[1] // 2
    dh = inner // MEM_HEADS
    y = DEEPNORM_ALPHA * x_ref[rows, :] + f_ref[rows, :].astype(F32)
    (x1,) = _ln_pieces([y], [g0_ref[...]], [b0_ref[...]])
    q = jnp.dot(x1.astype(BF16), wq_ref[...], preferred_element_type=F32) * (dh ** -0.5)
    heads = []
    for h in range(MEM_HEADS):
        qh = q[:, h * dh:(h + 1) * dh].astype(BF16)
        kh = kv_ref[:, h * dh:(h + 1) * dh]
        vh = kv_ref[:, inner + h * dh:inner + (h + 1) * dh]
        s = lax.dot_general(qh, kh, (((1,), (1,)), ((), ())), preferred_element_type=F32)
        p = jnp.exp(s - jnp.max(s, axis=-1, keepdims=True))
        l = jnp.sum(p, axis=-1, keepdims=True)
        heads.append((jnp.dot(p.astype(BF16), vh, preferred_element_type=F32) / l).astype(BF16))
    o = jnp.concatenate(heads, axis=1)
    c = jnp.dot(o, wo_ref[...], preferred_element_type=F32)
    (x2,) = _ln_pieces([DEEPNORM_ALPHA * x1 + c], [g1_ref[...]], [b1_ref[...]])
    x2_ref[rows, :] = x2
    _store_slabs(x2p_ref, 0, _pack_halves(x2[:, :half], x2[:, half:]), n_slabs=n_slabs, row0=slab_row0)

    lg = jnp.dot(x2.astype(BF16), wr_ref[...], preferred_element_type=F32) + br_ref[...]
    lane = lax.broadcasted_iota(I32, lg.shape, 1).astype(F32)
    far = float(LANES)

    def first_lane_of_max(v):
        m = jnp.max(v, axis=-1, keepdims=True)
        return m, jnp.min(jnp.where(v == m, lane, far), axis=-1, keepdims=True)

    gl = jnp.where(lane < n_groups, lg, NEG_BIG)
    gm, gsel = first_lane_of_max(gl)
    gw = 1.0 / jnp.sum(jnp.exp(gl - gm), axis=-1, keepdims=True)
    lo = n_groups + gsel * per_group
    el = jnp.where((lane >= lo) & (lane < lo + per_group), lg, NEG_BIG)
    m1, i1 = first_lane_of_max(el)
    m2, i2 = first_lane_of_max(jnp.where(lane == i1, NEG_BIG, el))
    e2 = jnp.exp(m2 - m1)
    den = 1.0 + e2
    id1 = i1 - n_groups
    id2 = i2 - n_groups

    hot1 = lane == id1
    hot2 = lane == id2
    hot = jnp.where(hot1 | hot2, 1.0, 0.0)
    tm = lg.shape[0]
    strict_lower = (lax.broadcasted_iota(I32, (tm, tm), 0) > lax.broadcasted_iota(I32, (tm, tm), 1))
    before = jnp.dot(jnp.where(strict_lower, 1.0, 0.0).astype(BF16), hot.astype(BF16),
                     preferred_element_type=F32) + counts_ref[...]
    rank1 = jnp.sum(jnp.where(hot1, before, 0.0), axis=-1, keepdims=True)
    rank2 = jnp.sum(jnp.where(hot2, before, 0.0), axis=-1, keepdims=True)
    counts_ref[...] = counts_ref[...] + jnp.sum(hot, axis=0, keepdims=True)

    route = jnp.where(lane == 0, id1,
                      jnp.where(lane == 1, id2,
                                jnp.where(lane == 2, gw / den,
                                          jnp.where(lane == 3, gw * e2 / den,
                                                    jnp.where(lane == 4, rank1,
                                                              jnp.where(lane == 5, rank2, 0.0))))))
    route_ref[rows, :] = route


def _post_block(x, f, ln_g, ln_b, w_q, kv, w_o, w_r, b_r, n_groups, per_group, *, tm=ROW_TILE):
    t, d = x.shape
    inner = w_q.shape[1]
    n_mem = kv.shape[0]
    n_slabs = d // 2 // LANES
    row = lambda a: a.reshape(1, d).astype(F32)
    full = lambda shape: pl.BlockSpec(shape, lambda i: (0,) * len(shape))
    tile = lambda w: pl.BlockSpec((tm, w), lambda i: (i, 0))
    return pl.pallas_call(
        functools.partial(_post_kernel, n_groups=n_groups, per_group=per_group),
        grid=(t // tm,),
        in_specs=[tile(d), tile(d), full((1, d)), full((1, d)), full((1, d)), full((1, d)),
                  full((d, inner)), full((n_mem, 2 * inner)), full((inner, d)), full((d, LANES)), full((1, LANES))],
        out_specs=[tile(d), pl.BlockSpec((tm * n_slabs, LANES), lambda i: (i, 0)), tile(LANES), full((1, LANES))],
        out_shape=[jax.ShapeDtypeStruct((t, d), F32), jax.ShapeDtypeStruct((t * n_slabs, LANES), U32),
                   jax.ShapeDtypeStruct((t, LANES), F32), jax.ShapeDtypeStruct((1, LANES), F32)],
        compiler_params=_params(1),
    )(x, f, row(ln_g[0]), row(ln_b[0]), row(ln_g[1]), row(ln_b[1]), w_q, kv, w_o, w_r, b_r)


MOE_CAST_ROWS = 128
MOE_WEIGHT_COPIES = 8
WEIGHT_DMA_PRIORITY = 1


def _moe_kernel(be_ref, first_ref, next_ref, src_ref, nu_ref, xg_hbm, wg_hbm, wu_hbm, wd_hbm, yb_ref,
                xbuf0, xbuf1, stage_g, stage_u, stage_d, wgu_ref, wd_ref, part_ref, sem, wsem, *, blk, layer):
    i = pl.program_id(0)
    n_used = nu_ref[0]
    bufs = (xbuf0, xbuf1)
    n_slabs = yb_ref.shape[0] // blk
    half = n_slabs * LANES
    de = wd_ref.shape[0]
    d = wd_ref.shape[1]

    def weight_copies(e):
        copies = []
        for k, (w_hbm, stage) in enumerate(((wg_hbm, stage_g), (wu_hbm, stage_u), (wd_hbm, stage_d))):
            rows = stage.shape[0] // MOE_WEIGHT_COPIES
            for c in range(MOE_WEIGHT_COPIES):
                piece = pl.ds(c * rows, rows)
                copies.append(pltpu.make_async_copy(w_hbm.at[layer, e, piece, :], stage.at[piece, :], wsem.at[k]))
        return copies

    @pl.when(i == 0)
    def _():
        for c in weight_copies(be_ref[0]):
            c.start(priority=WEIGHT_DMA_PRIORITY)

    @pl.when((i < n_used) & (first_ref[i] == 1))
    def _():
        for c in weight_copies(be_ref[i]):
            c.wait()

        def cast_up(r, carry):
            rows = pl.ds(pl.multiple_of(r * MOE_CAST_ROWS, MOE_CAST_ROWS), MOE_CAST_ROWS)
            wgu_ref[rows, 0:de] = stage_g[rows, :].astype(BF16)
            wgu_ref[rows, de:2 * de] = stage_u[rows, :].astype(BF16)
            return carry

        def cast_down(r, carry):
            rows = pl.ds(pl.multiple_of(r * MOE_CAST_ROWS, MOE_CAST_ROWS), MOE_CAST_ROWS)
            for c0 in range(0, d, 1024):
                wd_ref[rows, c0:c0 + 1024] = stage_d[rows, c0:c0 + 1024].astype(BF16)
            return carry

        lax.fori_loop(0, d // MOE_CAST_ROWS, cast_up, 0)
        lax.fori_loop(0, de // MOE_CAST_ROWS, cast_down, 0)

        @pl.when(next_ref[i] >= 0)
        def _():
            for c in weight_copies(next_ref[i]):
                c.start(priority=WEIGHT_DMA_PRIORITY)

    def row_copy(j, r, s):
        src = pl.multiple_of(src_ref[j * blk + r], n_slabs)
        return pltpu.make_async_copy(xg_hbm.at[pl.ds(src, n_slabs), :],
                                     bufs[s].at[pl.ds(r * GATHER_PITCH, n_slabs), :], sem.at[s])

    def wait_block(s):
        pltpu.make_async_copy(xg_hbm.at[pl.ds(0, blk * n_slabs), :], bufs[s].at[pl.ds(0, blk * n_slabs), :],
                              sem.at[s]).wait()

    @pl.when(i == 0)
    def _():
        def body(r, carry):
            src = pl.multiple_of(src_ref[r], n_slabs)
            dst = pl.multiple_of(r * GATHER_PITCH, 8)
            pltpu.make_async_copy(xg_hbm.at[pl.ds(src, n_slabs), :], xbuf0.at[pl.ds(dst, n_slabs), :],
                                  sem.at[0]).start()
            return carry
        lax.fori_loop(0, blk, body, 0)

    def expert_block(s):
        wait_block(s)
        los, his = [], []
        for q in range(n_slabs):
            lo_q, hi_q = _unpack_halves(_load_slab_column(bufs[s], q, blk))
            los.append(lo_q.astype(BF16))
            his.append(hi_q.astype(BF16))
        lo = jnp.concatenate(los, axis=1)
        hi = jnp.concatenate(his, axis=1)
        for r in range(blk):
            row_copy(i + 1, r, 1 - s).start()
        wu = 2 * de // MOE_UP_PIECES
        for c in range(MOE_UP_PIECES):
            cols = slice(c * wu, (c + 1) * wu)
            part_ref[:, cols] = jnp.dot(lo, wgu_ref[0:half, cols], preferred_element_type=F32)
        au = []
        for c in range(MOE_UP_PIECES):
            cols = slice(c * wu, (c + 1) * wu)
            au.append(part_ref[:, cols] + jnp.dot(hi, wgu_ref[half:, cols], preferred_element_type=F32))
        au = jnp.concatenate(au, axis=1)
        a = au[:, 0:de]
        u = au[:, de:]
        hmid = (a * jax.nn.sigmoid(a) * u).astype(BF16)
        wdn = half // MOE_DOWN_PIECES
        for c in range(MOE_DOWN_PIECES):
            y_lo = jnp.dot(hmid, wd_ref[:, c * wdn:(c + 1) * wdn], preferred_element_type=F32)
            y_hi = jnp.dot(hmid, wd_ref[:, half + c * wdn:half + (c + 1) * wdn], preferred_element_type=F32)
            _store_slabs(yb_ref, c * (wdn // LANES), _pack_halves(y_lo, y_hi))

    for s in range(2):
        pl.when((i < n_used) & (i % 2 == s))(functools.partial(expert_block, s))
        pl.when((i == n_used) & (i % 2 == s))(functools.partial(wait_block, s))

    @pl.when(i >= n_used)
    def _():
        yb_ref[...] = jnp.zeros(yb_ref.shape, yb_ref.dtype)


def _moe_experts(xg, plan, w_gate, w_up, w_down, layer, *, blk=MOE_BLOCK):
    n_slots = plan.slot_src.shape[0]
    nb = n_slots // blk
    de, d = w_down.shape[2:]
    n_slabs = d // 2 // LANES
    assert n_slabs <= GATHER_PITCH and GATHER_PITCH % 8 == 0
    assert d % MOE_CAST_ROWS == 0 and de % MOE_CAST_ROWS == 0 and d % 1024 == 0
    buf = pltpu.VMEM((blk * GATHER_PITCH, LANES), U32)
    any_spec = pl.BlockSpec(memory_space=pl.ANY)
    grid_spec = pltpu.PrefetchScalarGridSpec(
        num_scalar_prefetch=5,
        grid=(nb,),
        in_specs=[any_spec, any_spec, any_spec, any_spec],
        out_specs=pl.BlockSpec((blk * n_slabs, LANES), lambda i, *_: (i, 0)),
        scratch_shapes=[buf, buf,
                        pltpu.VMEM((d, de), F32), pltpu.VMEM((d, de), F32), pltpu.VMEM((de, d), F32),
                        pltpu.VMEM((d, 2 * de), BF16), pltpu.VMEM((de, d), BF16), pltpu.VMEM((blk, 2 * de), F32),
                        pltpu.SemaphoreType.DMA((2,)), pltpu.SemaphoreType.DMA((3,))],
    )
    return pl.pallas_call(
        functools.partial(_moe_kernel, blk=blk, layer=layer),
        grid_spec=grid_spec,
        out_shape=jax.ShapeDtypeStruct((n_slots * n_slabs, LANES), U32),
        compiler_params=_params(1),
    )(plan.blk_expert, plan.seg_first, plan.seg_next, plan.slot_src, plan.n_used, xg, w_gate, w_up, w_down)


def _combine_kernel(src_ref, x2_ref, route_ref, g_ref, b_ref, yb_hbm, x3_ref, *rest, tm, with_bf16):
    if with_bf16:
        x3b_ref, ybuf0, ybuf1, sem = rest
    else:
        x3b_ref = None
        ybuf0, ybuf1, sem = rest
    i = pl.program_id(0)
    n = pl.num_programs(0)
    bufs = (ybuf0, ybuf1)
    d = x2_ref.shape[1]
    half = d // 2
    n_slabs = half // LANES

    def wait_tile(s):
        for k in range(TOP_K):
            pltpu.make_async_copy(yb_hbm.at[pl.ds(0, tm * n_slabs), :], bufs[s].at[k, pl.ds(0, tm * n_slabs), :],
                                  sem.at[s]).wait()

    @pl.when(i == 0)
    def _():
        def body(r, carry):
            dst = pl.multiple_of(r * GATHER_PITCH, 8)
            for k in range(TOP_K):
                src = pl.multiple_of(src_ref[TOP_K * r + k], n_slabs)
                pltpu.make_async_copy(yb_hbm.at[pl.ds(src, n_slabs), :], ybuf0.at[k, pl.ds(dst, n_slabs), :],
                                      sem.at[0]).start()
            return carry

        lax.fori_loop(0, tm, body, 0)

    def tile(s):
        wait_tile(s)
        gates = [route_ref[:, TOP_K + k:TOP_K + k + 1] for k in range(TOP_K)]
        y_lo, y_hi = [], []
        for q in range(n_slabs):
            lo = hi = None
            for k in range(TOP_K):
                lo_k, hi_k = _unpack_halves(bufs[s][k, pl.ds(q, tm, stride=GATHER_PITCH), :])
                lo = lo_k * gates[k] if lo is None else lo + lo_k * gates[k]
                hi = hi_k * gates[k] if hi is None else hi + hi_k * gates[k]
            y_lo.append(DEEPNORM_ALPHA * x2_ref[:, q * LANES:(q + 1) * LANES] + lo)
            y_hi.append(DEEPNORM_ALPHA * x2_ref[:, half + q * LANES:half + (q + 1) * LANES] + hi)
        base = (i + 1) * tm * TOP_K
        for r in range(tm):
            for k in range(TOP_K):
                src = pl.multiple_of(src_ref[base + TOP_K * r + k], n_slabs)
                pltpu.make_async_copy(yb_hbm.at[pl.ds(src, n_slabs), :],
                                      bufs[1 - s].at[k, pl.ds(r * GATHER_PITCH, n_slabs), :],
                                      sem.at[1 - s]).start(priority=k % 2)
        cols = [slice(c * LANES, (c + 1) * LANES) for c in range(2 * n_slabs)]
        out = _ln_pieces(y_lo + y_hi, [g_ref[:, c] for c in cols], [b_ref[:, c] for c in cols])
        for c, o in zip(cols, out):
            x3_ref[:, c] = o
            if with_bf16:
                x3b_ref[:, c] = o.astype(BF16)

    for s in range(2):
        pl.when(i % 2 == s)(functools.partial(tile, s))
        pl.when((i == n - 1) & (i % 2 == s))(functools.partial(wait_tile, 1 - s))


def _moe_combine(x2, route, yb, slot_of, ln_g, ln_b, *, with_bf16, tm=ROW_TILE):
    t, d = x2.shape
    half = d // 2
    row = lambda a: a.reshape(1, d).astype(F32)
    grid_spec = pltpu.PrefetchScalarGridSpec(
        num_scalar_prefetch=1,
        grid=(t // tm,),
        in_specs=[pl.BlockSpec((tm, d), lambda i, dst: (i, 0)),
                  pl.BlockSpec((tm, LANES), lambda i, dst: (i, 0)),
                  pl.BlockSpec((1, d), lambda i, dst: (0, 0)),
                  pl.BlockSpec((1, d), lambda i, dst: (0, 0)),
                  pl.BlockSpec(memory_space=pl.ANY)],
        out_specs=[pl.BlockSpec((tm, d), lambda i, dst: (i, 0))] * (2 if with_bf16 else 1),
        scratch_shapes=[pltpu.VMEM((TOP_K, tm * GATHER_PITCH, LANES), U32),
                        pltpu.VMEM((TOP_K, tm * GATHER_PITCH, LANES), U32), pltpu.SemaphoreType.DMA((2,))],
    )
    return pl.pallas_call(
        functools.partial(_combine_kernel, tm=tm, with_bf16=with_bf16),
        grid_spec=grid_spec,
        out_shape=[jax.ShapeDtypeStruct((t, d), F32)] + ([jax.ShapeDtypeStruct((t, d), BF16)] if with_bf16 else []),
        compiler_params=_params(1),
    )(jnp.concatenate([slot_of, jnp.zeros((tm * TOP_K,), I32)]), x2, route, row(ln_g), row(ln_b), yb)


class DispatchPlan(NamedTuple):
    slot_of: jax.Array
    slot_src: jax.Array
    blk_expert: jax.Array
    seg_first: jax.Array
    seg_next: jax.Array
    n_used: jax.Array


def _dispatch_plan(route, counts, n_experts, blk, n_slabs):
    t = route.shape[0]
    ids = route[:, 0:TOP_K].astype(I32)
    rank = route[:, 2 * TOP_K:3 * TOP_K].astype(I32)
    counts = counts[0, :n_experts].astype(I32)
    padded = (counts + blk - 1) // blk * blk
    ends = jnp.cumsum(padded)
    starts = ends - padded
    expert_iota = jnp.arange(n_experts, dtype=I32)
    seg_start = jnp.sum(jnp.where(ids[:, :, None] == expert_iota, starts, 0), axis=-1)
    nb = (t * TOP_K) // blk + n_experts
    dest = jnp.clip(seg_start + rank, 0, nb * blk - 1).reshape(-1)
    flat_src = jnp.arange(t * TOP_K, dtype=I32) // TOP_K * n_slabs
    slot_src = jnp.zeros((nb * blk,), I32).at[dest].set(flat_src, unique_indices=True)
    blk_start = jnp.arange(nb, dtype=I32) * blk
    blk_expert = jnp.minimum(jnp.sum((ends[None, :] <= blk_start[:, None]).astype(I32), axis=1), n_experts - 1)
    n_used = ends[-1] // blk
    blk_iota = jnp.arange(nb, dtype=I32)
    used = blk_iota < n_used
    prev_expert = jnp.concatenate([jnp.full((1,), -1, I32), blk_expert[:-1]])
    seg_first = (used & (blk_expert != prev_expert)).astype(I32)
    later_segment = (blk_iota[None, :] > blk_iota[:, None]) & used[None, :] & (blk_expert[None, :] != blk_expert[:, None])
    seg_next = jnp.where(jnp.any(later_segment, axis=1), blk_expert[jnp.argmax(later_segment, axis=1)], -1).astype(I32)
    return DispatchPlan(dest * n_slabs, slot_src, blk_expert.astype(I32), seg_first, seg_next,
                        n_used.astype(I32).reshape(1))


def kernel(x, mem, attn_w_qkv, attn_w_o, attn_rel_bias, conv_w_in, conv_b_in, conv_w_dw, conv_b_dw, conv_ln_g,
           conv_ln_b, conv_w_out, conv_b_out, pool_w, pool_scale, mem_w_q, mem_w_kv, mem_w_o, moe_w_group,
           moe_b_group, moe_w_router, moe_b_router, moe_w_gate, moe_w_up, moe_w_down, ln_g, ln_b):
    batch, seq, d = x.shape
    assert batch == 1 and ln_g.shape[0] == DEPTH
    n_groups = moe_w_group.shape[2]
    per_group = moe_w_router.shape[3]
    n_experts = n_groups * per_group
    assert n_groups + n_experts <= LANES

    xf = x.reshape(seq, d)
    xb = xf.astype(BF16)
    memb = mem.reshape(mem.shape[1], d).astype(BF16)

    for i in range(DEPTH):
        kind = i % N_MIXERS
        j = i // N_MIXERS
        if kind == 0:
            qkv = _matmul(xb, attn_w_qkv, j, tn=ATTN_HEADS_PER_STEP * (d // ATTN_HEADS), column_blocked=True)
            o = _chunked_attention(qkv, attn_rel_bias[j], d)
            f = _matmul(o, attn_w_o, j)
        elif kind == 1:
            u = _glu_matmul(xb, conv_w_in, j, conv_b_in[j])
            v = _conv_ln_swish(u, conv_w_dw[j], conv_b_dw[j], conv_ln_g[j], conv_ln_b[j])
            f = _matmul(v, conv_w_out, j, conv_b_out[j])
        else:
            f = _multiscale_pool(xf, pool_w[j].astype(BF16), pool_scale[j])

        kv = _matmul(memb, mem_w_kv, i, tm=memb.shape[0])
        w_r = jnp.concatenate([moe_w_group[i], moe_w_router[i].transpose(1, 0, 2).reshape(d, n_experts)], axis=1)
        w_r = jnp.pad(w_r, ((0, 0), (0, LANES - w_r.shape[1]))).astype(BF16)
        b_r = jnp.concatenate([moe_b_group[i], moe_b_router[i].reshape(n_experts)])
        b_r = jnp.pad(b_r, (0, LANES - b_r.shape[0])).reshape(1, LANES).astype(F32)
        x2, x2p, route, counts = _post_block(xf, f, ln_g[i], ln_b[i], mem_w_q[i].astype(BF16), kv,
                                             mem_w_o[i].astype(BF16), w_r, b_r, n_groups, per_group)

        plan = _dispatch_plan(route, counts, n_experts, MOE_BLOCK, d // 2 // LANES)
        yb = _moe_experts(x2p, plan, moe_w_gate, moe_w_up, moe_w_down, i)
        next_wants_bf16 = i + 1 < DEPTH and (i + 1) % N_MIXERS != 2
        outs = _moe_combine(x2, route, yb, plan.slot_of, ln_g[i, 2], ln_b[i, 2], with_bf16=next_wants_bf16)
        xf, xb = (outs[0], outs[1]) if next_wants_bf16 else (outs[0], None)

    return xf.reshape(batch, seq, d)
```

```python
import functools
from typing import NamedTuple

import jax
import jax.numpy as jnp
from jax import lax
from jax.experimental import pallas as pl
from jax.experimental.pallas import tpu as pltpu

F32 = jnp.float32
BF16 = jnp.bfloat16
U32 = jnp.uint32
I32 = jnp.int32

DEPTH = 4
N_MIXERS = 3
ATTN_HEADS = 32
CHUNK = 64
LEFT_CHUNKS = 8
POOL_WINDOWS = (2, 4, 8, 16)
MEM_HEADS = 4
TOP_K = 2
LN_EPS = 1e-5
DEEPNORM_ALPHA = (2 * DEPTH) ** 0.25
NEG_BIG = -1e30
LOG2_E = 1.4426950408889634

V7X_VMEM_BYTES = 64 * 1024 * 1024
VMEM_LIMIT = V7X_VMEM_BYTES - 8 * 1024 * 1024
LANES = 128

ATTN_Q_TILE = 4 * CHUNK
ATTN_HEADS_PER_STEP = 8
MOE_BLOCK = 256
MOE_UP_PIECES = 3
MOE_DOWN_PIECES = 8
ROW_TILE = 256
POST_ROW_CHUNKS = 1


def _params(n_grid_dims):
    return pltpu.CompilerParams(dimension_semantics=("arbitrary",) * n_grid_dims,
                                vmem_limit_bytes=VMEM_LIMIT)


def _ln_pieces(pieces, g_pieces, b_pieces):
    d = sum(p.shape[-1] for p in pieces)
    mu = sum(jnp.sum(p, axis=-1, keepdims=True) for p in pieces) * (1.0 / d)
    cen = [p - mu for p in pieces]
    var = sum(jnp.sum(c * c, axis=-1, keepdims=True) for c in cen) * (1.0 / d)
    inv = lax.rsqrt(var + LN_EPS)
    return [c * inv * g + b for c, g, b in zip(cen, g_pieces, b_pieces)]


def _pack_halves(lo, hi):
    half_ulp = jnp.uint32(0x8000)
    lo_bits = (lax.bitcast_convert_type(lo, U32) + half_ulp) >> 16
    hi_bits = (lax.bitcast_convert_type(hi, U32) + half_ulp) & jnp.uint32(0xFFFF0000)
    return lo_bits | hi_bits


def _unpack_halves(w):
    lo = lax.bitcast_convert_type(w << 16, F32)
    hi = lax.bitcast_convert_type(w & jnp.uint32(0xFFFF0000), F32)
    return lo, hi


GATHER_PITCH = 24


def _store_slabs(ref, first_slab, packed, n_slabs=None, row0=0):
    rows = packed.shape[0]
    if n_slabs is None:
        n_slabs = ref.shape[0] // rows
    for q in range(packed.shape[1] // LANES):
        ref[pl.ds(row0 + first_slab + q, rows, stride=n_slabs), :] = packed[:, q * LANES:(q + 1) * LANES]


def _load_slab_column(ref_2d, s, rows):
    return ref_2d[pl.ds(s, rows, stride=GATHER_PITCH), :]


MM_CAST_ROWS = 512


def _cast_weight_tile(w_ref, w_bf16_ref):
    k = w_ref.shape[0]
    step = min(MM_CAST_ROWS, k)
    for r in range(0, k, step):
        w_bf16_ref[r:r + step, :] = w_ref[r:r + step, :].astype(BF16)


def _mm_kernel(x_ref, w_ref, *rest, has_bias):
    if has_bias:
        b_ref, o_ref, w_bf16_ref = rest
    else:
        o_ref, w_bf16_ref = rest

    @pl.when(pl.program_id(1) == 0)
    def _():
        _cast_weight_tile(w_ref, w_bf16_ref)

    acc = jnp.dot(x_ref[...], w_bf16_ref[...], preferred_element_type=F32)
    if has_bias:
        acc = acc + b_ref[...]
    o_ref[...] = acc.astype(o_ref.dtype)


def _matmul(x, w_stack, layer, bias=None, *, out_dtype=BF16, tm=512, tn=1024, column_blocked=False):
    m, k = x.shape
    n = w_stack.shape[2]
    tm = min(tm, m)
    tn = min(tn, n)
    assert m % tm == 0 and n % tn == 0 and w_stack.shape[1] == k
    in_specs = [pl.BlockSpec((tm, k), lambda j, i: (i, 0)),
                pl.BlockSpec((None, k, tn), lambda j, i: (layer, 0, j))]
    args = [x, w_stack]
    if bias is not None:
        in_specs.append(pl.BlockSpec((1, tn), lambda j, i: (0, j)))
        args.append(bias.reshape(1, n).astype(F32))
    return pl.pallas_call(
        functools.partial(_mm_kernel, has_bias=bias is not None),
        grid=(n // tn, m // tm),
        in_specs=in_specs,
        out_specs=(pl.BlockSpec((None, tm, tn), lambda j, i: (j, i, 0)) if column_blocked
                   else pl.BlockSpec((tm, tn), lambda j, i: (i, j))),
        out_shape=jax.ShapeDtypeStruct((n // tn, m, tn) if column_blocked else (m, n), out_dtype),
        scratch_shapes=[pltpu.VMEM((k, tn), BF16)],
        compiler_params=_params(2),
    )(*args)


def _attn_kernel(q_ref, k0_ref, k1_ref, k2_ref, v0_ref, v1_ref, v2_ref, u_ref, o_ref, bias_ref, *, hb, dh, qt):
    i = pl.program_id(1)

    @pl.when(i == 0)
    def _():
        shift = CHUNK.bit_length() - 1
        qc = lax.shift_right_logical(lax.broadcasted_iota(I32, (qt, 3 * qt), 0), shift)
        kc = lax.shift_right_logical(lax.broadcasted_iota(I32, (qt, 3 * qt), 1), shift)
        band = (kc >= qc) & (kc <= qc + LEFT_CHUNKS)
        for h in range(hb):
            rows = jnp.broadcast_to(u_ref[h:h + 1, :], (qt, u_ref.shape[1]))
            toep = pltpu.roll(rows, 0, 1, stride=1, stride_axis=0)
            bias_ref[h] = jnp.where(band, toep[:, 0:3 * qt] * LOG2_E, NEG_BIG)

    scale = dh ** -0.5 * LOG2_E
    k_refs = (k0_ref, k1_ref, k2_ref)
    v_refs = (v0_ref, v1_ref, v2_ref)

    def heads(at_sequence_start):
        pens = [jnp.where(i + j >= 2, 0.0, NEG_BIG).astype(F32) if at_sequence_start else None for j in range(2)]
        pens.append(None)
        for h in range(hb):
            sl = slice(h * dh, (h + 1) * dh)
            q = (q_ref[:, sl].astype(F32) * scale).astype(BF16)
            s = []
            for j in range(3):
                sj = lax.dot_general(q, k_refs[j][:, sl], (((1,), (1,)), ((), ())), preferred_element_type=F32)
                sj = sj + bias_ref[h, :, j * qt:(j + 1) * qt]
                if pens[j] is not None:
                    sj = sj + pens[j]
                s.append(sj)
            m = jnp.maximum(jnp.maximum(jnp.max(s[0], axis=-1, keepdims=True),
                                        jnp.max(s[1], axis=-1, keepdims=True)),
                            jnp.max(s[2], axis=-1, keepdims=True))
            p = [jnp.exp2(sj - m) for sj in s]
            l = sum(jnp.sum(pj, axis=-1, keepdims=True) for pj in p)
            o = sum(jnp.dot(p[j].astype(BF16), v_refs[j][:, sl], preferred_element_type=F32) for j in range(3))
            o_ref[:, sl] = (o / l).astype(o_ref.dtype)

    pl.when(i < 2)(functools.partial(heads, True))
    pl.when(i >= 2)(functools.partial(heads, False))


def _attn_offset_bias(rel_table, qt):
    max_rel = (rel_table.shape[1] - 1) // 2
    kw = 3 * qt
    period = 4 * qt
    m = jnp.arange(period)
    off = jnp.where(m < kw, m, m - period)
    idx = jnp.clip(2 * qt - off, -max_rel, max_rel) + max_rel
    return jnp.take(rel_table.astype(F32), idx, axis=1)


def _chunked_attention(qkv, rel_table, d):
    t = qkv.shape[1]
    qt = ATTN_Q_TILE
    hb = ATTN_HEADS_PER_STEP
    dh = d // ATTN_HEADS
    assert LEFT_CHUNKS * CHUNK == 2 * qt and t % qt == 0 and ATTN_HEADS % hb == 0
    w = hb * dh
    ng = d // w
    assert qkv.shape == (3 * ng, t, w)
    u = _attn_offset_bias(rel_table, qt)

    def kv_spec(j, base):
        return pl.BlockSpec((None, qt, w), lambda g, i: (base + g, jnp.maximum(i - 2 + j, 0), 0))

    return pl.pallas_call(
        functools.partial(_attn_kernel, hb=hb, dh=dh, qt=qt),
        grid=(ng, t // qt),
        in_specs=[pl.BlockSpec((None, qt, w), lambda g, i: (g, i, 0))]
                 + [kv_spec(j, ng) for j in range(3)]
                 + [kv_spec(j, 2 * ng) for j in range(3)]
                 + [pl.BlockSpec((hb, u.shape[1]), lambda g, i: (g, 0))],
        out_specs=pl.BlockSpec((qt, w), lambda g, i: (i, g)),
        out_shape=jax.ShapeDtypeStruct((t, d), BF16),
        scratch_shapes=[pltpu.VMEM((hb, qt, 3 * qt), F32)],
        compiler_params=_params(2),
    )(qkv, qkv, qkv, qkv, qkv, qkv, qkv, u)


def _glu_kernel(x_ref, wa_ref, wg_ref, ba_ref, bg_ref, o_ref, wa_bf16_ref, wg_bf16_ref):
    @pl.when(pl.program_id(1) == 0)
    def _():
        _cast_weight_tile(wa_ref, wa_bf16_ref)
        _cast_weight_tile(wg_ref, wg_bf16_ref)

    x = x_ref[...]
    a = jnp.dot(x, wa_bf16_ref[...], preferred_element_type=F32) + ba_ref[...]
    g = jnp.dot(x, wg_bf16_ref[...], preferred_element_type=F32) + bg_ref[...]
    o_ref[...] = (a * jax.nn.sigmoid(g)).astype(o_ref.dtype)


def _glu_matmul(x, w_stack, layer, b, *, tm=512, tn=512):
    m, k = x.shape
    n = w_stack.shape[2] // 2
    assert m % tm == 0 and n % tn == 0
    nb = n // tn
    b2 = b.reshape(1, 2 * n).astype(F32)
    return pl.pallas_call(
        _glu_kernel,
        grid=(nb, m // tm),
        in_specs=[pl.BlockSpec((tm, k), lambda j, i: (i, 0)),
                  pl.BlockSpec((None, k, tn), lambda j, i: (layer, 0, j)),
                  pl.BlockSpec((None, k, tn), lambda j, i: (layer, 0, nb + j)),
                  pl.BlockSpec((1, tn), lambda j, i: (0, j)),
                  pl.BlockSpec((1, tn), lambda j, i: (0, nb + j))],
        out_specs=pl.BlockSpec((tm, tn), lambda j, i: (i, j)),
        out_shape=jax.ShapeDtypeStruct((m, n), BF16),
        scratch_shapes=[pltpu.VMEM((k, tn), BF16), pltpu.VMEM((k, tn), BF16)],
        compiler_params=_params(2),
    )(x, w_stack, w_stack, b2, b2)


CONV_HALO = 32
CONV_ROWS = 64
CONV_COLS = 128


def _conv_kernel(u_ref, halo_ref, w_ref, bdw_ref, g_ref, b_ref, o_ref, ext_ref, acc_ref, shift_ref, *, width, tm):
    i = pl.program_id(0)
    d = u_ref.shape[1]
    halo = halo_ref[...].astype(F32)
    ext_ref[0:CONV_HALO, :] = jnp.where(i > 0, halo, 0.0)
    ext_ref[CONV_HALO:, :] = u_ref[...].astype(F32)
    off = CONV_HALO - (width - 1)

    def col_chunk(c, carry):
        c0 = pl.multiple_of(c * CONV_COLS, CONV_COLS)
        w = w_ref[:, pl.ds(c0, CONV_COLS)]
        for b in range(8):
            rows = tm + (width - 1 - b) // 8 * 8
            shift_ref[b, 0:rows, :] = ext_ref[pl.ds(off + b, rows), pl.ds(c0, CONV_COLS)]
        for r0 in range(0, tm, CONV_ROWS):
            acc = jnp.zeros((CONV_ROWS, CONV_COLS), F32)
            for k in range(width):
                a8 = k - k % 8
                acc = acc + shift_ref[k % 8, r0 + a8:r0 + a8 + CONV_ROWS, :] * w[k:k + 1, :]
            acc_ref[r0:r0 + CONV_ROWS, pl.ds(c0, CONV_COLS)] = acc + bdw_ref[:, pl.ds(c0, CONV_COLS)]
        return carry

    lax.fori_loop(0, d // CONV_COLS, col_chunk, 0)
    (y,) = _ln_pieces([acc_ref[...]], [g_ref[...]], [b_ref[...]])
    o_ref[...] = (y * jax.nn.sigmoid(y)).astype(o_ref.dtype)


def _conv_ln_swish(u, w_dw, b_dw, ln_g, ln_b, *, tm=ROW_TILE):
    t, d = u.shape
    width = w_dw.shape[0]
    assert width - 1 <= CONV_HALO and t % tm == 0 and tm % CONV_HALO == 0 and d % CONV_COLS == 0
    hb = tm // CONV_HALO
    row = lambda a: a.reshape(1, d).astype(F32)
    return pl.pallas_call(
        functools.partial(_conv_kernel, width=width, tm=tm),
        grid=(t // tm,),
        in_specs=[pl.BlockSpec((tm, d), lambda i: (i, 0)),
                  pl.BlockSpec((CONV_HALO, d), lambda i: (jnp.maximum(i * hb - 1, 0), 0)),
                  pl.BlockSpec((width, d), lambda i: (0, 0)),
                  pl.BlockSpec((1, d), lambda i: (0, 0)),
                  pl.BlockSpec((1, d), lambda i: (0, 0)),
                  pl.BlockSpec((1, d), lambda i: (0, 0))],
        out_specs=pl.BlockSpec((tm, d), lambda i: (i, 0)),
        out_shape=jax.ShapeDtypeStruct((t, d), BF16),
        scratch_shapes=[pltpu.VMEM((tm + CONV_HALO, d), F32), pltpu.VMEM((tm, d), F32),
                        pltpu.VMEM((8, tm + (width - 1) // 8 * 8, CONV_COLS), F32)],
        compiler_params=_params(1),
    )(u, u, w_dw.astype(F32), row(b_dw), row(ln_g), row(ln_b))


POOL_HALO = 32


def _pool_kernel(x_ref, halo_ref, w_ref, scale_ref, o_ref, ext_ref, sum0_ref, sum1_ref, *, tm, gw):
    i = pl.program_id(0)
    ext_ref[0:POOL_HALO, :] = jnp.where(i > 0, halo_ref[...], 0.0)
    ext_ref[POOL_HALO:, :] = x_ref[...]
    n = tm + POOL_HALO
    t_abs = i * tm + lax.broadcasted_iota(I32, (tm, 1), 0)
    level_bufs = (sum0_ref, sum1_ref)
    for g, win in enumerate(POOL_WINDOWS):
        cols = slice(g * gw, (g + 1) * gw)
        read = lambda start, size: ext_ref[pl.ds(start, size), cols]
        for l in range(1, win.bit_length()):
            lo = 8 * l
            buf = level_bufs[l % 2]
            buf[lo:n, :] = read(lo, n - lo) + read(lo - (1 << (l - 1)), n - lo)
            read = lambda start, size, buf=buf: buf[pl.ds(start, size), :]
        acc = read(POOL_HALO, tm)
        cnt = jnp.minimum(t_abs + 1, win).astype(F32)
        pooled = acc / cnt - ext_ref[POOL_HALO:POOL_HALO + tm, cols]
        y = jnp.dot(pooled.astype(BF16), w_ref[g], preferred_element_type=F32)
        o_ref[:, cols] = (y * scale_ref[:, cols]).astype(o_ref.dtype)


def _multiscale_pool(x, w_pool, scale, *, tm=ROW_TILE):
    t, d = x.shape
    ng, gw, _ = w_pool.shape
    assert ng == len(POOL_WINDOWS) and t % tm == 0 and tm % POOL_HALO == 0
    assert all(w & (w - 1) == 0 for w in POOL_WINDOWS) and 8 * (max(POOL_WINDOWS).bit_length() - 1) <= POOL_HALO
    hb = tm // POOL_HALO
    return pl.pallas_call(
        functools.partial(_pool_kernel, tm=tm, gw=gw),
        grid=(t // tm,),
        in_specs=[pl.BlockSpec((tm, d), lambda i: (i, 0)),
                  pl.BlockSpec((POOL_HALO, d), lambda i: (jnp.maximum(i * hb - 1, 0), 0)),
                  pl.BlockSpec((ng, gw, gw), lambda i: (0, 0, 0)),
                  pl.BlockSpec((1, d), lambda i: (0, 0))],
        out_specs=pl.BlockSpec((tm, d), lambda i: (i, 0)),
        out_shape=jax.ShapeDtypeStruct((t, d), BF16),
        scratch_shapes=[pltpu.VMEM((tm + POOL_HALO, d), F32), pltpu.VMEM((tm + POOL_HALO, gw), F32),
                        pltpu.VMEM((tm + POOL_HALO, gw), F32)],
        compiler_params=_params(1),
    )(x, x, w_pool, scale.reshape(1, d).astype(F32))


def _post_kernel(x_ref, f_ref, g0_ref, b0_ref, g1_ref, b1_ref, wq_ref, kv_ref, wo_ref, wr_ref, br_ref,
                 x2_ref, x2p_ref, route_ref, counts_ref, *, n_groups, per_group):
    d = x_ref.shape[1]
    half = d // 2
    n_mem, inner2 = kv_ref.shape
    inner = inner2 // 2
    dh = inner // MEM_HEADS
    n_slabs = half // LANES

    @pl.when(pl.program_id(0) == 0)
    def _():
        counts_ref[...] = jnp.zeros(counts_ref.shape, counts_ref.dtype)

    tm = x_ref.shape[0]
    rc = tm // POST_ROW_CHUNKS
    for r0 in range(0, tm, rc):
        _post_rows(slice(r0, r0 + rc), r0 * n_slabs, x_ref, f_ref, g0_ref, b0_ref, g1_ref, b1_ref, wq_ref, kv_ref,
                   wo_ref, wr_ref, br_ref, x2_ref, x2p_ref, route_ref, counts_ref, n_groups, per_group)


def _post_rows(rows, slab_row0, x_ref, f_ref, g0_ref, b0_ref, g1_ref, b1_ref, wq_ref, kv_ref, wo_ref, wr_ref, br_ref,
               x2_ref, x2p_ref, route_ref, counts_ref, n_groups, per_group):
    d = x_ref.shape[1]
    half = d // 2
    n_slabs = half // LANES
    inner = kv_ref.shape[1] // 2
    dh = inner // MEM_HEADS
    y = DEEPNORM_ALPHA * x_ref[rows, :] + f_ref[rows, :].astype(F32)
    (x1,) = _ln_pieces([y], [g0_ref[...]], [b0_ref[...]])
    q = jnp.dot(x1.astype(BF16), wq_ref[...], preferred_element_type=F32) * (dh ** -0.5)
    heads = []
    for h in range(MEM_HEADS):
        qh = q[:, h * dh:(h + 1) * dh].astype(BF16)
        kh = kv_ref[:, h * dh:(h + 1) * dh]
        vh = kv_ref[:, inner + h * dh:inner + (h + 1) * dh]
        s = lax.dot_general(qh, kh, (((1,), (1,)), ((), ())), preferred_element_type=F32)
        p = jnp.exp(s - jnp.max(s, axis=-1, keepdims=True))
        l = jnp.sum(p, axis=-1, keepdims=True)
        heads.append((jnp.dot(p.astype(BF16), vh, preferred_element_type=F32) / l).astype(BF16))
    o = jnp.concatenate(heads, axis=1)
    c = jnp.dot(o, wo_ref[...], preferred_element_type=F32)
    (x2,) = _ln_pieces([DEEPNORM_ALPHA * x1 + c], [g1_ref[...]], [b1_ref[...]])
    x2_ref[rows, :] = x2
    _store_slabs(x2p_ref, 0, _pack_halves(x2[:, :half], x2[:, half:]), n_slabs=n_slabs, row0=slab_row0)

    lg = jnp.dot(x2.astype(BF16), wr_ref[...], preferred_element_type=F32) + br_ref[...]
    lane = lax.broadcasted_iota(I32, lg.shape, 1).astype(F32)
    far = float(LANES)

    def first_lane_of_max(v):
        m = jnp.max(v, axis=-1, keepdims=True)
        return m, jnp.min(jnp.where(v == m, lane, far), axis=-1, keepdims=True)

    gl = jnp.where(lane < n_groups, lg, NEG_BIG)
    gm, gsel = first_lane_of_max(gl)
    gw = 1.0 / jnp.sum(jnp.exp(gl - gm), axis=-1, keepdims=True)
    lo = n_groups + gsel * per_group
    el = jnp.where((lane >= lo) & (lane < lo + per_group), lg, NEG_BIG)
    m1, i1 = first_lane_of_max(el)
    m2, i2 = first_lane_of_max(jnp.where(lane == i1, NEG_BIG, el))
    e2 = jnp.exp(m2 - m1)
    den = 1.0 + e2
    id1 = i1 - n_groups
    id2 = i2 - n_groups

    hot1 = lane == id1
    hot2 = lane == id2
    hot = jnp.where(hot1 | hot2, 1.0, 0.0)
    tm = lg.shape[0]
    strict_lower = (lax.broadcasted_iota(I32, (tm, tm), 0) > lax.broadcasted_iota(I32, (tm, tm), 1))
    before = jnp.dot(jnp.where(strict_lower, 1.0, 0.0).astype(BF16), hot.astype(BF16),
                     preferred_element_type=F32) + counts_ref[...]
    rank1 = jnp.sum(jnp.where(hot1, before, 0.0), axis=-1, keepdims=True)
    rank2 = jnp.sum(jnp.where(hot2, before, 0.0), axis=-1, keepdims=True)
    counts_ref[...] = counts_ref[...] + jnp.sum(hot, axis=0, keepdims=True)

    route = jnp.where(lane == 0, id1,
                      jnp.where(lane == 1, id2,
                                jnp.where(lane == 2, gw / den,
                                          jnp.where(lane == 3, gw * e2 / den,
                                                    jnp.where(lane == 4, rank1,
                                                              jnp.where(lane == 5, rank2, 0.0))))))
    route_ref[rows, :] = route


def _post_block(x, f, ln_g, ln_b, w_q, kv, w_o, w_r, b_r, n_groups, per_group, *, tm=ROW_TILE):
    t, d = x.shape
    inner = w_q.shape[1]
    n_mem = kv.shape[0]
    n_slabs = d // 2 // LANES
    row = lambda a: a.reshape(1, d).astype(F32)
    full = lambda shape: pl.BlockSpec(shape, lambda i: (0,) * len(shape))
    tile = lambda w: pl.BlockSpec((tm, w), lambda i: (i, 0))
    return pl.pallas_call(
        functools.partial(_post_kernel, n_groups=n_groups, per_group=per_group),
        grid=(t // tm,),
        in_specs=[tile(d), tile(d), full((1, d)), full((1, d)), full((1, d)), full((1, d)),
                  full((d, inner)), full((n_mem, 2 * inner)), full((inner, d)), full((d, LANES)), full((1, LANES))],
        out_specs=[tile(d), pl.BlockSpec((tm * n_slabs, LANES), lambda i: (i, 0)), tile(LANES), full((1, LANES))],
        out_shape=[jax.ShapeDtypeStruct((t, d), F32), jax.ShapeDtypeStruct((t * n_slabs, LANES), U32),
                   jax.ShapeDtypeStruct((t, LANES), F32), jax.ShapeDtypeStruct((1, LANES), F32)],
        compiler_params=_params(1),
    )(x, f, row(ln_g[0]), row(ln_b[0]), row(ln_g[1]), row(ln_b[1]), w_q, kv, w_o, w_r, b_r)


MOE_CAST_ROWS = 128
MOE_WEIGHT_COPIES = 8
WEIGHT_DMA_PRIORITY = 1


def _moe_kernel(be_ref, first_ref, next_ref, src_ref, nu_ref, xg_hbm, wg_hbm, wu_hbm, wd_hbm, yb_ref,
                xbuf0, xbuf1, stage_g, stage_u, stage_d, wgu_ref, wd_ref, part_ref, sem, wsem, *, blk, layer):
    i = pl.program_id(0)
    n_used = nu_ref[0]
    bufs = (xbuf0, xbuf1)
    n_slabs = yb_ref.shape[0] // blk
    half = n_slabs * LANES
    de = wd_ref.shape[0]
    d = wd_ref.shape[1]

    def weight_copies(e):
        copies = []
        for k, (w_hbm, stage) in enumerate(((wg_hbm, stage_g), (wu_hbm, stage_u), (wd_hbm, stage_d))):
            rows = stage.shape[0] // MOE_WEIGHT_COPIES
            for c in range(MOE_WEIGHT_COPIES):
                piece = pl.ds(c * rows, rows)
                copies.append(pltpu.make_async_copy(w_hbm.at[layer, e, piece, :], stage.at[piece, :], wsem.at[k]))
        return copies

    @pl.when(i == 0)
    def _():
        for c in weight_copies(be_ref[0]):
            c.start(priority=WEIGHT_DMA_PRIORITY)

    @pl.when((i < n_used) & (first_ref[i] == 1))
    def _():
        for c in weight_copies(be_ref[i]):
            c.wait()

        def cast_up(r, carry):
            rows = pl.ds(pl.multiple_of(r * MOE_CAST_ROWS, MOE_CAST_ROWS), MOE_CAST_ROWS)
            wgu_ref[rows, 0:de] = stage_g[rows, :].astype(BF16)
            wgu_ref[rows, de:2 * de] = stage_u[rows, :].astype(BF16)
            return carry

        def cast_down(r, carry):
            rows = pl.ds(pl.multiple_of(r * MOE_CAST_ROWS, MOE_CAST_ROWS), MOE_CAST_ROWS)
            for c0 in range(0, d, 1024):
                wd_ref[rows, c0:c0 + 1024] = stage_d[rows, c0:c0 + 1024].astype(BF16)
            return carry

        lax.fori_loop(0, d // MOE_CAST_ROWS, cast_up, 0)
        lax.fori_loop(0, de // MOE_CAST_ROWS, cast_down, 0)

        @pl.when(next_ref[i] >= 0)
        def _():
            for c in weight_copies(next_ref[i]):
                c.start(priority=WEIGHT_DMA_PRIORITY)

    def row_copy(j, r, s):
        src = pl.multiple_of(src_ref[j * blk + r], n_slabs)
        return pltpu.make_async_copy(xg_hbm.at[pl.ds(src, n_slabs), :],
                                     bufs[s].at[pl.ds(r * GATHER_PITCH, n_slabs), :], sem.at[s])

    def wait_block(s):
        pltpu.make_async_copy(xg_hbm.at[pl.ds(0, blk * n_slabs), :], bufs[s].at[pl.ds(0, blk * n_slabs), :],
                              sem.at[s]).wait()

    @pl.when(i == 0)
    def _():
        def body(r, carry):
            src = pl.multiple_of(src_ref[r], n_slabs)
            dst = pl.multiple_of(r * GATHER_PITCH, 8)
            pltpu.make_async_copy(xg_hbm.at[pl.ds(src, n_slabs), :], xbuf0.at[pl.ds(dst, n_slabs), :],
                                  sem.at[0]).start()
            return carry
        lax.fori_loop(0, blk, body, 0)

    def expert_block(s):
        wait_block(s)
        los, his = [], []
        for q in range(n_slabs):
            lo_q, hi_q = _unpack_halves(_load_slab_column(bufs[s], q, blk))
            los.append(lo_q.astype(BF16))
            his.append(hi_q.astype(BF16))
        lo = jnp.concatenate(los, axis=1)
        hi = jnp.concatenate(his, axis=1)
        for r in range(blk):
            row_copy(i + 1, r, 1 - s).start(priority=r % 2)
        wu = 2 * de // MOE_UP_PIECES
        for c in range(MOE_UP_PIECES):
            cols = slice(c * wu, (c + 1) * wu)
            part_ref[:, cols] = jnp.dot(lo, wgu_ref[0:half, cols], preferred_element_type=F32)
        au = []
        for c in range(MOE_UP_PIECES):
            cols = slice(c * wu, (c + 1) * wu)
            au.append(part_ref[:, cols] + jnp.dot(hi, wgu_ref[half:, cols], preferred_element_type=F32))
        au = jnp.concatenate(au, axis=1)
        a = au[:, 0:de]
        u = au[:, de:]
        hmid = (a * jax.nn.sigmoid(a) * u).astype(BF16)
        wdn = half // MOE_DOWN_PIECES
        for c in range(MOE_DOWN_PIECES):
            y_lo = jnp.dot(hmid, wd_ref[:, c * wdn:(c + 1) * wdn], preferred_element_type=F32)
            y_hi = jnp.dot(hmid, wd_ref[:, half + c * wdn:half + (c + 1) * wdn], preferred_element_type=F32)
            _store_slabs(yb_ref, c * (wdn // LANES), _pack_halves(y_lo, y_hi))

    for s in range(2):
        pl.when((i < n_used) & (i % 2 == s))(functools.partial(expert_block, s))
        pl.when((i == n_used) & (i % 2 == s))(functools.partial(wait_block, s))

    @pl.when(i >= n_used)
    def _():
        yb_ref[...] = jnp.zeros(yb_ref.shape, yb_ref.dtype)


def _moe_experts(xg, plan, w_gate, w_up, w_down, layer, *, blk=MOE_BLOCK):
    n_slots = plan.slot_src.shape[0]
    nb = n_slots // blk
    de, d = w_down.shape[2:]
    n_slabs = d // 2 // LANES
    assert n_slabs <= GATHER_PITCH and GATHER_PITCH % 8 == 0
    assert d % MOE_CAST_ROWS == 0 and de % MOE_CAST_ROWS == 0 and d % 1024 == 0
    buf = pltpu.VMEM((blk * GATHER_PITCH, LANES), U32)
    any_spec = pl.BlockSpec(memory_space=pl.ANY)
    grid_spec = pltpu.PrefetchScalarGridSpec(
        num_scalar_prefetch=5,
        grid=(nb,),
        in_specs=[any_spec, any_spec, any_spec, any_spec],
        out_specs=pl.BlockSpec((blk * n_slabs, LANES), lambda i, *_: (i, 0)),
        scratch_shapes=[buf, buf,
                        pltpu.VMEM((d, de), F32), pltpu.VMEM((d, de), F32), pltpu.VMEM((de, d), F32),
                        pltpu.VMEM((d, 2 * de), BF16), pltpu.VMEM((de, d), BF16), pltpu.VMEM((blk, 2 * de), F32),
                        pltpu.SemaphoreType.DMA((2,)), pltpu.SemaphoreType.DMA((3,))],
    )
    return pl.pallas_call(
        functools.partial(_moe_kernel, blk=blk, layer=layer),
        grid_spec=grid_spec,
        out_shape=jax.ShapeDtypeStruct((n_slots * n_slabs, LANES), U32),
        compiler_params=_params(1),
    )(plan.blk_expert, plan.seg_first, plan.seg_next, plan.slot_src, plan.n_used, xg, w_gate, w_up, w_down)


def _combine_kernel(src_ref, x2_ref, route_ref, g_ref, b_ref, yb_hbm, x3_ref, *rest, tm, with_bf16):
    if with_bf16:
        x3b_ref, ybuf0, ybuf1, sem = rest
    else:
        x3b_ref = None
        ybuf0, ybuf1, sem = rest
    i = pl.program_id(0)
    n = pl.num_programs(0)
    bufs = (ybuf0, ybuf1)
    d = x2_ref.shape[1]
    half = d // 2
    n_slabs = half // LANES

    def wait_tile(s):
        for k in range(TOP_K):
            pltpu.make_async_copy(yb_hbm.at[pl.ds(0, tm * n_slabs), :], bufs[s].at[k, pl.ds(0, tm * n_slabs), :],
                                  sem.at[s]).wait()

    @pl.when(i == 0)
    def _():
        def body(r, carry):
            dst = pl.multiple_of(r * GATHER_PITCH, 8)
            for k in range(TOP_K):
                src = pl.multiple_of(src_ref[TOP_K * r + k], n_slabs)
                pltpu.make_async_copy(yb_hbm.at[pl.ds(src, n_slabs), :], ybuf0.at[k, pl.ds(dst, n_slabs), :],
                                      sem.at[0]).start()
            return carry

        lax.fori_loop(0, tm, body, 0)

    def tile(s):
        wait_tile(s)
        gates = [route_ref[:, TOP_K + k:TOP_K + k + 1] for k in range(TOP_K)]
        y_lo, y_hi = [], []
        for q in range(n_slabs):
            lo = hi = None
            for k in range(TOP_K):
                lo_k, hi_k = _unpack_halves(bufs[s][k, pl.ds(q, tm, stride=GATHER_PITCH), :])
                lo = lo_k * gates[k] if lo is None else lo + lo_k * gates[k]
                hi = hi_k * gates[k] if hi is None else hi + hi_k * gates[k]
            y_lo.append(DEEPNORM_ALPHA * x2_ref[:, q * LANES:(q + 1) * LANES] + lo)
            y_hi.append(DEEPNORM_ALPHA * x2_ref[:, half + q * LANES:half + (q + 1) * LANES] + hi)
        base = (i + 1) * tm * TOP_K
        for r in range(tm):
            for k in range(TOP_K):
                src = pl.multiple_of(src_ref[base + TOP_K * r + k], n_slabs)
                pltpu.make_async_copy(yb_hbm.at[pl.ds(src, n_slabs), :],
                                      bufs[1 - s].at[k, pl.ds(r * GATHER_PITCH, n_slabs), :],
                                      sem.at[1 - s]).start(priority=k % 2)
        cols = [slice(c * LANES, (c + 1) * LANES) for c in range(2 * n_slabs)]
        out = _ln_pieces(y_lo + y_hi, [g_ref[:, c] for c in cols], [b_ref[:, c] for c in cols])
        for c, o in zip(cols, out):
            x3_ref[:, c] = o
            if with_bf16:
                x3b_ref[:, c] = o.astype(BF16)

    for s in range(2):
        pl.when(i % 2 == s)(functools.partial(tile, s))
        pl.when((i == n - 1) & (i % 2 == s))(functools.partial(wait_tile, 1 - s))


def _moe_combine(x2, route, yb, slot_of, ln_g, ln_b, *, with_bf16, tm=ROW_TILE):
    t, d = x2.shape
    half = d // 2
    row = lambda a: a.reshape(1, d).astype(F32)
    grid_spec = pltpu.PrefetchScalarGridSpec(
        num_scalar_prefetch=1,
        grid=(t // tm,),
        in_specs=[pl.BlockSpec((tm, d), lambda i, dst: (i, 0)),
                  pl.BlockSpec((tm, LANES), lambda i, dst: (i, 0)),
                  pl.BlockSpec((1, d), lambda i, dst: (0, 0)),
                  pl.BlockSpec((1, d), lambda i, dst: (0, 0)),
                  pl.BlockSpec(memory_space=pl.ANY)],
        out_specs=[pl.BlockSpec((tm, d), lambda i, dst: (i, 0))] * (2 if with_bf16 else 1),
        scratch_shapes=[pltpu.VMEM((TOP_K, tm * GATHER_PITCH, LANES), U32),
                        pltpu.VMEM((TOP_K, tm * GATHER_PITCH, LANES), U32), pltpu.SemaphoreType.DMA((2,))],
    )
    return pl.pallas_call(
        functools.partial(_combine_kernel, tm=tm, with_bf16=with_bf16),
        grid_spec=grid_spec,
        out_shape=[jax.ShapeDtypeStruct((t, d), F32)] + ([jax.ShapeDtypeStruct((t, d), BF16)] if with_bf16 else []),
        compiler_params=_params(1),
    )(jnp.concatenate([slot_of, jnp.zeros((tm * TOP_K,), I32)]), x2, route, row(ln_g), row(ln_b), yb)


class DispatchPlan(NamedTuple):
    slot_of: jax.Array
    slot_src: jax.Array
    blk_expert: jax.Array
    seg_first: jax.Array
    seg_next: jax.Array
    n_used: jax.Array


def _dispatch_plan(route, counts, n_experts, blk, n_slabs):
    t = route.shape[0]
    ids = route[:, 0:TOP_K].astype(I32)
    rank = route[:, 2 * TOP_K:3 * TOP_K].astype(I32)
    counts = counts[0, :n_experts].astype(I32)
    padded = (counts + blk - 1) // blk * blk
    ends = jnp.cumsum(padded)
    starts = ends - padded
    expert_iota = jnp.arange(n_experts, dtype=I32)
    seg_start = jnp.sum(jnp.where(ids[:, :, None] == expert_iota, starts, 0), axis=-1)
    nb = (t * TOP_K) // blk + n_experts
    dest = jnp.clip(seg_start + rank, 0, nb * blk - 1).reshape(-1)
    flat_src = jnp.arange(t * TOP_K, dtype=I32) // TOP_K * n_slabs
    slot_src = jnp.zeros((nb * blk,), I32).at[dest].set(flat_src, unique_indices=True)
    blk_start = jnp.arange(nb, dtype=I32) * blk
    blk_expert = jnp.minimum(jnp.sum((ends[None, :] <= blk_start[:, None]).astype(I32), axis=1), n_experts - 1)
    n_used = ends[-1] // blk
    blk_iota = jnp.arange(nb, dtype=I32)
    used = blk_iota < n_used
    prev_expert = jnp.concatenate([jnp.full((1,), -1, I32), blk_expert[:-1]])
    seg_first = (used & (blk_expert != prev_expert)).astype(I32)
    later_segment = (blk_iota[None, :] > blk_iota[:, None]) & used[None, :] & (blk_expert[None, :] != blk_expert[:, None])
    seg_next = jnp.where(jnp.any(later_segment, axis=1), blk_expert[jnp.argmax(later_segment, axis=1)], -1).astype(I32)
    return DispatchPlan(dest * n_slabs, slot_src, blk_expert.astype(I32), seg_first, seg_next,
                        n_used.astype(I32).reshape(1))


def kernel(x, mem, attn_w_qkv, attn_w_o, attn_rel_bias, conv_w_in, conv_b_in, conv_w_dw, conv_b_dw, conv_ln_g,
           conv_ln_b, conv_w_out, conv_b_out, pool_w, pool_scale, mem_w_q, mem_w_kv, mem_w_o, moe_w_group,
           moe_b_group, moe_w_router, moe_b_router, moe_w_gate, moe_w_up, moe_w_down, ln_g, ln_b):
    batch, seq, d = x.shape
    assert batch == 1 and ln_g.shape[0] == DEPTH
    n_groups = moe_w_group.shape[2]
    per_group = moe_w_router.shape[3]
    n_experts = n_groups * per_group
    assert n_groups + n_experts <= LANES

    xf = x.reshape(seq, d)
    xb = xf.astype(BF16)
    memb = mem.reshape(mem.shape[1], d).astype(BF16)

    for i in range(DEPTH):
        kind = i % N_MIXERS
        j = i // N_MIXERS
        if kind == 0:
            qkv = _matmul(xb, attn_w_qkv, j, tn=ATTN_HEADS_PER_STEP * (d // ATTN_HEADS), column_blocked=True)
            o = _chunked_attention(qkv, attn_rel_bias[j], d)
            f = _matmul(o, attn_w_o, j)
        elif kind == 1:
            u = _glu_matmul(xb, conv_w_in, j, conv_b_in[j])
            v = _conv_ln_swish(u, conv_w_dw[j], conv_b_dw[j], conv_ln_g[j], conv_ln_b[j])
            f = _matmul(v, conv_w_out, j, conv_b_out[j])
        else:
            f = _multiscale_pool(xf, pool_w[j].astype(BF16), pool_scale[j])

        kv = _matmul(memb, mem_w_kv, i, tm=memb.shape[0])
        w_r = jnp.concatenate([moe_w_group[i], moe_w_router[i].transpose(1, 0, 2).reshape(d, n_experts)], axis=1)
        w_r = jnp.pad(w_r, ((0, 0), (0, LANES - w_r.shape[1]))).astype(BF16)
        b_r = jnp.concatenate([moe_b_group[i], moe_b_router[i].reshape(n_experts)])
        b_r = jnp.pad(b_r, (0, LANES - b_r.shape[0])).reshape(1, LANES).astype(F32)
        x2, x2p, route, counts = _post_block(xf, f, ln_g[i], ln_b[i], mem_w_q[i].astype(BF16), kv,
                                             mem_w_o[i].astype(BF16), w_r, b_r, n_groups, per_group)

        plan = _dispatch_plan(route, counts, n_experts, MOE_BLOCK, d // 2 // LANES)
        yb = _moe_experts(x2p, plan, moe_w_gate, moe_w_up, moe_w_down, i)
        next_wants_bf16 = i + 1 < DEPTH and (i + 1) % N_MIXERS != 2
        outs = _moe_combine(x2, route, yb, plan.slot_of, ln_g[i, 2], ln_b[i, 2], with_bf16=next_wants_bf16)
        xf, xb = (outs[0], outs[1]) if next_wants_bf16 else (outs[0], None)

    return xf.reshape(batch, seq, d)
```
